```python
import math
import jax, jax.numpy as jnp
from jax import lax
import numpy as np

D_MODEL = 2048
BATCH = 1
SEQ = 8192
DEPTH = 2
DEC_BATCH = 128
DEC_SEQ = 1
PAST_LEN = 2048
PAGE_SIZE = 128

N_A = DEPTH // 2
N_B = DEPTH - N_A

EXPAND = 2
D_INNER = EXPAND * D_MODEL
SSM_HEAD_DIM = 64
SSM_HEADS = D_INNER // SSM_HEAD_DIM
SSM_GROUPS = 8
D_STATE = 128
CONV_WIDTH = 4
CONV_DIM = D_INNER + 2 * SSM_GROUPS * D_STATE
IN_PROJ_DIM = D_INNER + CONV_DIM + SSM_HEADS
SSD_CHUNK = 128
RMS_EPS = 1e-5

HEAD_DIM = 128
N_HEADS_B = D_MODEL // HEAD_DIM
DILATION_GROUPS = ((128, 1), (512, 4), (2048, 16))
N_GROUPS_B = len(DILATION_GROUPS)
WINDOW_MAX = max(w for w, _ in DILATION_GROUPS)
Q_DIM = N_GROUPS_B * N_HEADS_B * HEAD_DIM
KV_DIM = N_HEADS_B * HEAD_DIM
NUM_BUCKETS = 32
MAX_DISTANCE = WINDOW_MAX

D_FF = -(-8 * D_MODEL // (3 * 256)) * 256

ALPHA = (2 * DEPTH) ** 0.25
BETA = (8 * DEPTH) ** -0.25
LN_EPS = 1e-5

kernel_name = "yoco_mamba2_dilated_swa_decoder_step"


def layer_norm(x, g, b):
    xf = x.astype(jnp.float32)
    mu = xf.mean(-1, keepdims=True)
    var = jnp.square(xf - mu).mean(-1, keepdims=True)
    return ((xf - mu) * lax.rsqrt(var + LN_EPS) * g + b).astype(x.dtype)


def ada_params(c, w, b, n):
    m = jax.nn.silu(c) @ w + b
    return [t[:, None, :] for t in jnp.split(m, n, axis=-1)]


def swiglu(u, w_gate, w_up, w_down):
    return (jax.nn.silu(u @ w_gate) * (u @ w_up)) @ w_down


def causal_conv_silu(ext, conv_w, conv_b, t):
    acc = conv_b + conv_w[0] * ext[:, 0:t]
    for i in range(1, CONV_WIDTH):
        acc = acc + conv_w[i] * ext[:, i:i + t]
    return jax.nn.silu(acc)


def ssd_chunked(x, dt, a, bm, cm):
    b, s, h, p = x.shape
    nc = s // SSD_CHUNK
    hpg = h // SSM_GROUPS
    f32 = jnp.float32
    xc = x.reshape(b, nc, SSD_CHUNK, SSM_GROUPS, hpg, p).astype(f32)
    dtc = dt.reshape(b, nc, SSD_CHUNK, SSM_GROUPS, hpg)
    bc = bm.reshape(b, nc, SSD_CHUNK, SSM_GROUPS, D_STATE).astype(f32)
    cc = cm.reshape(b, nc, SSD_CHUNK, SSM_GROUPS, D_STATE).astype(f32)
    da_cs = jnp.cumsum(dtc * a.reshape(SSM_GROUPS, hpg), axis=2)
    causal = jnp.tril(jnp.ones((SSD_CHUNK, SSD_CHUNK), bool))[:, :, None, None]
    diff = da_cs[:, :, :, None] - da_cs[:, :, None]
    lmat = jnp.exp(jnp.where(causal, diff, -jnp.inf))
    cb = jnp.einsum('bclgn,bcsgn->bclsg', cc, bc)
    xdt = xc * dtc[..., None]
    y_diag = jnp.einsum('bclsgk,bcsgkp->bclgkp', cb[..., None] * lmat, xdt)
    decay_to_end = jnp.exp(da_cs[:, :, -1:] - da_cs)
    chunk_states = jnp.einsum('bclgn,bclgkp->bcgkpn', bc, xdt * decay_to_end[..., None])
    chunk_decay = jnp.exp(da_cs[:, :, -1])

    def step(hprev, inp):
        st, dec = inp
        return hprev * dec[..., None, None] + st, hprev

    h0 = jnp.zeros((b, SSM_GROUPS, hpg, p, D_STATE), f32)
    h_fin, h_prev = lax.scan(step, h0, (chunk_states.swapaxes(0, 1), chunk_decay.swapaxes(0, 1)))
    h_prev = h_prev.swapaxes(0, 1)
    y_off = jnp.einsum('bclgn,bcgkpn->bclgkp', cc, h_prev) * jnp.exp(da_cs)[..., None]
    y = (y_diag + y_off).reshape(b, s, h, p)
    return y, h_fin.reshape(b, h, p, D_STATE)


def ssd_recurrent(x, dt, a, bm, cm, h0):
    hpg = SSM_HEADS // SSM_GROUPS
    f32 = jnp.float32

    def step(h, inp):
        xt, dtt, bt, ct = inp
        bh = jnp.repeat(bt, hpg, axis=1).astype(f32)
        ch = jnp.repeat(ct, hpg, axis=1).astype(f32)
        h = h * jnp.exp(dtt * a)[..., None, None] + (dtt[..., None] * xt.astype(f32))[..., None] * bh[:, :, None, :]
        return h, jnp.einsum('bhpn,bhn->bhp', h, ch)

    h_fin, ys = lax.scan(step, h0.astype(f32),
                         (x.swapaxes(0, 1), dt.swapaxes(0, 1), bm.swapaxes(0, 1), cm.swapaxes(0, 1)))
    return ys.swapaxes(0, 1), h_fin


def gated_rmsnorm(y, z, w):
    g = y * jax.nn.silu(z.astype(jnp.float32))
    shp = g.shape
    g = g.reshape(shp[:-1] + (SSM_GROUPS, D_INNER // SSM_GROUPS))
    g = g * lax.rsqrt(jnp.mean(g * g, -1, keepdims=True) + RMS_EPS)
    return g.reshape(shp) * w


def mamba_mixer(u, conv_state, ssm_state, w_in, conv_w, conv_b, dt_bias, a_log, d_skip,
                norm_w, w_out, prompt):
    b, t, _ = u.shape
    z, xbc, dt_raw = jnp.split(u @ w_in, [D_INNER, D_INNER + CONV_DIM], axis=-1)
    if prompt:
        ext = jnp.pad(xbc, ((0, 0), (CONV_WIDTH - 1, 0), (0, 0)))
    else:
        ext = jnp.concatenate([conv_state.astype(xbc.dtype), xbc], axis=1)
    new_conv = ext[:, ext.shape[1] - (CONV_WIDTH - 1):]
    xbc = causal_conv_silu(ext, conv_w, conv_b, t)
    xs, bm, cm = jnp.split(xbc, [D_INNER, D_INNER + SSM_GROUPS * D_STATE], axis=-1)
    xs = xs.reshape(b, t, SSM_HEADS, SSM_HEAD_DIM)
    bm = bm.reshape(b, t, SSM_GROUPS, D_STATE)
    cm = cm.reshape(b, t, SSM_GROUPS, D_STATE)
    dt = jax.nn.softplus(dt_raw.astype(jnp.float32) + dt_bias.astype(jnp.float32))
    a = -jnp.exp(a_log.astype(jnp.float32))
    if prompt:
        y, h = ssd_chunked(xs, dt, a, bm, cm)
    else:
        y, h = ssd_recurrent(xs, dt, a, bm, cm, ssm_state)
    y = y + d_skip.astype(jnp.float32)[:, None] * xs.astype(jnp.float32)
    y = gated_rmsnorm(y.reshape(b, t, D_INNER), z, norm_w)
    return y.astype(u.dtype) @ w_out, new_conv, h


def rel_bucket(dist):
    max_exact = NUM_BUCKETS // 2
    df = jnp.maximum(dist, max_exact).astype(jnp.float32)
    large = max_exact + (jnp.log(df / max_exact) / math.log(MAX_DISTANCE / max_exact)
                         * (NUM_BUCKETS - max_exact)).astype(jnp.int32)
    return jnp.where(dist < max_exact, dist, jnp.minimum(large, NUM_BUCKETS - 1))


def dilated_group_prompt(q, k, v, bias_tab, window, dilation):
    b, s, h, e = q.shape
    n = window // dilation
    unit = n * dilation
    lp = -(-s // unit) * unit
    m = lp // dilation
    nb = m // n
    pad = ((0, 0), (0, lp - s), (0, 0), (0, 0))

    def to_res(t):
        t = jnp.pad(t, pad).reshape(b, m, dilation, h, e).transpose(0, 2, 1, 3, 4)
        return t.reshape(b, dilation, nb, n, h, e)

    def with_prev(t):
        prev = jnp.concatenate([jnp.zeros_like(t[:, :, :1]), t[:, :, :-1]], axis=2)
        return jnp.concatenate([prev, t], axis=3)

    qr = to_res(q)
    kc = with_prev(to_res(k))
    vc = with_prev(to_res(v))
    uu = jnp.arange(n)[:, None]
    aa = jnp.arange(2 * n)[None, :]
    j = uu + n - aa
    blk = jnp.arange(nb)[:, None, None]
    valid = ((j >= 0) & (j <= n))[None] & ((blk > 0) | (aa[None] >= n))
    bias = bias_tab[rel_bucket(jnp.clip(j, 0, n) * dilation)].transpose(2, 0, 1).astype(jnp.float32)
    sc = jnp.einsum('brcuhe,brcahe->brchua', qr, kc).astype(jnp.float32) * (e ** -0.5) + bias
    sc = jnp.where(valid[:, None], sc, -jnp.inf)
    mx = sc.max(-1, keepdims=True)
    pr = jnp.exp(sc - mx)
    den = pr.sum(-1, keepdims=True)
    o = jnp.einsum('brchua,brcahe->brcuhe', (pr / den).astype(v.dtype), vc)
    lse = (mx + jnp.log(den))[..., 0]
    o = o.reshape(b, dilation, m, h, e).transpose(0, 2, 1, 3, 4).reshape(b, lp, h, e)[:, :s]
    lse = lse.transpose(0, 1, 2, 4, 3).reshape(b, dilation, m, h).transpose(0, 2, 1, 3)
    lse = lse.reshape(b, lp, h)[:, :s]
    return o, lse


def dilated_group_decode(q, k_new, v_new, k_buf, v_buf, bias_tab, window, dilation):
    t = q.shape[1]
    n_buf = k_buf.shape[1]
    e = q.shape[-1]
    n = window // dilation
    jj = jnp.arange(n + 1)
    idx = n_buf + jnp.arange(t)[:, None] - dilation * jj[None]
    valid = idx >= 0
    from_new = (idx >= n_buf)[None, :, :, None, None]
    ib = jnp.clip(idx, 0, n_buf - 1)
    inew = jnp.clip(idx - n_buf, 0, t - 1)
    kg = jnp.where(from_new, k_new[:, inew], k_buf[:, ib].astype(k_new.dtype))
    vg = jnp.where(from_new, v_new[:, inew], v_buf[:, ib].astype(v_new.dtype))
    bias = bias_tab[rel_bucket(jj * dilation)].T.astype(jnp.float32)
    sc = jnp.einsum('bthe,btjhe->bhtj', q, kg).astype(jnp.float32) * (e ** -0.5) + bias[:, None, :]
    sc = jnp.where(valid[None, None], sc, -jnp.inf)
    mx = sc.max(-1, keepdims=True)
    pr = jnp.exp(sc - mx)
    den = pr.sum(-1, keepdims=True)
    o = jnp.einsum('bhtj,btjhe->bthe', (pr / den).astype(vg.dtype), vg)
    lse = (mx + jnp.log(den))[..., 0].transpose(0, 2, 1)
    return o, lse


def dilated_mixer(u, k, v, k_buf, v_buf, w_q, w_o, rel_bias, prompt):
    b, t, _ = u.shape
    q = (u @ w_q).reshape(b, t, N_GROUPS_B, N_HEADS_B, HEAD_DIM)
    outs, lses = [], []
    for g, (win, dil) in enumerate(DILATION_GROUPS):
        tab = rel_bias[:, g * N_HEADS_B:(g + 1) * N_HEADS_B]
        if prompt:
            o, l = dilated_group_prompt(q[:, :, g], k, v, tab, win, dil)
        else:
            o, l = dilated_group_decode(q[:, :, g], k, v, k_buf, v_buf, tab, win, dil)
        outs.append(o)
        lses.append(l)
    wts = jax.nn.softmax(jnp.stack(lses, axis=0), axis=0)
    o = jnp.einsum('gbth,gbthe->bthe', wts, jnp.stack(outs, axis=0).astype(jnp.float32))
    return o.astype(u.dtype).reshape(b, t, KV_DIM) @ w_o


def shared_kv(x, c, kv_w_ada, kv_b_ada, w_kv):
    shift, scale = ada_params(c, kv_w_ada, kv_b_ada, 2)
    kv = (x * (1 + scale) + shift) @ w_kv
    b, t, _ = x.shape
    k, v = jnp.split(kv, 2, axis=-1)
    return k.reshape(b, t, N_HEADS_B, HEAD_DIM), v.reshape(b, t, N_HEADS_B, HEAD_DIM)


def trunk(x, c, conv_state, ssm_state, k_buf, v_buf, p, prompt):
    new_conv, new_ssm = [], []
    k = v = None
    for l in range(DEPTH):
        sh1, sc1, g1, sh2, sc2, g2 = ada_params(c, p['w_ada'][l], p['b_ada'][l], 6)
        u = x * (1 + sc1) + sh1
        if l < N_A:
            y, cs, hs = mamba_mixer(
                u, None if prompt else conv_state[l], None if prompt else ssm_state[l],
                p['ssm_w_in'][l], p['ssm_conv_w'][l], p['ssm_conv_b'][l], p['ssm_dt_bias'][l],
                p['ssm_a_log'][l], p['ssm_d'][l], p['ssm_norm_w'][l], p['ssm_w_out'][l], prompt)
            new_conv.append(cs)
            new_ssm.append(hs)
        else:
            if l == N_A:
                k, v = shared_kv(x, c, p['kv_w_ada'], p['kv_b_ada'], p['w_kv'])
            y = dilated_mixer(u, k, v, k_buf, v_buf, p['attn_w_q'][l - N_A], p['attn_w_o'][l - N_A],
                              p['rel_bias'], prompt)
        x = layer_norm(ALPHA * x + (1 + g1) * y, p['ln_g'][l, 0], p['ln_b'][l, 0])
        u = x * (1 + sc2) + sh2
        y = swiglu(u, p['ffn_w_gate'][l], p['ffn_w_up'][l], p['ffn_w_down'][l])
        x = layer_norm(ALPHA * x + (1 + g2) * y, p['ln_g'][l, 1], p['ln_b'][l, 1])
    if prompt:
        n_keep = min(WINDOW_MAX, x.shape[1])
        k_rows, v_rows = k[:, x.shape[1] - n_keep:], v[:, x.shape[1] - n_keep:]
    else:
        k_rows, v_rows = k, v
    return x, jnp.stack(new_conv, axis=0), jnp.stack(new_ssm, axis=0), k_rows, v_rows


def setup_inputs(seed: int = 0) -> dict:
    key = jax.random.key(seed)
    ks = iter(jax.random.split(key, 40))
    f32 = jnp.float32

    def nrm(shape, scale):
        return jax.random.normal(next(ks), shape, f32) * scale

    n_buf = min(WINDOW_MAX, PAST_LEN)
    d = D_MODEL
    x_prompt = nrm((BATCH, SEQ, d), 1.0)
    x_sample = nrm((DEC_BATCH, DEC_SEQ, d), 1.0)
    state_conv = nrm((N_A, DEC_BATCH, CONV_WIDTH - 1, CONV_DIM), 1.0)
    state_ssm = nrm((N_A, DEC_BATCH, SSM_HEADS, SSM_HEAD_DIM, D_STATE), 0.3)
    cache_k = nrm((DEC_BATCH, n_buf, N_HEADS_B, HEAD_DIM), 1.0)
    cache_v = nrm((DEC_BATCH, n_buf, N_HEADS_B, HEAD_DIM), BETA)
    c_prompt = nrm((BATCH, d), 1.0)
    c_sample = nrm((DEC_BATCH, d), 1.0)
    w_ada = nrm((DEPTH, d, 6 * d), 0.1 * d ** -0.5)
    b_ada = nrm((DEPTH, 6 * d), 0.02)
    ln_g = 1.0 + nrm((DEPTH, 2, d), 0.02)
    ln_b = nrm((DEPTH, 2, d), 0.02)
    ffn_w_gate = nrm((DEPTH, d, D_FF), d ** -0.5)
    ffn_w_up = nrm((DEPTH, d, D_FF), d ** -0.5)
    ffn_w_down = nrm((DEPTH, D_FF, d), BETA * D_FF ** -0.5)
    ssm_w_in = nrm((N_A, d, IN_PROJ_DIM), d ** -0.5)
    ssm_conv_w = nrm((N_A, CONV_WIDTH, CONV_DIM), CONV_WIDTH ** -0.5)
    ssm_conv_b = nrm((N_A, CONV_DIM), 0.02)
    u_dt = jax.random.uniform(next(ks), (N_A, SSM_HEADS), f32)
    dt0 = jnp.exp(u_dt * (math.log(0.1) - math.log(0.001)) + math.log(0.001))
    ssm_dt_bias = dt0 + jnp.log(-jnp.expm1(-dt0))
    ssm_a_log = jnp.log(jax.random.uniform(next(ks), (N_A, SSM_HEADS), f32, 1.0, 16.0))
    ssm_d = 1.0 + nrm((N_A, SSM_HEADS), 0.02)
    ssm_norm_w = 1.0 + nrm((N_A, D_INNER), 0.02)
    ssm_w_out = nrm((N_A, D_INNER, d), BETA * D_INNER ** -0.5)
    kv_w_ada = nrm((d, 2 * d), 0.1 * d ** -0.5)
    kv_b_ada = nrm((2 * d,), 0.02)
    w_kv = jnp.concatenate([nrm((d, KV_DIM), d ** -0.5), nrm((d, KV_DIM), BETA * d ** -0.5)], axis=1)
    attn_w_q = nrm((N_B, d, Q_DIM), d ** -0.5)
    attn_w_o = nrm((N_B, KV_DIM, d), BETA * KV_DIM ** -0.5)
    rel_bias = nrm((NUM_BUCKETS, N_GROUPS_B * N_HEADS_B), 0.5)
    return {
        'x_prompt': x_prompt, 'x_sample': x_sample,
        'state_conv': state_conv, 'state_ssm': state_ssm,
        'cache_k': cache_k, 'cache_v': cache_v,
        'c_prompt': c_prompt, 'c_sample': c_sample,
        'w_ada': w_ada, 'b_ada': b_ada, 'ln_g': ln_g, 'ln_b': ln_b,
        'ffn_w_gate': ffn_w_gate, 'ffn_w_up': ffn_w_up, 'ffn_w_down': ffn_w_down,
        'ssm_w_in': ssm_w_in, 'ssm_conv_w': ssm_conv_w, 'ssm_conv_b': ssm_conv_b,
        'ssm_dt_bias': ssm_dt_bias, 'ssm_a_log': ssm_a_log, 'ssm_d': ssm_d,
        'ssm_norm_w': ssm_norm_w, 'ssm_w_out': ssm_w_out,
        'kv_w_ada': kv_w_ada, 'kv_b_ada': kv_b_ada, 'w_kv': w_kv,
        'attn_w_q': attn_w_q, 'attn_w_o': attn_w_o, 'rel_bias': rel_bias,
    }


def reference(x_prompt, x_sample, state_conv, state_ssm, cache_k, cache_v, c_prompt, c_sample,
              w_ada, b_ada, ln_g, ln_b, ffn_w_gate, ffn_w_up, ffn_w_down,
              ssm_w_in, ssm_conv_w, ssm_conv_b, ssm_dt_bias, ssm_a_log, ssm_d, ssm_norm_w, ssm_w_out,
              kv_w_ada, kv_b_ada, w_kv, attn_w_q, attn_w_o, rel_bias):
    p = dict(w_ada=w_ada, b_ada=b_ada, ln_g=ln_g, ln_b=ln_b,
             ffn_w_gate=ffn_w_gate, ffn_w_up=ffn_w_up, ffn_w_down=ffn_w_down,
             ssm_w_in=ssm_w_in, ssm_conv_w=ssm_conv_w, ssm_conv_b=ssm_conv_b,
             ssm_dt_bias=ssm_dt_bias, ssm_a_log=ssm_a_log, ssm_d=ssm_d,
             ssm_norm_w=ssm_norm_w, ssm_w_out=ssm_w_out,
             kv_w_ada=kv_w_ada, kv_b_ada=kv_b_ada, w_kv=w_kv,
             attn_w_q=attn_w_q, attn_w_o=attn_w_o, rel_bias=rel_bias)
    y_prompt, conv_prompt, ssm_prompt, k_prompt, v_prompt = trunk(
        x_prompt, c_prompt, None, None, None, None, p, True)
    y_sample, conv_sample, ssm_sample, k_sample, v_sample = trunk(
        x_sample, c_sample, state_conv, state_ssm, cache_k, cache_v, p, False)
    return (y_prompt, y_sample, conv_prompt, ssm_prompt, k_prompt, v_prompt,
            conv_sample, ssm_sample, k_sample, v_sample)
```

```python
import functools
import math

import jax
import jax.numpy as jnp
from jax import lax
from jax.experimental import pallas as pl
from jax.experimental.pallas import tpu as pltpu

F32 = jnp.float32
BF16 = jnp.bfloat16

D_MODEL = 2048
SEQ = 8192
DEPTH = 2
DEC_BATCH = 128
PAST_LEN = 2048
D_INNER = 2 * D_MODEL
SSM_HEAD_DIM = 64
SSM_HEADS = D_INNER // SSM_HEAD_DIM
SSM_GROUPS = 8
HEADS_PER_GROUP = SSM_HEADS // SSM_GROUPS
D_STATE = 128
CONV_WIDTH = 4
CONV_DIM = D_INNER + 2 * SSM_GROUPS * D_STATE
SSD_CHUNK = 128
RMS_EPS = 1e-5
HEAD_DIM = 128
N_HEADS_B = D_MODEL // HEAD_DIM
DILATION_GROUPS = ((128, 1), (512, 4), (2048, 16))
N_GROUPS_B = len(DILATION_GROUPS)
WINDOW_MAX = max(w for w, _ in DILATION_GROUPS)
Q_DIM = N_GROUPS_B * N_HEADS_B * HEAD_DIM
KV_DIM = N_HEADS_B * HEAD_DIM
NUM_BUCKETS = 32
MAX_DISTANCE = WINDOW_MAX
D_FF = -(-8 * D_MODEL // (3 * 256)) * 256
ALPHA = (2 * DEPTH) ** 0.25
LN_EPS = 1e-5

LANES = 128
SUBLANES = 8
VMEM_LIMIT_BYTES = 56 * 1024 * 1024
NEG_BIG = -1e30


def _cparams(semantics):
    return pltpu.CompilerParams(dimension_semantics=semantics, vmem_limit_bytes=VMEM_LIMIT_BYTES)


def _silu(x):
    return x * (1.0 / (1.0 + jnp.exp(-x)))


def _split3(x):
    hi = x.astype(BF16)
    r1 = x - hi.astype(F32)
    mid = r1.astype(BF16)
    lo = (r1 - mid.astype(F32)).astype(BF16)
    return hi, mid, lo


CAST_ROWS = 256


def _cast_tile(w_ref, wbf_ref):
    k = w_ref.shape[0]
    rows = CAST_ROWS if k % CAST_ROWS == 0 else k
    def body(r, c):
        off = pl.multiple_of(r * rows, rows)
        wbf_ref[pl.ds(off, rows), :] = w_ref[pl.ds(off, rows), :].astype(BF16)
        return c
    lax.fori_loop(0, k // rows, body, 0)


def _mm_body(*refs, act, has_bias):
    a_ref, w_ref = refs[0], refs[1]
    b_ref = refs[2] if has_bias else None
    o_ref, wbf_ref = refs[2 + has_bias], refs[3 + has_bias]

    @pl.when(pl.program_id(1) == 0)
    def _():
        _cast_tile(w_ref, wbf_ref)

    a = a_ref[...]
    if act == "silu":
        a = _silu(a.astype(F32))
    acc = jnp.dot(a.astype(BF16), wbf_ref[...], preferred_element_type=F32)
    if has_bias:
        acc = acc + b_ref[...]
    o_ref[...] = acc.astype(o_ref.dtype)


def _weight_spec(w, layer, tn, col_off=0):
    k = w.shape[-2]
    if w.ndim == 2:
        return pl.BlockSpec((k, tn), lambda j, i: (0, j + col_off))
    return pl.BlockSpec((None, k, tn), lambda j, i: (layer, 0, j + col_off))


def matmul(a, w, *, tm, tn, layer=0, n_out=None, col_off=0, bias=None, act=None, out_dtype=F32, name="mm"):
    m, k = a.shape
    n_out = w.shape[-1] if n_out is None else n_out
    grid = (pl.cdiv(n_out, tn), m // tm)
    in_specs = [
        pl.BlockSpec((tm, k), lambda j, i: (i, 0)),
        _weight_spec(w, layer, tn, col_off),
    ]
    args = [a, w]
    if bias is not None:
        in_specs.append(pl.BlockSpec((1, tn), lambda j, i: (0, j + col_off)))
        args.append(bias)
    return pl.pallas_call(
        functools.partial(_mm_body, act=act, has_bias=bias is not None),
        grid=grid,
        in_specs=in_specs,
        out_specs=pl.BlockSpec((tm, tn), lambda j, i: (i, j)),
        out_shape=jax.ShapeDtypeStruct((m, n_out), out_dtype),
        scratch_shapes=[pltpu.VMEM((k, tn), BF16)],
        compiler_params=_cparams(("arbitrary", "arbitrary")),
        name=name,
    )(*args)


def _glu_body(a_ref, wg_ref, wu_ref, o_ref, wgbf_ref, wubf_ref):
    @pl.when(pl.program_id(1) == 0)
    def _():
        _cast_tile(wg_ref, wgbf_ref)
        _cast_tile(wu_ref, wubf_ref)

    a = a_ref[...]
    g = jnp.dot(a, wgbf_ref[...], preferred_element_type=F32)
    u = jnp.dot(a, wubf_ref[...], preferred_element_type=F32)
    o_ref[...] = (_silu(g) * u).astype(o_ref.dtype)


def glu_matmul(a, w_gate, w_up, *, tm, tn, layer=0, name="glu"):
    m, k = a.shape
    n = w_gate.shape[-1]
    return pl.pallas_call(
        _glu_body,
        grid=(n // tn, m // tm),
        in_specs=[
            pl.BlockSpec((tm, k), lambda j, i: (i, 0)),
            _weight_spec(w_gate, layer, tn),
            _weight_spec(w_up, layer, tn),
        ],
        out_specs=pl.BlockSpec((tm, tn), lambda j, i: (i, j)),
        out_shape=jax.ShapeDtypeStruct((m, n), BF16),
        scratch_shapes=[pltpu.VMEM((k, tn), BF16), pltpu.VMEM((k, tn), BF16)],
        compiler_params=_cparams(("arbitrary", "arbitrary")),
        name=name,
    )(a, w_gate, w_up)


def _row_spec(arr, tm):
    if arr.shape[0] == 1:
        return pl.BlockSpec((1, arr.shape[1]), lambda i: (0, 0))
    return pl.BlockSpec((tm, arr.shape[1]), lambda i: (i, 0))


def _modulate_body(x_ref, sc_ref, sh_ref, o_ref):
    o_ref[...] = (x_ref[...] * (1.0 + sc_ref[...]) + sh_ref[...]).astype(o_ref.dtype)


def modulate(x, scale, shift, *, tm):
    m, d = x.shape
    return pl.pallas_call(
        _modulate_body,
        grid=(m // tm,),
        in_specs=[pl.BlockSpec((tm, d), lambda i: (i, 0)), _row_spec(scale, tm), _row_spec(shift, tm)],
        out_specs=pl.BlockSpec((tm, d), lambda i: (i, 0)),
        out_shape=jax.ShapeDtypeStruct((m, d), BF16),
        compiler_params=_cparams(("arbitrary",)),
        name="modulate",
    )(x, scale, shift)


def _ln_body(*refs, n_mod):
    x_ref, y_ref, gate_ref, g_ref, b_ref = refs[:5]
    mod_refs = refs[5:5 + 2 * n_mod]
    xo_ref = refs[5 + 2 * n_mod]
    u_refs = refs[6 + 2 * n_mod:]
    t = ALPHA * x_ref[...] + (1.0 + gate_ref[...]) * y_ref[...]
    mu = jnp.mean(t, axis=-1, keepdims=True)
    tc = t - mu
    var = jnp.mean(tc * tc, axis=-1, keepdims=True)
    xn = tc * lax.rsqrt(var + LN_EPS) * g_ref[...] + b_ref[...]
    xo_ref[...] = xn
    for q in range(n_mod):
        u_refs[q][...] = (xn * (1.0 + mod_refs[2 * q][...]) + mod_refs[2 * q + 1][...]).astype(BF16)


def post_ln(x, y, gate, ln_g, ln_b, mods, *, tm):
    m, d = x.shape
    row = pl.BlockSpec((tm, d), lambda i: (i, 0))
    vec = pl.BlockSpec((1, d), lambda i: (0, 0))
    in_specs = [row, row, _row_spec(gate, tm), vec, vec]
    args = [x, y, gate, ln_g, ln_b]
    for sc, sh in mods:
        in_specs += [_row_spec(sc, tm), _row_spec(sh, tm)]
        args += [sc, sh]
    outs = pl.pallas_call(
        functools.partial(_ln_body, n_mod=len(mods)),
        grid=(m // tm,),
        in_specs=in_specs,
        out_specs=[row] * (1 + len(mods)),
        out_shape=[jax.ShapeDtypeStruct((m, d), F32)] + [jax.ShapeDtypeStruct((m, d), BF16)] * len(mods),
        compiler_params=_cparams(("arbitrary",)),
        name="post_ln",
    )(*args)
    return outs


CONV_SLAB = 512
GROUP_X = D_INNER // SSM_GROUPS
PAIR = 2 * SSM_HEAD_DIM


def _softplus(x):
    return jnp.maximum(x, 0.0) + jnp.log1p(jnp.exp(-jnp.abs(x)))


def _dot_nt(a, b):
    return lax.dot_general(a, b, (((1,), (1,)), ((), ())), preferred_element_type=F32)


def _dot_tn(a, b):
    return lax.dot_general(a, b, (((0,), (0,)), ((), ())), preferred_element_type=F32)


def _ssd_prompt_body(z_ref, xbc_ref, dt_ref, cw_ref, cb_ref, dtb_ref, alog_ref, dexp_ref, nw_ref,
                     g_ref, hfin_ref,
                     ext_ref, xs_ref, b_ref, c_ref, y_ref, state_ref,
                     cs_ref, w_ref, cst_ref, dtt_ref, ecl_ref):
    c = pl.program_id(0)
    L = SSD_CHUNK

    @pl.when(c == 0)
    def _():
        ext_ref[0:SUBLANES, :] = jnp.zeros((SUBLANES, CONV_DIM), F32)
        state_ref[...] = jnp.zeros(state_ref.shape, F32)

    ext_ref[SUBLANES:SUBLANES + L, :] = xbc_ref[...]
    for s in range(CONV_DIM // CONV_SLAB):
        cols = slice(s * CONV_SLAB, (s + 1) * CONV_SLAB)
        acc = cb_ref[:, cols]
        for i in range(CONV_WIDTH):
            lo = SUBLANES - (CONV_WIDTH - 1) + i
            acc = acc + cw_ref[i:i + 1, cols] * ext_ref[lo:lo + L, cols]
        a = _silu(acc)
        if s < D_INNER // CONV_SLAB:
            xs_ref[s] = a
        else:
            t = s - D_INNER // CONV_SLAB
            per = SSM_GROUPS * D_STATE // CONV_SLAB
            dst = b_ref if t < per else c_ref
            t = t % per
            for q in range(CONV_SLAB // D_STATE):
                dst[t * (CONV_SLAB // D_STATE) + q] = a[:, q * D_STATE:(q + 1) * D_STATE]
    ext_ref[0:SUBLANES, :] = ext_ref[L:L + SUBLANES, :]

    lane = lax.broadcasted_iota(jnp.int32, (L, LANES), 1)
    row = lax.broadcasted_iota(jnp.int32, (L, LANES), 0)
    head_ok = lane < SSM_HEADS
    dt = jnp.where(head_ok, _softplus(dt_ref[...] + dtb_ref[...]), 0.0)
    da = dt * (-jnp.exp(alog_ref[...]))
    causal = row >= lane
    tril = jnp.where(causal, 1.0, 0.0).astype(BF16)
    cs = sum(jnp.dot(tril, p, preferred_element_type=F32) for p in _split3(da))
    cs_last = cs[L - 1:L, :]
    cs_ref[...] = cs
    w_ref[...] = dt * jnp.exp(cs_last - cs)
    cst = cs.T
    cst_ref[...] = cst
    dtt_ref[...] = dt.T
    ecl_ref[...] = jnp.broadcast_to(jnp.exp(cst[:, L - 1:L]), (LANES, LANES))
    lane_lo = lane < SSM_HEAD_DIM

    def group(g, carry):
        bg = b_ref[g].astype(BF16)
        cg = c_ref[g].astype(BF16)
        cb = _dot_nt(cg, bg)
        yoff = _dot_nt(cg, state_ref[g].astype(BF16))
        shift = LANES - HEADS_PER_GROUP * g
        csg = pltpu.roll(cs_ref[...], shift, 1)
        wg = pltpu.roll(w_ref[...], shift, 1)
        g8 = pl.multiple_of(g * HEADS_PER_GROUP, HEADS_PER_GROUP)
        cstg = cst_ref[pl.ds(g8, HEADS_PER_GROUP), :]
        dttg = dtt_ref[pl.ds(g8, HEADS_PER_GROUP), :]
        eclg = ecl_ref[pl.ds(g8, HEADS_PER_GROUP), :]
        for pr in range(HEADS_PER_GROUP // 2):
            x2 = xs_ref[g, :, pr * PAIR:(pr + 1) * PAIR]
            ms, ecols, wcols, decs = [], [], [], []
            for k in (2 * pr, 2 * pr + 1):
                col = jnp.broadcast_to(csg[:, k:k + 1], (L, LANES))
                rowv = jnp.broadcast_to(cstg[k:k + 1, :], (L, LANES))
                lmat = jnp.exp(jnp.where(causal, col - rowv, NEG_BIG))
                ms.append((cb * lmat * jnp.broadcast_to(dttg[k:k + 1, :], (L, LANES))).astype(BF16))
                ecols.append(jnp.exp(col))
                wcols.append(jnp.broadcast_to(wg[:, k:k + 1], (L, LANES)))
                decs.append(jnp.broadcast_to(eclg[k:k + 1, :], (SSM_HEAD_DIM, LANES)))
            lhs = jnp.concatenate(ms, axis=1)
            rhs = jnp.concatenate([jnp.where(lane_lo, x2, 0.0), jnp.where(lane_lo, 0.0, x2)], axis=0).astype(BF16)
            ydiag = jnp.dot(lhs, rhs, preferred_element_type=F32)
            y_ref[g, :, pr * PAIR:(pr + 1) * PAIR] = ydiag + yoff[:, pr * PAIR:(pr + 1) * PAIR] * jnp.where(lane_lo, ecols[0], ecols[1])
            xw = (x2 * jnp.where(lane_lo, wcols[0], wcols[1])).astype(BF16)
            snew = _dot_tn(xw, bg)
            rows = slice(pr * PAIR, (pr + 1) * PAIR)
            state_ref[g, rows, :] = state_ref[g, rows, :] * jnp.concatenate(decs, axis=0) + snew
        return carry

    lax.fori_loop(0, SSM_GROUPS, group, 0)

    for g in range(SSM_GROUPS):
        cols = slice(g * GROUP_X, (g + 1) * GROUP_X)
        y = y_ref[g] + dexp_ref[:, cols] * xs_ref[g]
        gt = y * _silu(z_ref[:, cols])
        ms = jnp.mean(gt * gt, axis=-1, keepdims=True)
        g_ref[:, cols] = (gt * lax.rsqrt(ms + RMS_EPS) * nw_ref[:, cols]).astype(g_ref.dtype)

    @pl.when(c == pl.num_programs(0) - 1)
    def _():
        hfin_ref[...] = state_ref[...]


def ssd_prompt(z, xbc, dt_raw, conv_w, conv_b, dt_bias, a_log, d_exp, norm_w):
    s = z.shape[0]
    L = SSD_CHUNK
    vec = lambda n: pl.BlockSpec((1, n), lambda c: (0, 0))
    return pl.pallas_call(
        _ssd_prompt_body,
        grid=(s // L,),
        in_specs=[
            pl.BlockSpec((L, D_INNER), lambda c: (c, 0)),
            pl.BlockSpec((L, CONV_DIM), lambda c: (c, 0)),
            pl.BlockSpec((L, LANES), lambda c: (c, 0)),
            pl.BlockSpec((CONV_WIDTH, CONV_DIM), lambda c: (0, 0)),
            vec(CONV_DIM), vec(LANES), vec(LANES), vec(D_INNER), vec(D_INNER),
        ],
        out_specs=[
            pl.BlockSpec((L, D_INNER), lambda c: (c, 0)),
            pl.BlockSpec((SSM_GROUPS, GROUP_X, D_STATE), lambda c: (0, 0, 0)),
        ],
        out_shape=[
            jax.ShapeDtypeStruct((s, D_INNER), BF16),
            jax.ShapeDtypeStruct((SSM_GROUPS, GROUP_X, D_STATE), F32),
        ],
        scratch_shapes=[
            pltpu.VMEM((L + 2 * SUBLANES, CONV_DIM), F32),
            pltpu.VMEM((SSM_GROUPS, L, GROUP_X), F32),
            pltpu.VMEM((SSM_GROUPS, L, D_STATE), F32),
            pltpu.VMEM((SSM_GROUPS, L, D_STATE), F32),
            pltpu.VMEM((SSM_GROUPS, L, GROUP_X), F32),
            pltpu.VMEM((SSM_GROUPS, GROUP_X, D_STATE), F32),
            pltpu.VMEM((L, LANES), F32), pltpu.VMEM((L, LANES), F32),
            pltpu.VMEM((LANES, L), F32), pltpu.VMEM((LANES, L), F32),
            pltpu.VMEM((LANES, LANES), F32),
        ],
        compiler_params=_cparams(("arbitrary",)),
        name="ssd_prompt",
    )(z, xbc, dt_raw, conv_w, conv_b, dt_bias, a_log, d_exp, norm_w)


def _decode_conv_body(s0_ref, s1_ref, s2_ref, xn_ref, cw_ref, cb_ref, o_ref):
    acc = cb_ref[...] + cw_ref[0:1, :] * s0_ref[...] + cw_ref[1:2, :] * s1_ref[...]
    acc = acc + cw_ref[2:3, :] * s2_ref[...] + cw_ref[3:4, :] * xn_ref[...]
    o_ref[...] = _silu(acc)


def decode_conv(s0, s1, s2, xnew, conv_w, conv_b):
    b, n = xnew.shape
    blk = pl.BlockSpec((b, CONV_SLAB), lambda j: (0, j))
    return pl.pallas_call(
        _decode_conv_body,
        grid=(n // CONV_SLAB,),
        in_specs=[blk, blk, blk, blk, pl.BlockSpec((CONV_WIDTH, CONV_SLAB), lambda j: (0, j)),
                  pl.BlockSpec((1, CONV_SLAB), lambda j: (0, j))],
        out_specs=blk,
        out_shape=jax.ShapeDtypeStruct((b, n), F32),
        compiler_params=_cparams(("arbitrary",)),
        name="decode_conv",
    )(s0, s1, s2, xnew, conv_w, conv_b)


def _decode_prep_body(xs_ref, b_ref, c_ref, dt_ref, dtb_ref, alog_ref, e_ref, dtx_ref, daexp_ref, bcx_ref):
    lane = lax.broadcasted_iota(jnp.int32, dt_ref.shape, 1)
    dt = jnp.where(lane < SSM_HEADS, _softplus(dt_ref[...] + dtb_ref[...]), 0.0)
    da = jnp.where(lane < SSM_HEADS, jnp.exp(dt * (-jnp.exp(alog_ref[...]))), 0.0)
    e = e_ref[...]
    expand = lambda v: sum(jnp.dot(p, e, preferred_element_type=F32) for p in _split3(v))
    dtx = expand(dt) * xs_ref[...]
    dtx_ref[...] = dtx
    daexp_ref[...] = expand(da)
    bcx_ref[...] = dtx * jnp.sum(b_ref[...] * c_ref[...], axis=-1, keepdims=True)


def decode_prep(xbc_act, dt_raw, dt_bias, a_log, expand_mat):
    b = xbc_act.shape[0]
    xblk = pl.BlockSpec((b, GROUP_X), lambda g: (0, g))
    vec = pl.BlockSpec((1, LANES), lambda g: (0, 0))
    nb = D_INNER // D_STATE
    out = jax.ShapeDtypeStruct((b, D_INNER), F32)
    return pl.pallas_call(
        _decode_prep_body,
        grid=(SSM_GROUPS,),
        in_specs=[
            xblk,
            pl.BlockSpec((b, D_STATE), lambda g: (0, nb + g)),
            pl.BlockSpec((b, D_STATE), lambda g: (0, nb + SSM_GROUPS + g)),
            pl.BlockSpec((b, LANES), lambda g: (0, 0)),
            vec, vec,
            pl.BlockSpec((LANES, GROUP_X), lambda g: (0, g)),
        ],
        out_specs=[xblk, xblk, xblk],
        out_shape=[out, out, out],
        compiler_params=_cparams(("arbitrary",)),
        name="decode_prep",
    )(xbc_act, xbc_act, xbc_act, dt_raw, dt_bias, a_log, expand_mat)


TILE_ROWS = 128
SPLIT_ROWS = 16


def _decode_state_body(h_ref, dtx_ref, da_ref, b_ref, c_ref, ho_ref, hc_ref):
    n_tiles = (SSM_HEADS * SSM_HEAD_DIM) // TILE_ROWS
    rid = lambda w: lax.broadcasted_iota(jnp.int32, (SPLIT_ROWS, w), 0)
    pick = lambda r, pieces, w: functools.reduce(
        lambda acc, kv: jnp.where(r == kv[0], jnp.broadcast_to(kv[1].astype(F32), (SPLIT_ROWS, w)), acc),
        pieces, jnp.zeros((SPLIT_ROWS, w), F32))
    xh, xm, xl = _split3(dtx_ref[...])
    ah, am, al = _split3(da_ref[...])
    bh, bm, bl = _split3(b_ref[...])
    r = rid(D_INNER)
    lmat = pick(r, [(0, xh), (1, xh), (2, xh), (3, xm), (4, xm), (5, xl), (6, ah), (7, am), (8, al)], D_INNER)
    rb = rid(SSM_GROUPS * D_STATE)
    ru = pick(rb, [(0, bh), (1, bm), (2, bl), (3, bh), (4, bm), (5, bh)], SSM_GROUPS * D_STATE).astype(BF16)
    r1 = rid(D_STATE)
    rd = jnp.where(jnp.logical_and(r1 >= 6, r1 <= 8), 1.0, 0.0).astype(BF16)
    crow = c_ref[...]
    for i in range(n_tiles):
        g = (i * TILE_ROWS) // GROUP_X
        lanes = slice(i * TILE_ROWS, (i + 1) * TILE_ROWS)
        lt = lmat[:, lanes].T.astype(BF16)
        upd = jnp.dot(lt, ru[:, g * D_STATE:(g + 1) * D_STATE], preferred_element_type=F32)
        dec = jnp.dot(lt, rd, preferred_element_type=F32)
        h = h_ref[lanes, :]
        ho_ref[lanes, :] = h * dec + upd
        t = (h * crow[:, g * D_STATE:(g + 1) * D_STATE]).T
        hc_ref[0, :, lanes] = jnp.sum(t, axis=0, keepdims=True)


def decode_state(h, dtx, daexp, xbc_act3):
    b = h.shape[0]
    rows = SSM_HEADS * SSM_HEAD_DIM
    hblk = pl.BlockSpec((None, rows, D_STATE), lambda i: (i, 0, 0))
    rblk = pl.BlockSpec((None, 1, D_INNER), lambda i: (i, 0, 0))
    bc = SSM_GROUPS * D_STATE
    return pl.pallas_call(
        _decode_state_body,
        grid=(b,),
        in_specs=[
            hblk, rblk, rblk,
            pl.BlockSpec((None, 1, bc), lambda i: (i, 0, D_INNER // bc)),
            pl.BlockSpec((None, 1, bc), lambda i: (i, 0, D_INNER // bc + 1)),
        ],
        out_specs=[hblk, pl.BlockSpec((1, 1, D_INNER), lambda i: (i, 0, 0))],
        out_shape=[jax.ShapeDtypeStruct((b, rows, D_STATE), F32), jax.ShapeDtypeStruct((b, 1, D_INNER), F32)],
        compiler_params=_cparams(("arbitrary",)),
        name="decode_state",
    )(h, dtx, daexp, xbc_act3, xbc_act3)


def _decode_post_body(hc_ref, da_ref, bcx_ref, xs_ref, z_ref, dexp_ref, nw_ref, o_ref):
    y = da_ref[...] * hc_ref[...] + bcx_ref[...] + dexp_ref[...] * xs_ref[...]
    gt = y * _silu(z_ref[...])
    ms = jnp.mean(gt * gt, axis=-1, keepdims=True)
    o_ref[...] = (gt * lax.rsqrt(ms + RMS_EPS) * nw_ref[...]).astype(o_ref.dtype)


def decode_post(hc, daexp, bcx, xbc_act, z, d_exp, norm_w):
    b = hc.shape[0]
    xblk = pl.BlockSpec((b, GROUP_X), lambda g: (0, g))
    vblk = pl.BlockSpec((1, GROUP_X), lambda g: (0, g))
    return pl.pallas_call(
        _decode_post_body,
        grid=(SSM_GROUPS,),
        in_specs=[xblk, xblk, xblk, xblk, xblk, vblk, vblk],
        out_specs=xblk,
        out_shape=jax.ShapeDtypeStruct((b, D_INNER), BF16),
        compiler_params=_cparams(("arbitrary",)),
        name="decode_post",
    )(hc, daexp, bcx, xbc_act, z, d_exp, norm_w)


SUPER = WINDOW_MAX
ATT_N = 128
MERGE_ROWS = 256


def _rel_bucket(dist):
    max_exact = NUM_BUCKETS // 2
    df = jnp.maximum(dist, max_exact).astype(F32)
    large = max_exact + (jnp.log(df / max_exact) / math.log(MAX_DISTANCE / max_exact)
                         * (NUM_BUCKETS - max_exact)).astype(jnp.int32)
    return jnp.where(dist < max_exact, dist, jnp.minimum(large, NUM_BUCKETS - 1))


def _prompt_bias_table(rel_bias):
    n = ATT_N
    j = jnp.arange(n)[:, None] + n - jnp.arange(2 * n)[None, :]
    valid = (j >= 0) & (j <= n)
    tabs = []
    for g, (_, dil) in enumerate(DILATION_GROUPS):
        tab = rel_bias[:, g * N_HEADS_B:(g + 1) * N_HEADS_B]
        bias = tab[_rel_bucket(jnp.clip(j, 0, n) * dil)].transpose(2, 0, 1).astype(F32)
        tabs.append(jnp.where(valid[None], bias, NEG_BIG))
    return jnp.stack(tabs, axis=0)


def _rows(start, size, stride):
    return pl.ds(start, size) if stride == 1 else pl.ds(start, size, stride=stride)


def _attn_prompt_body(q0_ref, q1_ref, q2_ref, kc_ref, kp_ref, vc_ref, vp_ref, t_ref, o_ref, og_ref, lse_ref):
    first = pl.program_id(0) == 0
    n = ATT_N
    scale = HEAD_DIM ** -0.5
    key_id = lax.broadcasted_iota(jnp.int32, (n, 2 * n), 1)
    q_refs = (q0_ref, q1_ref, q2_ref)
    for g, (win, dil) in enumerate(DILATION_GROUPS):
        nblk = SUPER // (n * dil)
        bias = t_ref[g, 0]
        for r in range(dil):
            for c in range(nblk):
                start = c * n * dil + r
                cur = _rows(start, n, dil)
                qb = q_refs[g][cur, :].astype(BF16)
                if c > 0:
                    prev = _rows(start - n * dil, n, dil)
                    kprev, vprev = kc_ref[prev, :], vc_ref[prev, :]
                else:
                    prev = _rows((nblk - 1) * n * dil + r, n, dil)
                    kprev, vprev = kp_ref[prev, :], vp_ref[prev, :]
                keys = jnp.concatenate([kprev, kc_ref[cur, :]], axis=0).astype(BF16)
                vals = jnp.concatenate([vprev, vc_ref[cur, :]], axis=0).astype(BF16)
                s = _dot_nt(qb, keys) * scale + bias
                if c == 0:
                    s = jnp.where(jnp.logical_and(first, key_id < n), NEG_BIG, s)
                m = jnp.max(s, axis=-1, keepdims=True)
                p = jnp.exp(s - m)
                den = jnp.sum(p, axis=-1, keepdims=True)
                o = jnp.dot(p.astype(BF16), vals, preferred_element_type=F32) * (1.0 / den)
                og_ref[g, cur, :] = o
                lse_ref[g, cur, :] = jnp.broadcast_to(m + jnp.log(den), (n, LANES))
    for i in range(SUPER // MERGE_ROWS):
        rows = slice(i * MERGE_ROWS, (i + 1) * MERGE_ROWS)
        ls = [lse_ref[g, rows, :] for g in range(N_GROUPS_B)]
        mx = functools.reduce(jnp.maximum, ls)
        es = [jnp.exp(l - mx) for l in ls]
        tot = functools.reduce(lambda a, b: a + b, es)
        acc = functools.reduce(lambda a, b: a + b, [es[g] * og_ref[g, rows, :] for g in range(N_GROUPS_B)])
        o_ref[rows, :] = (acc * (1.0 / tot)).astype(o_ref.dtype)


def attn_prompt(q, kv, bias_tab):
    s = q.shape[0]
    blk = lambda f: pl.BlockSpec((SUPER, HEAD_DIM), f)
    return pl.pallas_call(
        _attn_prompt_body,
        grid=(s // SUPER, N_HEADS_B),
        in_specs=[
            blk(lambda c, h: (c, h)),
            blk(lambda c, h: (c, N_HEADS_B + h)),
            blk(lambda c, h: (c, 2 * N_HEADS_B + h)),
            blk(lambda c, h: (c, h)),
            blk(lambda c, h: (jnp.maximum(c - 1, 0), h)),
            blk(lambda c, h: (c, N_HEADS_B + h)),
            blk(lambda c, h: (jnp.maximum(c - 1, 0), N_HEADS_B + h)),
            pl.BlockSpec((N_GROUPS_B, 1, ATT_N, 2 * ATT_N), lambda c, h: (0, h, 0, 0)),
        ],
        out_specs=blk(lambda c, h: (c, h)),
        out_shape=jax.ShapeDtypeStruct((s, KV_DIM), BF16),
        scratch_shapes=[
            pltpu.VMEM((N_GROUPS_B, SUPER, HEAD_DIM), F32),
            pltpu.VMEM((N_GROUPS_B, SUPER, LANES), F32),
        ],
        compiler_params=_cparams(("arbitrary", "arbitrary")),
        name="attn_prompt",
    )(q, q, q, kv, kv, kv, kv, bias_tab)


N_BUF = min(WINDOW_MAX, PAST_LEN)
ROW_MAJOR = 16
ROW_MID = 4


def _decode_bias_tables(rel_bias):
    n = ATT_N
    j = n - jnp.arange(n)
    cached, new = [], []
    for g, (_, dil) in enumerate(DILATION_GROUPS):
        tab = rel_bias[:, g * N_HEADS_B:(g + 1) * N_HEADS_B]
        cached.append(tab[_rel_bucket(j * dil)])
        new.append(tab[_rel_bucket(jnp.zeros((), jnp.int32))])
    cached = jnp.broadcast_to(jnp.stack(cached)[..., None], (N_GROUPS_B, n, N_HEADS_B, LANES)).astype(F32)
    new = jnp.broadcast_to(jnp.stack(new)[..., None], (N_GROUPS_B, N_HEADS_B, LANES)).astype(F32)
    return cached, new


def _attn_decode_body(q_ref, kvn_ref, k0_ref, v0_ref, k1_ref, v1_ref, k2_ref, v2_ref, bc_ref, bn_ref, o_ref):
    n = ATT_N
    scale = HEAD_DIM ** -0.5
    kn, vn = kvn_ref[0], kvn_ref[1]
    kv_refs = ((k0_ref, v0_ref), (k1_ref, v1_ref), (k2_ref, v2_ref))
    outs, lses = [], []
    for g in range(N_GROUPS_B):
        q = q_ref[g]
        k = kv_refs[g][0][...].reshape(n, N_HEADS_B, HEAD_DIM)
        v = kv_refs[g][1][...].reshape(n, N_HEADS_B, HEAD_DIM)
        s = jnp.sum(k * q[None], axis=-1, keepdims=True) * scale + bc_ref[g]
        s0 = jnp.sum(kn * q, axis=-1, keepdims=True) * scale + bn_ref[g]
        m = jnp.maximum(jnp.max(s, axis=0), s0)
        p = jnp.exp(s - m[None])
        p0 = jnp.exp(s0 - m)
        den = jnp.sum(p, axis=0) + p0
        outs.append((jnp.sum(p * v, axis=0) + p0 * vn) * (1.0 / den))
        lses.append(m + jnp.log(den))
    mx = functools.reduce(jnp.maximum, lses)
    es = [jnp.exp(l - mx) for l in lses]
    tot = functools.reduce(lambda a, b: a + b, es)
    acc = functools.reduce(lambda a, b: a + b, [e * o for e, o in zip(es, outs)])
    o_ref[...] = (acc * (1.0 / tot)).astype(o_ref.dtype)


def attn_decode(q, kv_new, cache_k, cache_v, bias_cached, bias_new):
    b = q.shape[0]
    view = (b, N_BUF // ROW_MAJOR, ROW_MAJOR // ROW_MID, ROW_MID, N_HEADS_B, HEAD_DIM)
    ck, cv = cache_k.reshape(view), cache_v.reshape(view)
    n = ATT_N
    hd = (N_HEADS_B, HEAD_DIM)
    g0 = pl.BlockSpec((None, n // ROW_MAJOR, ROW_MAJOR // ROW_MID, ROW_MID) + hd,
                      lambda i: (i, N_BUF // n - 1, 0, 0, 0, 0))
    g1 = pl.BlockSpec((None, n * ROW_MID // ROW_MAJOR, ROW_MAJOR // ROW_MID, None) + hd,
                      lambda i: (i, N_BUF // (n * ROW_MID) - 1, 0, 0, 0, 0))
    g2 = pl.BlockSpec((None, n, None, None) + hd, lambda i: (i, 0, 0, 0, 0, 0))
    return pl.pallas_call(
        _attn_decode_body,
        grid=(b,),
        in_specs=[
            pl.BlockSpec((None, N_GROUPS_B) + hd, lambda i: (i, 0, 0, 0)),
            pl.BlockSpec((None, 2) + hd, lambda i: (i, 0, 0, 0)),
            g0, g0, g1, g1, g2, g2,
            pl.BlockSpec((N_GROUPS_B, n) + (N_HEADS_B, LANES), lambda i: (0, 0, 0, 0)),
            pl.BlockSpec((N_GROUPS_B, N_HEADS_B, LANES), lambda i: (0, 0, 0)),
        ],
        out_specs=pl.BlockSpec((None,) + hd, lambda i: (i, 0, 0)),
        out_shape=jax.ShapeDtypeStruct((b,) + hd, BF16),
        compiler_params=_cparams(("arbitrary",)),
        name="attn_decode",
    )(q, kv_new, ck, cv, ck, cv, ck, cv, bias_cached, bias_new)


ADA_ROWS = 136
Z_TILE = 1024


def _pad_lanes(v):
    return jnp.pad(v.astype(F32), (0, LANES - v.shape[0]))[None]


def _trunk(x, ada0, ada1, adakv, mixer_a, mixer_b, p, *, tm, tm_ew):
    sh1, sc1, g1, sh2, sc2, g2 = ada0
    sh1b, sc1b, g1b, sh2b, sc2b, g2b = ada1
    shkv, sckv = adakv
    vec = lambda a, l, s: a[l, s][None]

    u = modulate(x, sc1, sh1, tm=tm_ew)
    w_in = p["ssm_w_in"].reshape(D_MODEL, -1)
    z = matmul(u, w_in, tm=tm, tn=Z_TILE, n_out=D_INNER, name="in_z")
    xbc = matmul(u, w_in, tm=tm, tn=Z_TILE, n_out=CONV_DIM, col_off=D_INNER // Z_TILE, name="in_xbc")
    dt = matmul(u, w_in, tm=tm, tn=LANES, n_out=LANES, col_off=(D_INNER + CONV_DIM) // LANES, name="in_dt")
    gated, extras = mixer_a(z, xbc, dt)
    y = matmul(gated, p["ssm_w_out"].reshape(D_INNER, D_MODEL), tm=tm, tn=512, name="ssm_out")
    x, u = post_ln(x, y, g1, vec(p["ln_g"], 0, 0), vec(p["ln_b"], 0, 0), [(sc2, sh2)], tm=tm_ew)
    h = glu_matmul(u, p["ffn_w_gate"], p["ffn_w_up"], layer=0, tm=tm, tn=512, name="ffn0_glu")
    y = matmul(h, p["ffn_w_down"], layer=0, tm=tm, tn=512, name="ffn0_down")
    x, u, ukv = post_ln(x, y, g2, vec(p["ln_g"], 0, 1), vec(p["ln_b"], 0, 1), [(sc1b, sh1b), (sckv, shkv)], tm=tm_ew)

    kv = matmul(ukv, p["w_kv"], tm=tm, tn=1024, name="kv_proj")
    q = matmul(u, p["attn_w_q"], layer=0, tm=tm, tn=1024, name="q_proj")
    o = mixer_b(q, kv)
    y = matmul(o, p["attn_w_o"], layer=0, tm=tm, tn=512, name="attn_out")
    x, u = post_ln(x, y, g1b, vec(p["ln_g"], 1, 0), vec(p["ln_b"], 1, 0), [(sc2b, sh2b)], tm=tm_ew)
    h = glu_matmul(u, p["ffn_w_gate"], p["ffn_w_up"], layer=1, tm=tm, tn=512, name="ffn1_glu")
    y = matmul(h, p["ffn_w_down"], layer=1, tm=tm, tn=512, name="ffn1_down")
    (x,) = post_ln(x, y, g2b, vec(p["ln_g"], 1, 1), vec(p["ln_b"], 1, 1), [], tm=tm_ew)
    return x, extras, kv


def kernel(x_prompt, x_sample, state_conv, state_ssm, cache_k, cache_v, c_prompt, c_sample, w_ada, b_ada, ln_g, ln_b, ffn_w_gate, ffn_w_up, ffn_w_down, ssm_w_in, ssm_conv_w, ssm_conv_b, ssm_dt_bias, ssm_a_log, ssm_d, ssm_norm_w, ssm_w_out, kv_w_ada, kv_b_ada, w_kv, attn_w_q, attn_w_o, rel_bias):
    p = dict(ln_g=ln_g, ln_b=ln_b, ffn_w_gate=ffn_w_gate, ffn_w_up=ffn_w_up, ffn_w_down=ffn_w_down,
             ssm_w_in=ssm_w_in, ssm_w_out=ssm_w_out, w_kv=w_kv, attn_w_q=attn_w_q, attn_w_o=attn_w_o)
    s, b, d = SEQ, DEC_BATCH, D_MODEL

    c_all = jnp.concatenate([c_sample, c_prompt, jnp.zeros((ADA_ROWS - b - 1, d), F32)], axis=0)
    ada = [matmul(c_all, w_ada, layer=l, tm=ADA_ROWS, tn=1024, bias=b_ada[l][None], act="silu", name=f"ada{l}")
           for l in range(DEPTH)]
    adakv = matmul(c_all, kv_w_ada, tm=ADA_ROWS, tn=1024, bias=kv_b_ada[None], act="silu", name="ada_kv")
    split = lambda m, n, rows: [m[rows, i * d:(i + 1) * d] for i in range(n)]
    smp, prm = slice(0, b), slice(b, b + 1)

    conv_w, conv_b = ssm_conv_w[0], ssm_conv_b[0][None]
    dt_bias, a_log = _pad_lanes(ssm_dt_bias[0]), _pad_lanes(ssm_a_log[0])
    d_exp = jnp.repeat(ssm_d[0].astype(F32), SSM_HEAD_DIM)[None]
    norm_w = ssm_norm_w[0][None]

    def mixer_a_prompt(z, xbc, dt):
        gated, hfin = ssd_prompt(z, xbc, dt, conv_w, conv_b, dt_bias, a_log, d_exp, norm_w)
        return gated, (xbc[s - (CONV_WIDTH - 1):], hfin)

    bias_tab = _prompt_bias_table(rel_bias)
    y_p, (conv_p, ssm_p), kv_p = _trunk(
        x_prompt.reshape(s, d), split(ada[0], 6, prm), split(ada[1], 6, prm), split(adakv, 2, prm),
        mixer_a_prompt, lambda q, kv: attn_prompt(q, kv, bias_tab), p, tm=512, tm_ew=256)

    expand_mat = (lax.broadcasted_iota(jnp.int32, (LANES, D_INNER), 0)
                  == lax.broadcasted_iota(jnp.int32, (LANES, D_INNER), 1) // SSM_HEAD_DIM).astype(BF16)

    def mixer_a_decode(z, xbc, dt):
        st = state_conv[0]
        xa = decode_conv(st[:, 0], st[:, 1], st[:, 2], xbc, conv_w, conv_b)
        dtx, daexp, bcx = decode_prep(xa, dt, dt_bias, a_log, expand_mat)
        hnew, hc = decode_state(state_ssm.reshape(b, SSM_HEADS * SSM_HEAD_DIM, D_STATE),
                                dtx[:, None], daexp[:, None], xa[:, None])
        gated = decode_post(hc[:, 0], daexp, bcx, xa, z, d_exp, norm_w)
        new_conv = jnp.concatenate([st[:, 1:], xbc[:, None]], axis=1)
        return gated, (new_conv, hnew)

    bias_cached, bias_new = _decode_bias_tables(rel_bias)

    def mixer_b_decode(q, kv):
        o = attn_decode(q.reshape(b, N_GROUPS_B, N_HEADS_B, HEAD_DIM), kv.reshape(b, 2, N_HEADS_B, HEAD_DIM),
                        cache_k, cache_v, bias_cached, bias_new)
        return o.reshape(b, KV_DIM)

    y_s, (conv_s, ssm_s), kv_s = _trunk(
        x_sample.reshape(b, d), split(ada[0], 6, smp), split(ada[1], 6, smp), split(adakv, 2, smp),
        mixer_a_decode, mixer_b_decode, p, tm=b, tm_ew=b)

    n_keep = min(WINDOW_MAX, s)
    heads = (N_HEADS_B, HEAD_DIM)
    return (
        y_p.reshape(1, s, d),
        y_s.reshape(b, 1, d),
        conv_p.reshape(1, 1, CONV_WIDTH - 1, CONV_DIM),
        ssm_p.reshape(1, 1, SSM_HEADS, SSM_HEAD_DIM, D_STATE),
        kv_p[s - n_keep:, :KV_DIM].reshape((1, n_keep) + heads),
        kv_p[s - n_keep:, KV_DIM:].reshape((1, n_keep) + heads),
        conv_s.reshape(1, b, CONV_WIDTH - 1, CONV_DIM),
        ssm_s.reshape(1, b, SSM_HEADS, SSM_HEAD_DIM, D_STATE),
        kv_s[:, :KV_DIM].reshape((b, 1) + heads),
        kv_s[:, KV_DIM:].reshape((b, 1) + heads),
    )
```

```python
import functools
import math

import jax
import jax.numpy as jnp
from jax import lax
from jax.experimental import pallas as pl
from jax.experimental.pallas import tpu as pltpu

F32 = jnp.float32
BF16 = jnp.bfloat16

D_MODEL = 2048
SEQ = 8192
DEPTH = 2
DEC_BATCH = 128
PAST_LEN = 2048
D_INNER = 2 * D_MODEL
SSM_HEAD_DIM = 64
SSM_HEADS = D_INNER // SSM_HEAD_DIM
SSM_GROUPS = 8
HEADS_PER_GROUP = SSM_HEADS // SSM_GROUPS
D_STATE = 128
CONV_WIDTH = 4
CONV_DIM = D_INNER + 2 * SSM_GROUPS * D_STATE
SSD_CHUNK = 128
RMS_EPS = 1e-5
HEAD_DIM = 128
N_HEADS_B = D_MODEL // HEAD_DIM
DILATION_GROUPS = ((128, 1), (512, 4), (2048, 16))
N_GROUPS_B = len(DILATION_GROUPS)
WINDOW_MAX = max(w for w, _ in DILATION_GROUPS)
Q_DIM = N_GROUPS_B * N_HEADS_B * HEAD_DIM
KV_DIM = N_HEADS_B * HEAD_DIM
NUM_BUCKETS = 32
MAX_DISTANCE = WINDOW_MAX
D_FF = -(-8 * D_MODEL // (3 * 256)) * 256
ALPHA = (2 * DEPTH) ** 0.25
LN_EPS = 1e-5

LANES = 128
SUBLANES = 8
VMEM_LIMIT_BYTES = 56 * 1024 * 1024
NEG_BIG = -1e30


def _cparams(semantics):
    return pltpu.CompilerParams(dimension_semantics=semantics, vmem_limit_bytes=VMEM_LIMIT_BYTES)


def _silu(x):
    return x * (1.0 / (1.0 + jnp.exp(-x)))


def _split3(x):
    hi = x.astype(BF16)
    r1 = x - hi.astype(F32)
    mid = r1.astype(BF16)
    lo = (r1 - mid.astype(F32)).astype(BF16)
    return hi, mid, lo


CAST_ROWS = 256


def _cast_tile(w_ref, wbf_ref, transposed=False):
    k, tn = wbf_ref.shape
    if transposed:
        step = min(CAST_ROWS, tn)
        for c in range(tn // step):
            wbf_ref[:, c * step:(c + 1) * step] = w_ref[c * step:(c + 1) * step, :].T.astype(BF16)
        return
    rows = CAST_ROWS if k % CAST_ROWS == 0 else k
    def body(r, c):
        off = pl.multiple_of(r * rows, rows)
        wbf_ref[pl.ds(off, rows), :] = w_ref[pl.ds(off, rows), :].astype(BF16)
        return c
    lax.fori_loop(0, k // rows, body, 0)


def _mm_body(*refs, act, has_bias, has_rider, w_transposed):
    refs = list(refs)
    a_ref, w_ref = refs.pop(0), refs.pop(0)
    b_ref = refs.pop(0) if has_bias else None
    a2_ref = refs.pop(0) if has_rider else None
    o_ref = refs.pop(0)
    o2_ref = refs.pop(0) if has_rider else None
    wbf_ref = refs.pop(0)

    def product(lhs_ref):
        a = lhs_ref[...]
        if act == "silu":
            a = _silu(a.astype(F32))
        acc = jnp.dot(a.astype(BF16), wbf_ref[...], preferred_element_type=F32)
        return acc + b_ref[...] if has_bias else acc

    @pl.when(pl.program_id(1) == 0)
    def _():
        _cast_tile(w_ref, wbf_ref, w_transposed)
        if has_rider:
            o2_ref[...] = product(a2_ref).astype(o2_ref.dtype)

    o_ref[...] = product(a_ref).astype(o_ref.dtype)


def _weight_spec(w, layer, tn, col_off=0, transposed=False):
    if transposed:
        return pl.BlockSpec((tn, w.shape[1]), lambda j, i: (j + col_off, 0))
    k = w.shape[-2]
    if w.ndim == 2:
        return pl.BlockSpec((k, tn), lambda j, i: (0, j + col_off))
    return pl.BlockSpec((None, k, tn), lambda j, i: (layer, 0, j + col_off))


def matmul(a, w, *, tm, tn, layer=0, n_out=None, col_off=0, bias=None, act=None, rider=None,
           w_transposed=False, out_dtype=F32, name="mm"):
    m, k = a.shape
    n_out = (w.shape[0] if w_transposed else w.shape[-1]) if n_out is None else n_out
    grid = (pl.cdiv(n_out, tn), m // tm)
    in_specs = [
        pl.BlockSpec((tm, k), lambda j, i: (i, 0)),
        _weight_spec(w, layer, tn, col_off, w_transposed),
    ]
    args = [a, w]
    out_specs = [pl.BlockSpec((tm, tn), lambda j, i: (i, j))]
    out_shape = [jax.ShapeDtypeStruct((m, n_out), out_dtype)]
    if bias is not None:
        in_specs.append(pl.BlockSpec((1, tn), lambda j, i: (0, j + col_off)))
        args.append(bias)
    if rider is not None:
        m2 = rider.shape[0]
        in_specs.append(pl.BlockSpec((m2, k), lambda j, i: (0, 0)))
        args.append(rider)
        out_specs.append(pl.BlockSpec((m2, tn), lambda j, i: (0, j)))
        out_shape.append(jax.ShapeDtypeStruct((m2, n_out), out_dtype))
    outs = pl.pallas_call(
        functools.partial(_mm_body, act=act, has_bias=bias is not None, has_rider=rider is not None,
                          w_transposed=w_transposed),
        grid=grid,
        in_specs=in_specs,
        out_specs=out_specs,
        out_shape=out_shape,
        scratch_shapes=[pltpu.VMEM((k, tn), BF16)],
        compiler_params=_cparams(("arbitrary", "arbitrary")),
        name=name,
    )(*args)
    return outs if rider is not None else outs[0]


def _glu_body(a_ref, wg_ref, wu_ref, a2_ref, o_ref, o2_ref, wgbf_ref, wubf_ref):
    def product(lhs_ref):
        a = lhs_ref[...]
        g = jnp.dot(a, wgbf_ref[...], preferred_element_type=F32)
        u = jnp.dot(a, wubf_ref[...], preferred_element_type=F32)
        return (_silu(g) * u).astype(BF16)

    @pl.when(pl.program_id(1) == 0)
    def _():
        _cast_tile(wg_ref, wgbf_ref)
        _cast_tile(wu_ref, wubf_ref)
        o2_ref[...] = product(a2_ref)

    o_ref[...] = product(a_ref)


def glu_matmul(a, rider, w_gate, w_up, *, tm, tn, layer=0, name="glu"):
    m, k = a.shape
    m2 = rider.shape[0]
    n = w_gate.shape[-1]
    return pl.pallas_call(
        _glu_body,
        grid=(n // tn, m // tm),
        in_specs=[
            pl.BlockSpec((tm, k), lambda j, i: (i, 0)),
            _weight_spec(w_gate, layer, tn),
            _weight_spec(w_up, layer, tn),
            pl.BlockSpec((m2, k), lambda j, i: (0, 0)),
        ],
        out_specs=[pl.BlockSpec((tm, tn), lambda j, i: (i, j)), pl.BlockSpec((m2, tn), lambda j, i: (0, j))],
        out_shape=[jax.ShapeDtypeStruct((m, n), BF16), jax.ShapeDtypeStruct((m2, n), BF16)],
        scratch_shapes=[pltpu.VMEM((k, tn), BF16), pltpu.VMEM((k, tn), BF16)],
        compiler_params=_cparams(("arbitrary", "arbitrary")),
        name=name,
    )(a, w_gate, w_up, rider)


def _row_spec(arr, tm):
    if arr.shape[0] == 1:
        return pl.BlockSpec((1, arr.shape[1]), lambda i: (0, 0))
    return pl.BlockSpec((tm, arr.shape[1]), lambda i: (i, 0))


def _modulate_body(x_ref, sc_ref, sh_ref, o_ref):
    o_ref[...] = (x_ref[...] * (1.0 + sc_ref[...]) + sh_ref[...]).astype(o_ref.dtype)


def modulate(x, scale, shift, *, tm):
    m, d = x.shape
    return pl.pallas_call(
        _modulate_body,
        grid=(m // tm,),
        in_specs=[pl.BlockSpec((tm, d), lambda i: (i, 0)), _row_spec(scale, tm), _row_spec(shift, tm)],
        out_specs=pl.BlockSpec((tm, d), lambda i: (i, 0)),
        out_shape=jax.ShapeDtypeStruct((m, d), BF16),
        compiler_params=_cparams(("arbitrary",)),
        name="modulate",
    )(x, scale, shift)


def _ln_body(*refs, n_mod):
    x_ref, y_ref, gate_ref, g_ref, b_ref = refs[:5]
    mod_refs = refs[5:5 + 2 * n_mod]
    xo_ref = refs[5 + 2 * n_mod]
    u_refs = refs[6 + 2 * n_mod:]
    t = ALPHA * x_ref[...] + (1.0 + gate_ref[...]) * y_ref[...]
    mu = jnp.mean(t, axis=-1, keepdims=True)
    tc = t - mu
    var = jnp.mean(tc * tc, axis=-1, keepdims=True)
    xn = tc * lax.rsqrt(var + LN_EPS) * g_ref[...] + b_ref[...]
    xo_ref[...] = xn
    for q in range(n_mod):
        u_refs[q][...] = (xn * (1.0 + mod_refs[2 * q][...]) + mod_refs[2 * q + 1][...]).astype(BF16)


def post_ln(x, y, gate, ln_g, ln_b, mods, *, tm):
    m, d = x.shape
    row = pl.BlockSpec((tm, d), lambda i: (i, 0))
    vec = pl.BlockSpec((1, d), lambda i: (0, 0))
    in_specs = [row, row, _row_spec(gate, tm), vec, vec]
    args = [x, y, gate, ln_g, ln_b]
    for sc, sh in mods:
        in_specs += [_row_spec(sc, tm), _row_spec(sh, tm)]
        args += [sc, sh]
    outs = pl.pallas_call(
        functools.partial(_ln_body, n_mod=len(mods)),
        grid=(m // tm,),
        in_specs=in_specs,
        out_specs=[row] * (1 + len(mods)),
        out_shape=[jax.ShapeDtypeStruct((m, d), F32)] + [jax.ShapeDtypeStruct((m, d), BF16)] * len(mods),
        compiler_params=_cparams(("arbitrary",)),
        name="post_ln",
    )(*args)
    return outs


CONV_SLAB = 512
GROUP_X = D_INNER // SSM_GROUPS
PAIR = 2 * SSM_HEAD_DIM


def _softplus(x):
    return jnp.maximum(x, 0.0) + jnp.log1p(jnp.exp(-jnp.abs(x)))


def _dot_nt(a, b):
    return lax.dot_general(a, b, (((1,), (1,)), ((), ())), preferred_element_type=F32)


def _dot_tn(a, b):
    return lax.dot_general(a, b, (((0,), (0,)), ((), ())), preferred_element_type=F32)


def _ssd_prompt_body(z_ref, xbc_ref, dt_ref, cw_ref, cb_ref, dtb_ref, alog_ref, dexp_ref, nw_ref,
                     g_ref, hfin_ref,
                     ext_ref, xs_ref, b_ref, c_ref, y_ref, state_ref,
                     cs_ref, w_ref, cst_ref, dtt_ref, ecl_ref):
    c = pl.program_id(0)
    L = SSD_CHUNK

    @pl.when(c == 0)
    def _():
        ext_ref[0:SUBLANES, :] = jnp.zeros((SUBLANES, CONV_DIM), F32)
        state_ref[...] = jnp.zeros(state_ref.shape, F32)

    ext_ref[SUBLANES:SUBLANES + L, :] = xbc_ref[...]
    for s in range(CONV_DIM // LANES):
        cols = slice(s * LANES, (s + 1) * LANES)
        acc = cb_ref[:, cols]
        for i in range(CONV_WIDTH):
            lo = SUBLANES - (CONV_WIDTH - 1) + i
            acc = acc + cw_ref[i:i + 1, cols] * ext_ref[lo:lo + L, cols]
        a = _silu(acc)
        if s < D_INNER // LANES:
            per = GROUP_X // LANES
            xs_ref[s // per, :, (s % per) * LANES:(s % per + 1) * LANES] = a
        else:
            t = s - D_INNER // LANES
            dst = b_ref if t < SSM_GROUPS else c_ref
            dst[t % SSM_GROUPS] = a
    ext_ref[0:SUBLANES, :] = ext_ref[L:L + SUBLANES, :]

    lane = lax.broadcasted_iota(jnp.int32, (L, LANES), 1)
    row = lax.broadcasted_iota(jnp.int32, (L, LANES), 0)
    head_ok = lane < SSM_HEADS
    dt = jnp.where(head_ok, _softplus(dt_ref[...] + dtb_ref[...]), 0.0)
    da = dt * (-jnp.exp(alog_ref[...]))
    causal = row >= lane
    tril = jnp.where(causal, 1.0, 0.0).astype(BF16)
    cs = sum(jnp.dot(tril, p, preferred_element_type=F32) for p in _split3(da))
    cs_last = cs[L - 1:L, :]
    cs_ref[...] = cs
    w_ref[...] = dt * jnp.exp(cs_last - cs)
    cst = cs.T
    cst_ref[...] = cst
    dtt_ref[...] = dt.T
    ecl_ref[...] = jnp.broadcast_to(jnp.exp(cst[:, L - 1:L]), (LANES, LANES))
    lane_lo = lane < SSM_HEAD_DIM

    def group(g, carry):
        bg = b_ref[g].astype(BF16)
        cg = c_ref[g].astype(BF16)
        cb = _dot_nt(cg, bg)
        yoff = _dot_nt(cg, state_ref[g].astype(BF16))
        shift = LANES - HEADS_PER_GROUP * g
        csg = pltpu.roll(cs_ref[...], shift, 1)
        wg = pltpu.roll(w_ref[...], shift, 1)
        g8 = pl.multiple_of(g * HEADS_PER_GROUP, HEADS_PER_GROUP)
        cstg = cst_ref[pl.ds(g8, HEADS_PER_GROUP), :]
        dttg = dtt_ref[pl.ds(g8, HEADS_PER_GROUP), :]
        eclg = ecl_ref[pl.ds(g8, HEADS_PER_GROUP), :]
        for pr in range(HEADS_PER_GROUP // 2):
            x2 = xs_ref[g, :, pr * PAIR:(pr + 1) * PAIR]
            ms, ecols, wcols, decs = [], [], [], []
            for k in (2 * pr, 2 * pr + 1):
                col = jnp.broadcast_to(csg[:, k:k + 1], (L, LANES))
                rowv = jnp.broadcast_to(cstg[k:k + 1, :], (L, LANES))
                lmat = jnp.exp(jnp.where(causal, col - rowv, NEG_BIG))
                ms.append((cb * lmat * jnp.broadcast_to(dttg[k:k + 1, :], (L, LANES))).astype(BF16))
                ecols.append(jnp.exp(col))
                wcols.append(jnp.broadcast_to(wg[:, k:k + 1], (L, LANES)))
                decs.append(jnp.broadcast_to(eclg[k:k + 1, :], (SSM_HEAD_DIM, LANES)))
            lhs = jnp.concatenate(ms, axis=1)
            rhs = jnp.concatenate([jnp.where(lane_lo, x2, 0.0), jnp.where(lane_lo, 0.0, x2)], axis=0).astype(BF16)
            ydiag = jnp.dot(lhs, rhs, preferred_element_type=F32)
            y_ref[g, :, pr * PAIR:(pr + 1) * PAIR] = ydiag + yoff[:, pr * PAIR:(pr + 1) * PAIR] * jnp.where(lane_lo, ecols[0], ecols[1])
            xw = (x2 * jnp.where(lane_lo, wcols[0], wcols[1])).astype(BF16)
            snew = _dot_tn(xw, bg)
            rows = slice(pr * PAIR, (pr + 1) * PAIR)
            state_ref[g, rows, :] = state_ref[g, rows, :] * jnp.concatenate(decs, axis=0) + snew
        return carry

    lax.fori_loop(0, SSM_GROUPS, group, 0)

    for g in range(SSM_GROUPS):
        cols = slice(g * GROUP_X, (g + 1) * GROUP_X)
        y = y_ref[g] + dexp_ref[:, cols] * xs_ref[g]
        gt = y * _silu(z_ref[:, cols])
        ms = jnp.mean(gt * gt, axis=-1, keepdims=True)
        g_ref[:, cols] = (gt * lax.rsqrt(ms + RMS_EPS) * nw_ref[:, cols]).astype(g_ref.dtype)

    @pl.when(c == pl.num_programs(0) - 1)
    def _():
        hfin_ref[...] = state_ref[...]


def ssd_prompt(z, xbc, dt_raw, conv_w, conv_b, dt_bias, a_log, d_exp, norm_w):
    s = z.shape[0]
    L = SSD_CHUNK
    vec = lambda n: pl.BlockSpec((1, n), lambda c: (0, 0))
    return pl.pallas_call(
        _ssd_prompt_body,
        grid=(s // L,),
        in_specs=[
            pl.BlockSpec((L, D_INNER), lambda c: (c, 0)),
            pl.BlockSpec((L, CONV_DIM), lambda c: (c, 0)),
            pl.BlockSpec((L, LANES), lambda c: (c, 0)),
            pl.BlockSpec((CONV_WIDTH, CONV_DIM), lambda c: (0, 0)),
            vec(CONV_DIM), vec(LANES), vec(LANES), vec(D_INNER), vec(D_INNER),
        ],
        out_specs=[
            pl.BlockSpec((L, D_INNER), lambda c: (c, 0)),
            pl.BlockSpec((SSM_GROUPS, GROUP_X, D_STATE), lambda c: (0, 0, 0)),
        ],
        out_shape=[
            jax.ShapeDtypeStruct((s, D_INNER), BF16),
            jax.ShapeDtypeStruct((SSM_GROUPS, GROUP_X, D_STATE), F32),
        ],
        scratch_shapes=[
            pltpu.VMEM((L + 2 * SUBLANES, CONV_DIM), F32),
            pltpu.VMEM((SSM_GROUPS, L, GROUP_X), F32),
            pltpu.VMEM((SSM_GROUPS, L, D_STATE), F32),
            pltpu.VMEM((SSM_GROUPS, L, D_STATE), F32),
            pltpu.VMEM((SSM_GROUPS, L, GROUP_X), F32),
            pltpu.VMEM((SSM_GROUPS, GROUP_X, D_STATE), F32),
            pltpu.VMEM((L, LANES), F32), pltpu.VMEM((L, LANES), F32),
            pltpu.VMEM((LANES, L), F32), pltpu.VMEM((LANES, L), F32),
            pltpu.VMEM((LANES, LANES), F32),
        ],
        compiler_params=_cparams(("arbitrary",)),
        name="ssd_prompt",
    )(z, xbc, dt_raw, conv_w, conv_b, dt_bias, a_log, d_exp, norm_w)


def _decode_conv_body(s0_ref, s1_ref, s2_ref, xn_ref, cw_ref, cb_ref, o_ref):
    acc = cb_ref[...] + cw_ref[0:1, :] * s0_ref[...] + cw_ref[1:2, :] * s1_ref[...]
    acc = acc + cw_ref[2:3, :] * s2_ref[...] + cw_ref[3:4, :] * xn_ref[...]
    o_ref[...] = _silu(acc)


def decode_conv(s0, s1, s2, xnew, conv_w, conv_b):
    b, n = xnew.shape
    blk = pl.BlockSpec((b, CONV_SLAB), lambda j: (0, j))
    return pl.pallas_call(
        _decode_conv_body,
        grid=(n // CONV_SLAB,),
        in_specs=[blk, blk, blk, blk, pl.BlockSpec((CONV_WIDTH, CONV_SLAB), lambda j: (0, j)),
                  pl.BlockSpec((1, CONV_SLAB), lambda j: (0, j))],
        out_specs=blk,
        out_shape=jax.ShapeDtypeStruct((b, n), F32),
        compiler_params=_cparams(("arbitrary",)),
        name="decode_conv",
    )(s0, s1, s2, xnew, conv_w, conv_b)


def _decode_prep_body(xs_ref, b_ref, c_ref, dt_ref, dtb_ref, alog_ref, e_ref, dtx_ref, daexp_ref, bcx_ref):
    lane = lax.broadcasted_iota(jnp.int32, dt_ref.shape, 1)
    dt = jnp.where(lane < SSM_HEADS, _softplus(dt_ref[...] + dtb_ref[...]), 0.0)
    da = jnp.where(lane < SSM_HEADS, jnp.exp(dt * (-jnp.exp(alog_ref[...]))), 0.0)
    e = e_ref[...]
    expand = lambda v: sum(jnp.dot(p, e, preferred_element_type=F32) for p in _split3(v))
    dtx = expand(dt) * xs_ref[...]
    dtx_ref[...] = dtx
    daexp_ref[...] = expand(da)
    bcx_ref[...] = dtx * jnp.sum(b_ref[...] * c_ref[...], axis=-1, keepdims=True)


def decode_prep(xbc_act, dt_raw, dt_bias, a_log, expand_mat):
    b = xbc_act.shape[0]
    xblk = pl.BlockSpec((b, GROUP_X), lambda g: (0, g))
    vec = pl.BlockSpec((1, LANES), lambda g: (0, 0))
    nb = D_INNER // D_STATE
    out = jax.ShapeDtypeStruct((b, D_INNER), F32)
    return pl.pallas_call(
        _decode_prep_body,
        grid=(SSM_GROUPS,),
        in_specs=[
            xblk,
            pl.BlockSpec((b, D_STATE), lambda g: (0, nb + g)),
            pl.BlockSpec((b, D_STATE), lambda g: (0, nb + SSM_GROUPS + g)),
            pl.BlockSpec((b, LANES), lambda g: (0, 0)),
            vec, vec,
            pl.BlockSpec((LANES, GROUP_X), lambda g: (0, g)),
        ],
        out_specs=[xblk, xblk, xblk],
        out_shape=[out, out, out],
        compiler_params=_cparams(("arbitrary",)),
        name="decode_prep",
    )(xbc_act, xbc_act, xbc_act, dt_raw, dt_bias, a_log, expand_mat)


TILE_ROWS = 128
SPLIT_ROWS = 16


def _decode_state_body(h_ref, dtx_ref, da_ref, b_ref, c_ref, ho_ref, hc_ref):
    n_tiles = (SSM_HEADS * SSM_HEAD_DIM) // TILE_ROWS
    rid = lambda w: lax.broadcasted_iota(jnp.int32, (SPLIT_ROWS, w), 0)
    pick = lambda r, pieces, w: functools.reduce(
        lambda acc, kv: jnp.where(r == kv[0], jnp.broadcast_to(kv[1].astype(F32), (SPLIT_ROWS, w)), acc),
        pieces, jnp.zeros((SPLIT_ROWS, w), F32))
    xh, xm, xl = _split3(dtx_ref[...])
    ah, am, al = _split3(da_ref[...])
    bh, bm, bl = _split3(b_ref[...])
    r = rid(D_INNER)
    lmat = pick(r, [(0, xh), (1, xh), (2, xh), (3, xm), (4, xm), (5, xl), (6, ah), (7, am), (8, al)], D_INNER)
    rb = rid(SSM_GROUPS * D_STATE)
    ru = pick(rb, [(0, bh), (1, bm), (2, bl), (3, bh), (4, bm), (5, bh)], SSM_GROUPS * D_STATE).astype(BF16)
    r1 = rid(D_STATE)
    rd = jnp.where(jnp.logical_and(r1 >= 6, r1 <= 8), 1.0, 0.0).astype(BF16)
    crow = c_ref[...]
    for i in range(n_tiles):
        g = (i * TILE_ROWS) // GROUP_X
        lanes = slice(i * TILE_ROWS, (i + 1) * TILE_ROWS)
        lt = lmat[:, lanes].T.astype(BF16)
        upd = jnp.dot(lt, ru[:, g * D_STATE:(g + 1) * D_STATE], preferred_element_type=F32)
        dec = jnp.dot(lt, rd, preferred_element_type=F32)
        h = h_ref[lanes, :]
        ho_ref[lanes, :] = h * dec + upd
        t = (h * crow[:, g * D_STATE:(g + 1) * D_STATE]).T
        hc_ref[0, :, lanes] = jnp.sum(t, axis=0, keepdims=True)


def decode_state(h, dtx, daexp, xbc_act3):
    b = h.shape[0]
    rows = SSM_HEADS * SSM_HEAD_DIM
    hblk = pl.BlockSpec((None, rows, D_STATE), lambda i: (i, 0, 0))
    rblk = pl.BlockSpec((None, 1, D_INNER), lambda i: (i, 0, 0))
    bc = SSM_GROUPS * D_STATE
    return pl.pallas_call(
        _decode_state_body,
        grid=(b,),
        in_specs=[
            hblk, rblk, rblk,
            pl.BlockSpec((None, 1, bc), lambda i: (i, 0, D_INNER // bc)),
            pl.BlockSpec((None, 1, bc), lambda i: (i, 0, D_INNER // bc + 1)),
        ],
        out_specs=[hblk, pl.BlockSpec((1, 1, D_INNER), lambda i: (i, 0, 0))],
        out_shape=[jax.ShapeDtypeStruct((b, rows, D_STATE), F32), jax.ShapeDtypeStruct((b, 1, D_INNER), F32)],
        compiler_params=_cparams(("arbitrary",)),
        name="decode_state",
    )(h, dtx, daexp, xbc_act3, xbc_act3)


def _decode_post_body(hc_ref, da_ref, bcx_ref, xs_ref, z_ref, dexp_ref, nw_ref, o_ref):
    y = da_ref[...] * hc_ref[...] + bcx_ref[...] + dexp_ref[...] * xs_ref[...]
    gt = y * _silu(z_ref[...])
    ms = jnp.mean(gt * gt, axis=-1, keepdims=True)
    o_ref[...] = (gt * lax.rsqrt(ms + RMS_EPS) * nw_ref[...]).astype(o_ref.dtype)


def decode_post(hc, daexp, bcx, xbc_act, z, d_exp, norm_w):
    b = hc.shape[0]
    xblk = pl.BlockSpec((b, GROUP_X), lambda g: (0, g))
    vblk = pl.BlockSpec((1, GROUP_X), lambda g: (0, g))
    return pl.pallas_call(
        _decode_post_body,
        grid=(SSM_GROUPS,),
        in_specs=[xblk, xblk, xblk, xblk, xblk, vblk, vblk],
        out_specs=xblk,
        out_shape=jax.ShapeDtypeStruct((b, D_INNER), BF16),
        compiler_params=_cparams(("arbitrary",)),
        name="decode_post",
    )(hc, daexp, bcx, xbc_act, z, d_exp, norm_w)


SUPER = WINDOW_MAX
ATT_N = 128
MERGE_ROWS = 256


def _rel_bucket(dist):
    max_exact = NUM_BUCKETS // 2
    df = jnp.maximum(dist, max_exact).astype(F32)
    large = max_exact + (jnp.log(df / max_exact) / math.log(MAX_DISTANCE / max_exact)
                         * (NUM_BUCKETS - max_exact)).astype(jnp.int32)
    return jnp.where(dist < max_exact, dist, jnp.minimum(large, NUM_BUCKETS - 1))


def _prompt_bias_rows(rel_bias):
    n = ATT_N
    back = n - jnp.arange(2 * n)
    rows = []
    for g, (_, dil) in enumerate(DILATION_GROUPS):
        tab = rel_bias[:, g * N_HEADS_B:(g + 1) * N_HEADS_B]
        bias = tab[_rel_bucket(jnp.clip(back, 0, n) * dil)].astype(F32)
        rows.append(jnp.where((back >= 0)[:, None], bias, NEG_BIG).T)
    return jnp.stack(rows, axis=0)[:, :, None, :]


def _rows(start, size, stride):
    return pl.ds(start, size) if stride == 1 else pl.ds(start, size, stride=stride)


def _attn_prompt_body(q0_ref, q1_ref, q2_ref, kc_ref, kp_ref, vc_ref, vp_ref, t_ref, o_ref, og_ref, lse_ref):
    first = pl.program_id(0) == 0
    n = ATT_N
    scale = HEAD_DIM ** -0.5
    key_id = lax.broadcasted_iota(jnp.int32, (n, 2 * n), 1)
    q_refs = (q0_ref, q1_ref, q2_ref)
    for g, (win, dil) in enumerate(DILATION_GROUPS):
        nblk = SUPER // (n * dil)
        bias = pltpu.roll(jnp.broadcast_to(t_ref[g, 0], (n, 2 * n)), 0, 1, stride=1, stride_axis=0)
        for r in range(dil):
            for c in range(nblk):
                start = c * n * dil + r
                cur = _rows(start, n, dil)
                qb = q_refs[g][cur, :].astype(BF16)
                if c > 0:
                    prev = _rows(start - n * dil, n, dil)
                    kprev, vprev = kc_ref[prev, :], vc_ref[prev, :]
                else:
                    prev = _rows((nblk - 1) * n * dil + r, n, dil)
                    kprev, vprev = kp_ref[prev, :], vp_ref[prev, :]
                keys = jnp.concatenate([kprev, kc_ref[cur, :]], axis=0).astype(BF16)
                vals = jnp.concatenate([vprev, vc_ref[cur, :]], axis=0).astype(BF16)
                s = _dot_nt(qb, keys) * scale + bias
                if c == 0:
                    s = jnp.where(jnp.logical_and(first, key_id < n), NEG_BIG, s)
                m = jnp.max(s, axis=-1, keepdims=True)
                p = jnp.exp(s - m)
                den = jnp.sum(p, axis=-1, keepdims=True)
                o = jnp.dot(p.astype(BF16), vals, preferred_element_type=F32) * (1.0 / den)
                og_ref[g, cur, :] = o
                lse_ref[g, cur, :] = jnp.broadcast_to(m + jnp.log(den), (n, LANES))
    for i in range(SUPER // MERGE_ROWS):
        rows = slice(i * MERGE_ROWS, (i + 1) * MERGE_ROWS)
        ls = [lse_ref[g, rows, :] for g in range(N_GROUPS_B)]
        mx = functools.reduce(jnp.maximum, ls)
        es = [jnp.exp(l - mx) for l in ls]
        tot = functools.reduce(lambda a, b: a + b, es)
        acc = functools.reduce(lambda a, b: a + b, [es[g] * og_ref[g, rows, :] for g in range(N_GROUPS_B)])
        o_ref[rows, :] = (acc * (1.0 / tot)).astype(o_ref.dtype)


def attn_prompt(q, kv, bias_rows):
    s = q.shape[0]
    blk = lambda f: pl.BlockSpec((SUPER, HEAD_DIM), f)
    return pl.pallas_call(
        _attn_prompt_body,
        grid=(s // SUPER, N_HEADS_B),
        in_specs=[
            blk(lambda c, h: (c, h)),
            blk(lambda c, h: (c, N_HEADS_B + h)),
            blk(lambda c, h: (c, 2 * N_HEADS_B + h)),
            blk(lambda c, h: (c, h)),
            blk(lambda c, h: (jnp.maximum(c - 1, 0), h)),
            blk(lambda c, h: (c, N_HEADS_B + h)),
            blk(lambda c, h: (jnp.maximum(c - 1, 0), N_HEADS_B + h)),
            pl.BlockSpec((N_GROUPS_B, 1, 1, 2 * ATT_N), lambda c, h: (0, h, 0, 0)),
        ],
        out_specs=blk(lambda c, h: (c, h)),
        out_shape=jax.ShapeDtypeStruct((s, KV_DIM), BF16),
        scratch_shapes=[
            pltpu.VMEM((N_GROUPS_B, SUPER, HEAD_DIM), F32),
            pltpu.VMEM((N_GROUPS_B, SUPER, LANES), F32),
        ],
        compiler_params=_cparams(("arbitrary", "arbitrary")),
        name="attn_prompt",
    )(q, q, q, kv, kv, kv, kv, bias_rows)


N_BUF = min(WINDOW_MAX, PAST_LEN)
ROW_MAJOR = 16
ROW_MID = 4


def _decode_bias_tables(rel_bias):
    n = ATT_N
    j = n - jnp.arange(n)
    cached, new = [], []
    for g, (_, dil) in enumerate(DILATION_GROUPS):
        tab = rel_bias[:, g * N_HEADS_B:(g + 1) * N_HEADS_B]
        cached.append(tab[_rel_bucket(j * dil)])
        new.append(tab[_rel_bucket(jnp.zeros((), jnp.int32))])
    cached = jnp.broadcast_to(jnp.stack(cached)[..., None], (N_GROUPS_B, n, N_HEADS_B, LANES)).astype(F32)
    new = jnp.broadcast_to(jnp.stack(new)[..., None], (N_GROUPS_B, N_HEADS_B, LANES)).astype(F32)
    return cached, new


def _attn_decode_body(q_ref, kvn_ref, k0_ref, v0_ref, k1_ref, v1_ref, k2_ref, v2_ref, bc_ref, bn_ref, o_ref):
    n = ATT_N
    scale = HEAD_DIM ** -0.5
    kn, vn = kvn_ref[0], kvn_ref[1]
    kv_refs = ((k0_ref, v0_ref), (k1_ref, v1_ref), (k2_ref, v2_ref))
    outs, lses = [], []
    for g in range(N_GROUPS_B):
        q = q_ref[g]
        k = kv_refs[g][0][...].reshape(n, N_HEADS_B, HEAD_DIM)
        v = kv_refs[g][1][...].reshape(n, N_HEADS_B, HEAD_DIM)
        s = jnp.sum(k * q[None], axis=-1, keepdims=True) * scale + bc_ref[g]
        s0 = jnp.sum(kn * q, axis=-1, keepdims=True) * scale + bn_ref[g]
        m = jnp.maximum(jnp.max(s, axis=0), s0)
        p = jnp.exp(s - m[None])
        p0 = jnp.exp(s0 - m)
        den = jnp.sum(p, axis=0) + p0
        outs.append((jnp.sum(p * v, axis=0) + p0 * vn) * (1.0 / den))
        lses.append(m + jnp.log(den))
    mx = functools.reduce(jnp.maximum, lses)
    es = [jnp.exp(l - mx) for l in lses]
    tot = functools.reduce(lambda a, b: a + b, es)
    acc = functools.reduce(lambda a, b: a + b, [e * o for e, o in zip(es, outs)])
    o_ref[...] = (acc * (1.0 / tot)).astype(o_ref.dtype)


def attn_decode(q, kv_new, cache_k, cache_v, bias_cached, bias_new):
    b = q.shape[0]
    view = (b, N_BUF // ROW_MAJOR, ROW_MAJOR // ROW_MID, ROW_MID, N_HEADS_B, HEAD_DIM)
    ck, cv = cache_k.reshape(view), cache_v.reshape(view)
    n = ATT_N
    hd = (N_HEADS_B, HEAD_DIM)
    g0 = pl.BlockSpec((None, n // ROW_MAJOR, ROW_MAJOR // ROW_MID, ROW_MID) + hd,
                      lambda i: (i, N_BUF // n - 1, 0, 0, 0, 0))
    g1 = pl.BlockSpec((None, n * ROW_MID // ROW_MAJOR, ROW_MAJOR // ROW_MID, None) + hd,
                      lambda i: (i, N_BUF // (n * ROW_MID) - 1, 0, 0, 0, 0))
    g2 = pl.BlockSpec((None, n, None, None) + hd, lambda i: (i, 0, 0, 0, 0, 0))
    return pl.pallas_call(
        _attn_decode_body,
        grid=(b,),
        in_specs=[
            pl.BlockSpec((None, N_GROUPS_B) + hd, lambda i: (i, 0, 0, 0)),
            pl.BlockSpec((None, 2) + hd, lambda i: (i, 0, 0, 0)),
            g0, g0, g1, g1, g2, g2,
            pl.BlockSpec((N_GROUPS_B, n) + (N_HEADS_B, LANES), lambda i: (0, 0, 0, 0)),
            pl.BlockSpec((N_GROUPS_B, N_HEADS_B, LANES), lambda i: (0, 0, 0)),
        ],
        out_specs=pl.BlockSpec((None,) + hd, lambda i: (i, 0, 0)),
        out_shape=jax.ShapeDtypeStruct((b,) + hd, BF16),
        compiler_params=_cparams(("arbitrary",)),
        name="attn_decode",
    )(q, kv_new, ck, cv, ck, cv, ck, cv, bias_cached, bias_new)


ADA_ROWS = 136
Z_TILE = 1024


def _pad_lanes(v):
    return jnp.pad(v.astype(F32), (0, LANES - v.shape[0]))[None]


TM = 512
TM_EW = 256


def _trunk(xs, adas, mixers_a, mixers_b, p):
    tms = (TM_EW, xs[1].shape[0])
    vec = lambda a, l, s: a[l, s][None]
    sh1, sc1, g1, sh2, sc2, g2 = zip(*(a[0] for a in adas))
    sh1b, sc1b, g1b, sh2b, sc2b, g2b = zip(*(a[1] for a in adas))
    shkv, sckv = zip(*(a[2] for a in adas))
    paths = range(2)

    def ln(x, y, gate, l, s, mods):
        outs = [post_ln(x[i], y[i], gate[i], vec(p["ln_g"], l, s), vec(p["ln_b"], l, s),
                        [(sc[i], sh[i]) for sc, sh in mods], tm=tms[i]) for i in paths]
        return zip(*outs)

    def mm(a, w, **kw):
        tm = kw.pop("tm", TM)
        return matmul(a[0], w, rider=a[1], tm=tm, **kw)

    u = [modulate(xs[i], sc1[i], sh1[i], tm=tms[i]) for i in paths]
    w_in_t = jnp.swapaxes(p["ssm_w_in"], 1, 2).reshape(-1, D_MODEL)
    z = mm(u, w_in_t, w_transposed=True, tn=512, n_out=D_INNER, name="in_z")
    xbc = mm(u, w_in_t, w_transposed=True, tm=2 * TM, tn=Z_TILE, n_out=CONV_DIM, col_off=D_INNER // Z_TILE, name="in_xbc")
    dt = mm(u, w_in_t, w_transposed=True, tn=LANES, n_out=LANES, col_off=(D_INNER + CONV_DIM) // LANES, name="in_dt")
    gated, extras = zip(*(mixers_a[i](z[i], xbc[i], dt[i]) for i in paths))
    y = mm(gated, p["ssm_w_out"].reshape(D_INNER, D_MODEL), tn=512, name="ssm_out")
    x, u = ln(xs, y, g1, 0, 0, [(sc2, sh2)])
    h = glu_matmul(u[0], u[1], p["ffn_w_gate"], p["ffn_w_up"], layer=0, tm=2 * TM, tn=512, name="ffn0_glu")
    y = mm(h, p["ffn_w_down"], layer=0, tn=512, name="ffn0_down")
    x, u, ukv = ln(x, y, g2, 0, 1, [(sc1b, sh1b), (sckv, shkv)])

    kv = mm(ukv, p["w_kv"], tn=1024, name="kv_proj")
    q = mm(u, p["attn_w_q"], layer=0, tn=1024, name="q_proj")
    o = [mixers_b[i](q[i], kv[i]) for i in paths]
    y = mm(o, p["attn_w_o"], layer=0, tn=512, name="attn_out")
    x, u = ln(x, y, g1b, 1, 0, [(sc2b, sh2b)])
    h = glu_matmul(u[0], u[1], p["ffn_w_gate"], p["ffn_w_up"], layer=1, tm=TM, tn=512, name="ffn1_glu")
    y = mm(h, p["ffn_w_down"], layer=1, tn=512, name="ffn1_down")
    (x,) = ln(x, y, g2b, 1, 1, [])
    return [(x[i], extras[i], kv[i]) for i in paths]


def kernel(x_prompt, x_sample, state_conv, state_ssm, cache_k, cache_v, c_prompt, c_sample, w_ada, b_ada, ln_g, ln_b, ffn_w_gate, ffn_w_up, ffn_w_down, ssm_w_in, ssm_conv_w, ssm_conv_b, ssm_dt_bias, ssm_a_log, ssm_d, ssm_norm_w, ssm_w_out, kv_w_ada, kv_b_ada, w_kv, attn_w_q, attn_w_o, rel_bias):
    p = dict(ln_g=ln_g, ln_b=ln_b, ffn_w_gate=ffn_w_gate, ffn_w_up=ffn_w_up, ffn_w_down=ffn_w_down,
             ssm_w_in=ssm_w_in, ssm_w_out=ssm_w_out, w_kv=w_kv, attn_w_q=attn_w_q, attn_w_o=attn_w_o)
    s, b, d = SEQ, DEC_BATCH, D_MODEL

    c_all = jnp.concatenate([c_sample, c_prompt, jnp.zeros((ADA_ROWS - b - 1, d), F32)], axis=0)
    ada = [matmul(c_all, w_ada, layer=l, tm=ADA_ROWS, tn=1024, bias=b_ada[l][None], act="silu", name=f"ada{l}")
           for l in range(DEPTH)]
    adakv = matmul(c_all, kv_w_ada, tm=ADA_ROWS, tn=1024, bias=kv_b_ada[None], act="silu", name="ada_kv")
    split = lambda m, n, rows: [m[rows, i * d:(i + 1) * d] for i in range(n)]
    smp, prm = slice(0, b), slice(b, b + 1)

    conv_w, conv_b = ssm_conv_w[0], ssm_conv_b[0][None]
    dt_bias, a_log = _pad_lanes(ssm_dt_bias[0]), _pad_lanes(ssm_a_log[0])
    d_exp = jnp.repeat(ssm_d[0].astype(F32), SSM_HEAD_DIM)[None]
    norm_w = ssm_norm_w[0][None]

    def mixer_a_prompt(z, xbc, dt):
        gated, hfin = ssd_prompt(z, xbc, dt, conv_w, conv_b, dt_bias, a_log, d_exp, norm_w)
        return gated, (xbc[s - (CONV_WIDTH - 1):], hfin)

    bias_rows = _prompt_bias_rows(rel_bias)
    mixer_b_prompt = lambda q, kv: attn_prompt(q, kv, bias_rows)

    expand_mat = (lax.broadcasted_iota(jnp.int32, (LANES, D_INNER), 0)
                  == lax.broadcasted_iota(jnp.int32, (LANES, D_INNER), 1) // SSM_HEAD_DIM).astype(BF16)

    def mixer_a_decode(z, xbc, dt):
        st = state_conv[0]
        xa = decode_conv(st[:, 0], st[:, 1], st[:, 2], xbc, conv_w, conv_b)
        dtx, daexp, bcx = decode_prep(xa, dt, dt_bias, a_log, expand_mat)
        hnew, hc = decode_state(state_ssm.reshape(b, SSM_HEADS * SSM_HEAD_DIM, D_STATE),
                                dtx[:, None], daexp[:, None], xa[:, None])
        gated = decode_post(hc[:, 0], daexp, bcx, xa, z, d_exp, norm_w)
        new_conv = jnp.concatenate([st[:, 1:], xbc[:, None]], axis=1)
        return gated, (new_conv, hnew)

    bias_cached, bias_new = _decode_bias_tables(rel_bias)

    def mixer_b_decode(q, kv):
        o = attn_decode(q.reshape(b, N_GROUPS_B, N_HEADS_B, HEAD_DIM), kv.reshape(b, 2, N_HEADS_B, HEAD_DIM),
                        cache_k, cache_v, bias_cached, bias_new)
        return o.reshape(b, KV_DIM)

    adas = [(split(ada[0], 6, rows), split(ada[1], 6, rows), split(adakv, 2, rows)) for rows in (prm, smp)]
    (y_p, (conv_p, ssm_p), kv_p), (y_s, (conv_s, ssm_s), kv_s) = _trunk(
        (x_prompt.reshape(s, d), x_sample.reshape(b, d)), adas,
        (mixer_a_prompt, mixer_a_decode), (mixer_b_prompt, mixer_b_decode), p)

    n_keep = min(WINDOW_MAX, s)
    heads = (N_HEADS_B, HEAD_DIM)
    return (
        y_p.reshape(1, s, d),
        y_s.reshape(b, 1, d),
        conv_p.reshape(1, 1, CONV_WIDTH - 1, CONV_DIM),
        ssm_p.reshape(1, 1, SSM_HEADS, SSM_HEAD_DIM, D_STATE),
        kv_p[s - n_keep:, :KV_DIM].reshape((1, n_keep) + heads),
        kv_p[s - n_keep:, KV_DIM:].reshape((1, n_keep) + heads),
        conv_s.reshape(1, b, CONV_WIDTH - 1, CONV_DIM),
        ssm_s.reshape(1, b, SSM_HEADS, SSM_HEAD_DIM, D_STATE),
        kv_s[:, :KV_DIM].reshape((b, 1) + heads),
        kv_s[:, KV_DIM:].reshape((b, 1) + heads),
    )
```

```python
import functools
import math

import jax
import jax.numpy as jnp
from jax import lax
from jax.experimental import pallas as pl
from jax.experimental.pallas import tpu as pltpu

F32 = jnp.float32
BF16 = jnp.bfloat16

D_MODEL = 2048
SEQ = 8192
DEPTH = 2
DEC_BATCH = 128
PAST_LEN = 2048
D_INNER = 2 * D_MODEL
SSM_HEAD_DIM = 64
SSM_HEADS = D_INNER // SSM_HEAD_DIM
SSM_GROUPS = 8
HEADS_PER_GROUP = SSM_HEADS // SSM_GROUPS
D_STATE = 128
CONV_WIDTH = 4
CONV_DIM = D_INNER + 2 * SSM_GROUPS * D_STATE
SSD_CHUNK = 128
RMS_EPS = 1e-5
HEAD_DIM = 128
N_HEADS_B = D_MODEL // HEAD_DIM
DILATION_GROUPS = ((128, 1), (512, 4), (2048, 16))
N_GROUPS_B = len(DILATION_GROUPS)
WINDOW_MAX = max(w for w, _ in DILATION_GROUPS)
Q_DIM = N_GROUPS_B * N_HEADS_B * HEAD_DIM
KV_DIM = N_HEADS_B * HEAD_DIM
NUM_BUCKETS = 32
MAX_DISTANCE = WINDOW_MAX
D_FF = -(-8 * D_MODEL // (3 * 256)) * 256
ALPHA = (2 * DEPTH) ** 0.25
LN_EPS = 1e-5

LANES = 128
SUBLANES = 8
VMEM_LIMIT_BYTES = 56 * 1024 * 1024
NEG_BIG = -1e30


def _cparams(semantics):
    return pltpu.CompilerParams(dimension_semantics=semantics, vmem_limit_bytes=VMEM_LIMIT_BYTES)


def _silu(x):
    return x * (1.0 / (1.0 + jnp.exp(-x)))


def _split3(x):
    hi = x.astype(BF16)
    r1 = x - hi.astype(F32)
    mid = r1.astype(BF16)
    lo = (r1 - mid.astype(F32)).astype(BF16)
    return hi, mid, lo


CAST_ROWS = 256


def _cast_tile(w_ref, wbf_ref, transposed=False):
    k, tn = wbf_ref.shape
    if transposed:
        step = min(CAST_ROWS, tn)
        for c in range(tn // step):
            wbf_ref[:, c * step:(c + 1) * step] = w_ref[c * step:(c + 1) * step, :].T.astype(BF16)
        return
    rows = CAST_ROWS if k % CAST_ROWS == 0 else k
    def body(r, c):
        off = pl.multiple_of(r * rows, rows)
        wbf_ref[pl.ds(off, rows), :] = w_ref[pl.ds(off, rows), :].astype(BF16)
        return c
    lax.fori_loop(0, k // rows, body, 0)


def _mm_body(*refs, act, has_bias, has_rider, w_transposed):
    refs = list(refs)
    a_ref, w_ref = refs.pop(0), refs.pop(0)
    b_ref = refs.pop(0) if has_bias else None
    a2_ref = refs.pop(0) if has_rider else None
    o_ref = refs.pop(0)
    o2_ref = refs.pop(0) if has_rider else None
    wbf_ref = refs.pop(0)

    def product(lhs_ref):
        a = lhs_ref[...]
        if act == "silu":
            a = _silu(a.astype(F32))
        acc = jnp.dot(a.astype(BF16), wbf_ref[...], preferred_element_type=F32)
        return acc + b_ref[...] if has_bias else acc

    @pl.when(pl.program_id(1) == 0)
    def _():
        _cast_tile(w_ref, wbf_ref, w_transposed)
        if has_rider:
            o2_ref[...] = product(a2_ref).astype(o2_ref.dtype)

    o_ref[...] = product(a_ref).astype(o_ref.dtype)


def _weight_spec(w, layer, tn, col_off=0, transposed=False):
    if transposed:
        return pl.BlockSpec((tn, w.shape[1]), lambda j, i: (j + col_off, 0))
    k = w.shape[-2]
    if w.ndim == 2:
        return pl.BlockSpec((k, tn), lambda j, i: (0, j + col_off))
    return pl.BlockSpec((None, k, tn), lambda j, i: (layer, 0, j + col_off))


def matmul(a, w, *, tm, tn, layer=0, n_out=None, col_off=0, bias=None, act=None, rider=None,
           w_transposed=False, out_dtype=F32, name="mm"):
    m, k = a.shape
    n_out = (w.shape[0] if w_transposed else w.shape[-1]) if n_out is None else n_out
    grid = (pl.cdiv(n_out, tn), m // tm)
    in_specs = [
        pl.BlockSpec((tm, k), lambda j, i: (i, 0)),
        _weight_spec(w, layer, tn, col_off, w_transposed),
    ]
    args = [a, w]
    out_specs = [pl.BlockSpec((tm, tn), lambda j, i: (i, j))]
    out_shape = [jax.ShapeDtypeStruct((m, n_out), out_dtype)]
    if bias is not None:
        in_specs.append(pl.BlockSpec((1, tn), lambda j, i: (0, j + col_off)))
        args.append(bias)
    if rider is not None:
        m2 = rider.shape[0]
        in_specs.append(pl.BlockSpec((m2, k), lambda j, i: (0, 0)))
        args.append(rider)
        out_specs.append(pl.BlockSpec((m2, tn), lambda j, i: (0, j)))
        out_shape.append(jax.ShapeDtypeStruct((m2, n_out), out_dtype))
    outs = pl.pallas_call(
        functools.partial(_mm_body, act=act, has_bias=bias is not None, has_rider=rider is not None,
                          w_transposed=w_transposed),
        grid=grid,
        in_specs=in_specs,
        out_specs=out_specs,
        out_shape=out_shape,
        scratch_shapes=[pltpu.VMEM((k, tn), BF16)],
        compiler_params=_cparams(("arbitrary", "arbitrary")),
        name=name,
    )(*args)
    return outs if rider is not None else outs[0]


def _glu_body(a_ref, wg_ref, wu_ref, a2_ref, o_ref, o2_ref, wgbf_ref, wubf_ref):
    def product(lhs_ref):
        a = lhs_ref[...]
        g = jnp.dot(a, wgbf_ref[...], preferred_element_type=F32)
        u = jnp.dot(a, wubf_ref[...], preferred_element_type=F32)
        return (_silu(g) * u).astype(BF16)

    @pl.when(pl.program_id(1) == 0)
    def _():
        _cast_tile(wg_ref, wgbf_ref)
        _cast_tile(wu_ref, wubf_ref)
        o2_ref[...] = product(a2_ref)

    o_ref[...] = product(a_ref)


def glu_matmul(a, rider, w_gate, w_up, *, tm, tn, layer=0, name="glu"):
    m, k = a.shape
    m2 = rider.shape[0]
    n = w_gate.shape[-1]
    return pl.pallas_call(
        _glu_body,
        grid=(n // tn, m // tm),
        in_specs=[
            pl.BlockSpec((tm, k), lambda j, i: (i, 0)),
            _weight_spec(w_gate, layer, tn),
            _weight_spec(w_up, layer, tn),
            pl.BlockSpec((m2, k), lambda j, i: (0, 0)),
        ],
        out_specs=[pl.BlockSpec((tm, tn), lambda j, i: (i, j)), pl.BlockSpec((m2, tn), lambda j, i: (0, j))],
        out_shape=[jax.ShapeDtypeStruct((m, n), BF16), jax.ShapeDtypeStruct((m2, n), BF16)],
        scratch_shapes=[pltpu.VMEM((k, tn), BF16), pltpu.VMEM((k, tn), BF16)],
        compiler_params=_cparams(("arbitrary", "arbitrary")),
        name=name,
    )(a, w_gate, w_up, rider)


def _row_spec(arr, tm):
    if arr.shape[0] == 1:
        return pl.BlockSpec((1, arr.shape[1]), lambda i: (0, 0))
    return pl.BlockSpec((tm, arr.shape[1]), lambda i: (i, 0))


def _modulate_body(x_ref, sc_ref, sh_ref, o_ref):
    o_ref[...] = (x_ref[...] * (1.0 + sc_ref[...]) + sh_ref[...]).astype(o_ref.dtype)


def modulate(x, scale, shift, *, tm):
    m, d = x.shape
    return pl.pallas_call(
        _modulate_body,
        grid=(m // tm,),
        in_specs=[pl.BlockSpec((tm, d), lambda i: (i, 0)), _row_spec(scale, tm), _row_spec(shift, tm)],
        out_specs=pl.BlockSpec((tm, d), lambda i: (i, 0)),
        out_shape=jax.ShapeDtypeStruct((m, d), BF16),
        compiler_params=_cparams(("arbitrary",)),
        name="modulate",
    )(x, scale, shift)


def _ln_body(*refs, n_mod):
    x_ref, y_ref, gate_ref, g_ref, b_ref = refs[:5]
    mod_refs = refs[5:5 + 2 * n_mod]
    xo_ref = refs[5 + 2 * n_mod]
    u_refs = refs[6 + 2 * n_mod:]
    t = ALPHA * x_ref[...] + (1.0 + gate_ref[...]) * y_ref[...]
    mu = jnp.mean(t, axis=-1, keepdims=True)
    tc = t - mu
    var = jnp.mean(tc * tc, axis=-1, keepdims=True)
    xn = tc * lax.rsqrt(var + LN_EPS) * g_ref[...] + b_ref[...]
    xo_ref[...] = xn
    for q in range(n_mod):
        u_refs[q][...] = (xn * (1.0 + mod_refs[2 * q][...]) + mod_refs[2 * q + 1][...]).astype(BF16)


def post_ln(x, y, gate, ln_g, ln_b, mods, *, tm):
    m, d = x.shape
    row = pl.BlockSpec((tm, d), lambda i: (i, 0))
    vec = pl.BlockSpec((1, d), lambda i: (0, 0))
    in_specs = [row, row, _row_spec(gate, tm), vec, vec]
    args = [x, y, gate, ln_g, ln_b]
    for sc, sh in mods:
        in_specs += [_row_spec(sc, tm), _row_spec(sh, tm)]
        args += [sc, sh]
    outs = pl.pallas_call(
        functools.partial(_ln_body, n_mod=len(mods)),
        grid=(m // tm,),
        in_specs=in_specs,
        out_specs=[row] * (1 + len(mods)),
        out_shape=[jax.ShapeDtypeStruct((m, d), F32)] + [jax.ShapeDtypeStruct((m, d), BF16)] * len(mods),
        compiler_params=_cparams(("arbitrary",)),
        name="post_ln",
    )(*args)
    return outs


CONV_SLAB = 512
GROUP_X = D_INNER // SSM_GROUPS
PAIR = 2 * SSM_HEAD_DIM


def _softplus(x):
    return jnp.maximum(x, 0.0) + jnp.log1p(jnp.exp(-jnp.abs(x)))


def _dot_nt(a, b):
    return lax.dot_general(a, b, (((1,), (1,)), ((), ())), preferred_element_type=F32)


def _dot_tn(a, b):
    return lax.dot_general(a, b, (((0,), (0,)), ((), ())), preferred_element_type=F32)


def _head_spread_matrix():
    r = lax.broadcasted_iota(jnp.int32, (SSM_GROUPS, 2 * LANES, HEADS_PER_GROUP * LANES), 1)
    col = lax.broadcasted_iota(jnp.int32, (SSM_GROUPS, 2 * LANES, HEADS_PER_GROUP * LANES), 2)
    g = lax.broadcasted_iota(jnp.int32, (SSM_GROUPS, 2 * LANES, HEADS_PER_GROUP * LANES), 0)
    hit = jnp.logical_and(r % SSM_HEADS == g * HEADS_PER_GROUP + col // LANES, r < 3 * SSM_HEADS)
    return hit.astype(BF16)


def _ssd_prompt_body(z_ref, xbc_ref, dt_ref, cw_ref, cb_ref, dtb_ref, alog_ref, dexp_ref, nw_ref, e_ref,
                     g_ref, hfin_ref,
                     ext_ref, xs_ref, b_ref, c_ref, y_ref, statet_ref,
                     p_ref, wt_ref, cst_ref, dtt_ref):
    c = pl.program_id(0)
    L = SSD_CHUNK

    @pl.when(c == 0)
    def _():
        ext_ref[0:SUBLANES, :] = jnp.zeros((SUBLANES, CONV_DIM), F32)
        statet_ref[...] = jnp.zeros(statet_ref.shape, F32)

    ext_ref[SUBLANES:SUBLANES + L, :] = xbc_ref[...]
    for s in range(CONV_DIM // LANES):
        cols = slice(s * LANES, (s + 1) * LANES)
        acc = cb_ref[:, cols]
        for i in range(CONV_WIDTH):
            lo = SUBLANES - (CONV_WIDTH - 1) + i
            acc = acc + cw_ref[i:i + 1, cols] * ext_ref[lo:lo + L, cols]
        a = _silu(acc)
        if s < D_INNER // LANES:
            per = GROUP_X // LANES
            xs_ref[s // per, :, (s % per) * LANES:(s % per + 1) * LANES] = a
        else:
            t = s - D_INNER // LANES
            dst = b_ref if t < SSM_GROUPS else c_ref
            dst[t % SSM_GROUPS] = a
    ext_ref[0:SUBLANES, :] = ext_ref[L:L + SUBLANES, :]

    lane = lax.broadcasted_iota(jnp.int32, (L, LANES), 1)
    row = lax.broadcasted_iota(jnp.int32, (L, LANES), 0)
    head_ok = lane < SSM_HEADS
    dt = jnp.where(head_ok, _softplus(dt_ref[...] + dtb_ref[...]), 0.0)
    da = dt * (-jnp.exp(alog_ref[...]))
    causal = row >= lane
    tril = jnp.where(causal, 1.0, 0.0).astype(BF16)
    cs = sum(jnp.dot(tril, p, preferred_element_type=F32) for p in _split3(da))
    hi, mid, lo = _split3(cs)
    p_ref[:, 0:LANES] = jnp.where(head_ok, hi.astype(F32), pltpu.roll(mid.astype(F32), SSM_HEADS, 1)).astype(BF16)
    p_ref[:, LANES:2 * LANES] = lo
    cst = cs.T
    cst_ref[...] = cst
    dtt = dt.T
    dtt_ref[...] = dtt
    wt_ref[...] = dtt * jnp.exp(jnp.broadcast_to(cst[:, L - 1:L], (LANES, L)) - cst)
    lane_lo = lane < SSM_HEAD_DIM

    def group(g, carry):
        bg = b_ref[g]
        bg_bf = bg.astype(BF16)
        bgt = bg.T
        cg = c_ref[g].astype(BF16)
        cb = _dot_nt(cg, bg_bf)
        yoff = jnp.dot(cg, statet_ref[g].astype(BF16), preferred_element_type=F32)
        colb = jnp.dot(p_ref[...], e_ref[g], preferred_element_type=F32)
        g8 = pl.multiple_of(g * HEADS_PER_GROUP, HEADS_PER_GROUP)
        cstg = cst_ref[pl.ds(g8, HEADS_PER_GROUP), :]
        dttg = dtt_ref[pl.ds(g8, HEADS_PER_GROUP), :]
        wtg = wt_ref[pl.ds(g8, HEADS_PER_GROUP), :]
        for pr in range(HEADS_PER_GROUP // 2):
            lanes = slice(pr * PAIR, (pr + 1) * PAIR)
            x2 = xs_ref[g, :, lanes]
            ms, ecols, ss = [], [], []
            for k in (2 * pr, 2 * pr + 1):
                col = colb[:, k * LANES:(k + 1) * LANES]
                rowv = jnp.broadcast_to(cstg[k:k + 1, :], (L, LANES))
                lmat = jnp.exp(jnp.where(causal, col - rowv, NEG_BIG))
                ms.append((cb * lmat * jnp.broadcast_to(dttg[k:k + 1, :], (L, LANES))).astype(BF16))
                ecols.append(jnp.exp(col))
                ss.append((bgt * jnp.broadcast_to(wtg[k:k + 1, :], (LANES, L))).astype(BF16))
            rhs = jnp.concatenate([jnp.where(lane_lo, x2, 0.0), jnp.where(lane_lo, 0.0, x2)], axis=0).astype(BF16)
            ydiag = jnp.dot(jnp.concatenate(ms, axis=1), rhs, preferred_element_type=F32)
            escale = jnp.where(lane_lo, ecols[0], ecols[1])
            y_ref[g, :, lanes] = ydiag + yoff[:, lanes] * escale
            snew = jnp.dot(jnp.concatenate(ss, axis=1), rhs, preferred_element_type=F32)
            decay = jnp.broadcast_to(escale[L - 1:L, :], (LANES, LANES))
            statet_ref[g, :, lanes] = statet_ref[g, :, lanes] * decay + snew
        return carry

    lax.fori_loop(0, SSM_GROUPS, group, 0, unroll=True)

    for g in range(SSM_GROUPS):
        cols = slice(g * GROUP_X, (g + 1) * GROUP_X)
        y = y_ref[g] + dexp_ref[:, cols] * xs_ref[g]
        gt = y * _silu(z_ref[:, cols])
        ms = jnp.mean(gt * gt, axis=-1, keepdims=True)
        g_ref[:, cols] = (gt * lax.rsqrt(ms + RMS_EPS) * nw_ref[:, cols]).astype(g_ref.dtype)

    @pl.when(c == pl.num_programs(0) - 1)
    def _():
        for g in range(SSM_GROUPS):
            for q in range(GROUP_X // LANES):
                blk = slice(q * LANES, (q + 1) * LANES)
                hfin_ref[g, blk, :] = statet_ref[g, :, blk].T


def ssd_prompt(z, xbc, dt_raw, conv_w, conv_b, dt_bias, a_log, d_exp, norm_w):
    s = z.shape[0]
    L = SSD_CHUNK
    vec = lambda n: pl.BlockSpec((1, n), lambda c: (0, 0))
    return pl.pallas_call(
        _ssd_prompt_body,
        grid=(s // L,),
        in_specs=[
            pl.BlockSpec((L, D_INNER), lambda c: (c, 0)),
            pl.BlockSpec((L, CONV_DIM), lambda c: (c, 0)),
            pl.BlockSpec((L, LANES), lambda c: (c, 0)),
            pl.BlockSpec((CONV_WIDTH, CONV_DIM), lambda c: (0, 0)),
            vec(CONV_DIM), vec(LANES), vec(LANES), vec(D_INNER), vec(D_INNER),
            pl.BlockSpec((SSM_GROUPS, 2 * LANES, HEADS_PER_GROUP * LANES), lambda c: (0, 0, 0)),
        ],
        out_specs=[
            pl.BlockSpec((L, D_INNER), lambda c: (c, 0)),
            pl.BlockSpec((SSM_GROUPS, GROUP_X, D_STATE), lambda c: (0, 0, 0)),
        ],
        out_shape=[
            jax.ShapeDtypeStruct((s, D_INNER), BF16),
            jax.ShapeDtypeStruct((SSM_GROUPS, GROUP_X, D_STATE), F32),
        ],
        scratch_shapes=[
            pltpu.VMEM((L + 2 * SUBLANES, CONV_DIM), F32),
            pltpu.VMEM((SSM_GROUPS, L, GROUP_X), F32),
            pltpu.VMEM((SSM_GROUPS, L, D_STATE), F32),
            pltpu.VMEM((SSM_GROUPS, L, D_STATE), F32),
            pltpu.VMEM((SSM_GROUPS, L, GROUP_X), F32),
            pltpu.VMEM((SSM_GROUPS, D_STATE, GROUP_X), F32),
            pltpu.VMEM((L, 2 * LANES), BF16),
            pltpu.VMEM((LANES, L), F32),
            pltpu.VMEM((LANES, L), F32), pltpu.VMEM((LANES, L), F32),
        ],
        compiler_params=_cparams(("arbitrary",)),
        name="ssd_prompt",
    )(z, xbc, dt_raw, conv_w, conv_b, dt_bias, a_log, d_exp, norm_w, _head_spread_matrix())


def _decode_conv_body(s0_ref, s1_ref, s2_ref, xn_ref, cw_ref, cb_ref, o_ref):
    acc = cb_ref[...] + cw_ref[0:1, :] * s0_ref[...] + cw_ref[1:2, :] * s1_ref[...]
    acc = acc + cw_ref[2:3, :] * s2_ref[...] + cw_ref[3:4, :] * xn_ref[...]
    o_ref[...] = _silu(acc)


def decode_conv(s0, s1, s2, xnew, conv_w, conv_b):
    b, n = xnew.shape
    blk = pl.BlockSpec((b, CONV_SLAB), lambda j: (0, j))
    return pl.pallas_call(
        _decode_conv_body,
        grid=(n // CONV_SLAB,),
        in_specs=[blk, blk, blk, blk, pl.BlockSpec((CONV_WIDTH, CONV_SLAB), lambda j: (0, j)),
                  pl.BlockSpec((1, CONV_SLAB), lambda j: (0, j))],
        out_specs=blk,
        out_shape=jax.ShapeDtypeStruct((b, n), F32),
        compiler_params=_cparams(("arbitrary",)),
        name="decode_conv",
    )(s0, s1, s2, xnew, conv_w, conv_b)


def _decode_prep_body(xs_ref, b_ref, c_ref, dt_ref, dtb_ref, alog_ref, e_ref, dtx_ref, daexp_ref, bcx_ref):
    lane = lax.broadcasted_iota(jnp.int32, dt_ref.shape, 1)
    dt = jnp.where(lane < SSM_HEADS, _softplus(dt_ref[...] + dtb_ref[...]), 0.0)
    da = jnp.where(lane < SSM_HEADS, jnp.exp(dt * (-jnp.exp(alog_ref[...]))), 0.0)
    e = e_ref[...]
    expand = lambda v: sum(jnp.dot(p, e, preferred_element_type=F32) for p in _split3(v))
    dtx = expand(dt) * xs_ref[...]
    dtx_ref[...] = dtx
    daexp_ref[...] = expand(da)
    bcx_ref[...] = dtx * jnp.sum(b_ref[...] * c_ref[...], axis=-1, keepdims=True)


def decode_prep(xbc_act, dt_raw, dt_bias, a_log, expand_mat):
    b = xbc_act.shape[0]
    xblk = pl.BlockSpec((b, GROUP_X), lambda g: (0, g))
    vec = pl.BlockSpec((1, LANES), lambda g: (0, 0))
    nb = D_INNER // D_STATE
    out = jax.ShapeDtypeStruct((b, D_INNER), F32)
    return pl.pallas_call(
        _decode_prep_body,
        grid=(SSM_GROUPS,),
        in_specs=[
            xblk,
            pl.BlockSpec((b, D_STATE), lambda g: (0, nb + g)),
            pl.BlockSpec((b, D_STATE), lambda g: (0, nb + SSM_GROUPS + g)),
            pl.BlockSpec((b, LANES), lambda g: (0, 0)),
            vec, vec,
            pl.BlockSpec((LANES, GROUP_X), lambda g: (0, g)),
        ],
        out_specs=[xblk, xblk, xblk],
        out_shape=[out, out, out],
        compiler_params=_cparams(("arbitrary",)),
        name="decode_prep",
    )(xbc_act, xbc_act, xbc_act, dt_raw, dt_bias, a_log, expand_mat)


TILE_ROWS = 128
SPLIT_ROWS = 16


def _decode_state_body(h_ref, dtx_ref, da_ref, b_ref, c_ref, ho_ref, hc_ref):
    n_tiles = (SSM_HEADS * SSM_HEAD_DIM) // TILE_ROWS
    rid = lambda w: lax.broadcasted_iota(jnp.int32, (SPLIT_ROWS, w), 0)
    pick = lambda r, pieces, w: functools.reduce(
        lambda acc, kv: jnp.where(r == kv[0], jnp.broadcast_to(kv[1].astype(F32), (SPLIT_ROWS, w)), acc),
        pieces, jnp.zeros((SPLIT_ROWS, w), F32))
    xh, xm, xl = _split3(dtx_ref[...])
    ah, am, al = _split3(da_ref[...])
    bh, bm, bl = _split3(b_ref[...])
    r = rid(D_INNER)
    lmat = pick(r, [(0, xh), (1, xh), (2, xh), (3, xm), (4, xm), (5, xl), (6, ah), (7, am), (8, al)], D_INNER)
    rb = rid(SSM_GROUPS * D_STATE)
    ru = pick(rb, [(0, bh), (1, bm), (2, bl), (3, bh), (4, bm), (5, bh)], SSM_GROUPS * D_STATE).astype(BF16)
    r1 = rid(D_STATE)
    rd = jnp.where(jnp.logical_and(r1 >= 6, r1 <= 8), 1.0, 0.0).astype(BF16)
    crow = c_ref[...]
    for i in range(n_tiles):
        g = (i * TILE_ROWS) // GROUP_X
        lanes = slice(i * TILE_ROWS, (i + 1) * TILE_ROWS)
        lt = lmat[:, lanes].T.astype(BF16)
        upd = jnp.dot(lt, ru[:, g * D_STATE:(g + 1) * D_STATE], preferred_element_type=F32)
        dec = jnp.dot(lt, rd, preferred_element_type=F32)
        h = h_ref[lanes, :]
        ho_ref[lanes, :] = h * dec + upd
        t = (h * crow[:, g * D_STATE:(g + 1) * D_STATE]).T
        hc_ref[0, :, lanes] = jnp.sum(t, axis=0, keepdims=True)


def decode_state(h, dtx, daexp, xbc_act3):
    b = h.shape[0]
    rows = SSM_HEADS * SSM_HEAD_DIM
    hblk = pl.BlockSpec((None, rows, D_STATE), lambda i: (i, 0, 0))
    rblk = pl.BlockSpec((None, 1, D_INNER), lambda i: (i, 0, 0))
    bc = SSM_GROUPS * D_STATE
    return pl.pallas_call(
        _decode_state_body,
        grid=(b,),
        in_specs=[
            hblk, rblk, rblk,
            pl.BlockSpec((None, 1, bc), lambda i: (i, 0, D_INNER // bc)),
            pl.BlockSpec((None, 1, bc), lambda i: (i, 0, D_INNER // bc + 1)),
        ],
        out_specs=[hblk, pl.BlockSpec((1, 1, D_INNER), lambda i: (i, 0, 0))],
        out_shape=[jax.ShapeDtypeStruct((b, rows, D_STATE), F32), jax.ShapeDtypeStruct((b, 1, D_INNER), F32)],
        compiler_params=_cparams(("arbitrary",)),
        name="decode_state",
    )(h, dtx, daexp, xbc_act3, xbc_act3)


def _decode_post_body(hc_ref, da_ref, bcx_ref, xs_ref, z_ref, dexp_ref, nw_ref, o_ref):
    y = da_ref[...] * hc_ref[...] + bcx_ref[...] + dexp_ref[...] * xs_ref[...]
    gt = y * _silu(z_ref[...])
    ms = jnp.mean(gt * gt, axis=-1, keepdims=True)
    o_ref[...] = (gt * lax.rsqrt(ms + RMS_EPS) * nw_ref[...]).astype(o_ref.dtype)


def decode_post(hc, daexp, bcx, xbc_act, z, d_exp, norm_w):
    b = hc.shape[0]
    xblk = pl.BlockSpec((b, GROUP_X), lambda g: (0, g))
    vblk = pl.BlockSpec((1, GROUP_X), lambda g: (0, g))
    return pl.pallas_call(
        _decode_post_body,
        grid=(SSM_GROUPS,),
        in_specs=[xblk, xblk, xblk, xblk, xblk, vblk, vblk],
        out_specs=xblk,
        out_shape=jax.ShapeDtypeStruct((b, D_INNER), BF16),
        compiler_params=_cparams(("arbitrary",)),
        name="decode_post",
    )(hc, daexp, bcx, xbc_act, z, d_exp, norm_w)


SUPER = WINDOW_MAX
ATT_N = 128
MERGE_ROWS = 256


def _rel_bucket(dist):
    max_exact = NUM_BUCKETS // 2
    df = jnp.maximum(dist, max_exact).astype(F32)
    large = max_exact + (jnp.log(df / max_exact) / math.log(MAX_DISTANCE / max_exact)
                         * (NUM_BUCKETS - max_exact)).astype(jnp.int32)
    return jnp.where(dist < max_exact, dist, jnp.minimum(large, NUM_BUCKETS - 1))


def _prompt_bias_rows(rel_bias):
    n = ATT_N
    back = n - jnp.arange(2 * n)
    rows = []
    for g, (_, dil) in enumerate(DILATION_GROUPS):
        tab = rel_bias[:, g * N_HEADS_B:(g + 1) * N_HEADS_B]
        bias = tab[_rel_bucket(jnp.clip(back, 0, n) * dil)].astype(F32)
        rows.append(jnp.where((back >= 0)[:, None], bias, NEG_BIG).T)
    return jnp.stack(rows, axis=0)[:, :, None, :]


def _rows(start, size, stride):
    return pl.ds(start, size) if stride == 1 else pl.ds(start, size, stride=stride)


def _attn_prompt_body(q0_ref, q1_ref, q2_ref, kc_ref, kp_ref, vc_ref, vp_ref, t_ref, o_ref, og_ref, lse_ref):
    first = pl.program_id(0) == 0
    n = ATT_N
    scale = HEAD_DIM ** -0.5
    key_id = lax.broadcasted_iota(jnp.int32, (n, 2 * n), 1)
    q_refs = (q0_ref, q1_ref, q2_ref)
    for g, (win, dil) in enumerate(DILATION_GROUPS):
        nblk = SUPER // (n * dil)
        bias = pltpu.roll(jnp.broadcast_to(t_ref[g, 0], (n, 2 * n)), 0, 1, stride=1, stride_axis=0)
        for r in range(dil):
            prev = _rows((nblk - 1) * n * dil + r, n, dil)
            kprev, vprev = kp_ref[prev, :].astype(BF16), vp_ref[prev, :].astype(BF16)
            for c in range(nblk):
                start = c * n * dil + r
                cur = _rows(start, n, dil)
                qb = q_refs[g][cur, :].astype(BF16)
                kcur, vcur = kc_ref[cur, :].astype(BF16), vc_ref[cur, :].astype(BF16)
                keys = jnp.concatenate([kprev, kcur], axis=0)
                vals = jnp.concatenate([vprev, vcur], axis=0)
                kprev, vprev = kcur, vcur
                s = _dot_nt(qb, keys) * scale + bias
                if c == 0:
                    s = jnp.where(jnp.logical_and(first, key_id < n), NEG_BIG, s)
                m = jnp.max(s, axis=-1, keepdims=True)
                p = jnp.exp(s - m)
                den = jnp.sum(p, axis=-1, keepdims=True)
                o = jnp.dot(p.astype(BF16), vals, preferred_element_type=F32) * (1.0 / den)
                og_ref[g, cur, :] = o
                lse_ref[g, cur, :] = jnp.broadcast_to(m + jnp.log(den), (n, LANES))
    for i in range(SUPER // MERGE_ROWS):
        rows = slice(i * MERGE_ROWS, (i + 1) * MERGE_ROWS)
        ls = [lse_ref[g, rows, :] for g in range(N_GROUPS_B)]
        mx = functools.reduce(jnp.maximum, ls)
        es = [jnp.exp(l - mx) for l in ls]
        tot = functools.reduce(lambda a, b: a + b, es)
        acc = functools.reduce(lambda a, b: a + b, [es[g] * og_ref[g, rows, :] for g in range(N_GROUPS_B)])
        o_ref[rows, :] = (acc * (1.0 / tot)).astype(o_ref.dtype)


def attn_prompt(q, kv, bias_rows):
    s = q.shape[0]
    blk = lambda f: pl.BlockSpec((SUPER, HEAD_DIM), f)
    return pl.pallas_call(
        _attn_prompt_body,
        grid=(s // SUPER, N_HEADS_B),
        in_specs=[
            blk(lambda c, h: (c, h)),
            blk(lambda c, h: (c, N_HEADS_B + h)),
            blk(lambda c, h: (c, 2 * N_HEADS_B + h)),
            blk(lambda c, h: (c, h)),
            blk(lambda c, h: (jnp.maximum(c - 1, 0), h)),
            blk(lambda c, h: (c, N_HEADS_B + h)),
            blk(lambda c, h: (jnp.maximum(c - 1, 0), N_HEADS_B + h)),
            pl.BlockSpec((N_GROUPS_B, 1, 1, 2 * ATT_N), lambda c, h: (0, h, 0, 0)),
        ],
        out_specs=blk(lambda c, h: (c, h)),
        out_shape=jax.ShapeDtypeStruct((s, KV_DIM), BF16),
        scratch_shapes=[
            pltpu.VMEM((N_GROUPS_B, SUPER, HEAD_DIM), F32),
            pltpu.VMEM((N_GROUPS_B, SUPER, LANES), F32),
        ],
        compiler_params=_cparams(("arbitrary", "arbitrary")),
        name="attn_prompt",
    )(q, q, q, kv, kv, kv, kv, bias_rows)


N_BUF = min(WINDOW_MAX, PAST_LEN)
ROW_MAJOR = 16
ROW_MID = 4


def _decode_bias_tables(rel_bias):
    n = ATT_N
    j = n - jnp.arange(n)
    cached, new = [], []
    for g, (_, dil) in enumerate(DILATION_GROUPS):
        tab = rel_bias[:, g * N_HEADS_B:(g + 1) * N_HEADS_B]
        cached.append(tab[_rel_bucket(j * dil)])
        new.append(tab[_rel_bucket(jnp.zeros((), jnp.int32))])
    cached = jnp.broadcast_to(jnp.stack(cached)[..., None], (N_GROUPS_B, n, N_HEADS_B, LANES)).astype(F32)
    new = jnp.broadcast_to(jnp.stack(new)[..., None], (N_GROUPS_B, N_HEADS_B, LANES)).astype(F32)
    return cached, new


def _attn_decode_body(q_ref, kvn_ref, k0_ref, v0_ref, k1_ref, v1_ref, k2_ref, v2_ref, bc_ref, bn_ref, o_ref):
    n = ATT_N
    scale = HEAD_DIM ** -0.5
    kn, vn = kvn_ref[0], kvn_ref[1]
    kv_refs = ((k0_ref, v0_ref), (k1_ref, v1_ref), (k2_ref, v2_ref))
    outs, lses = [], []
    for g in range(N_GROUPS_B):
        q = q_ref[g]
        k = kv_refs[g][0][...].reshape(n, N_HEADS_B, HEAD_DIM)
        v = kv_refs[g][1][...].reshape(n, N_HEADS_B, HEAD_DIM)
        s = jnp.sum(k * q[None], axis=-1, keepdims=True) * scale + bc_ref[g]
        s0 = jnp.sum(kn * q, axis=-1, keepdims=True) * scale + bn_ref[g]
        m = jnp.maximum(jnp.max(s, axis=0), s0)
        p = jnp.exp(s - m[None])
        p0 = jnp.exp(s0 - m)
        den = jnp.sum(p, axis=0) + p0
        outs.append((jnp.sum(p * v, axis=0) + p0 * vn) * (1.0 / den))
        lses.append(m + jnp.log(den))
    mx = functools.reduce(jnp.maximum, lses)
    es = [jnp.exp(l - mx) for l in lses]
    tot = functools.reduce(lambda a, b: a + b, es)
    acc = functools.reduce(lambda a, b: a + b, [e * o for e, o in zip(es, outs)])
    o_ref[...] = (acc * (1.0 / tot)).astype(o_ref.dtype)


def attn_decode(q, kv_new, cache_k, cache_v, bias_cached, bias_new):
    b = q.shape[0]
    view = (b, N_BUF // ROW_MAJOR, ROW_MAJOR // ROW_MID, ROW_MID, N_HEADS_B, HEAD_DIM)
    ck, cv = cache_k.reshape(view), cache_v.reshape(view)
    n = ATT_N
    hd = (N_HEADS_B, HEAD_DIM)
    g0 = pl.BlockSpec((None, n // ROW_MAJOR, ROW_MAJOR // ROW_MID, ROW_MID) + hd,
                      lambda i: (i, N_BUF // n - 1, 0, 0, 0, 0))
    g1 = pl.BlockSpec((None, n * ROW_MID // ROW_MAJOR, ROW_MAJOR // ROW_MID, None) + hd,
                      lambda i: (i, N_BUF // (n * ROW_MID) - 1, 0, 0, 0, 0))
    g2 = pl.BlockSpec((None, n, None, None) + hd, lambda i: (i, 0, 0, 0, 0, 0))
    return pl.pallas_call(
        _attn_decode_body,
        grid=(b,),
        in_specs=[
            pl.BlockSpec((None, N_GROUPS_B) + hd, lambda i: (i, 0, 0, 0)),
            pl.BlockSpec((None, 2) + hd, lambda i: (i, 0, 0, 0)),
            g0, g0, g1, g1, g2, g2,
            pl.BlockSpec((N_GROUPS_B, n) + (N_HEADS_B, LANES), lambda i: (0, 0, 0, 0)),
            pl.BlockSpec((N_GROUPS_B, N_HEADS_B, LANES), lambda i: (0, 0, 0)),
        ],
        out_specs=pl.BlockSpec((None,) + hd, lambda i: (i, 0, 0)),
        out_shape=jax.ShapeDtypeStruct((b,) + hd, BF16),
        compiler_params=_cparams(("arbitrary",)),
        name="attn_decode",
    )(q, kv_new, ck, cv, ck, cv, ck, cv, bias_cached, bias_new)


ADA_ROWS = 136
Z_TILE = 1024


def _pad_lanes(v):
    return jnp.pad(v.astype(F32), (0, LANES - v.shape[0]))[None]


TM = 1024
TM_LONG_K = 512
TM_EW = 256


def _trunk(xs, adas, mixers_a, mixers_b, p):
    tms = (TM_EW, xs[1].shape[0])
    vec = lambda a, l, s: a[l, s][None]
    sh1, sc1, g1, sh2, sc2, g2 = zip(*(a[0] for a in adas))
    sh1b, sc1b, g1b, sh2b, sc2b, g2b = zip(*(a[1] for a in adas))
    shkv, sckv = zip(*(a[2] for a in adas))
    paths = range(2)

    def ln(x, y, gate, l, s, mods):
        outs = [post_ln(x[i], y[i], gate[i], vec(p["ln_g"], l, s), vec(p["ln_b"], l, s),
                        [(sc[i], sh[i]) for sc, sh in mods], tm=tms[i]) for i in paths]
        return zip(*outs)

    def mm(a, w, **kw):
        tm = kw.pop("tm", TM)
        return matmul(a[0], w, rider=a[1], tm=tm, **kw)

    u = [modulate(xs[i], sc1[i], sh1[i], tm=tms[i]) for i in paths]
    w_in_t = jnp.swapaxes(p["ssm_w_in"], 1, 2).reshape(-1, D_MODEL)
    z = mm(u, w_in_t, w_transposed=True, tn=Z_TILE, n_out=D_INNER, name="in_z")
    xbc = mm(u, w_in_t, w_transposed=True, tn=Z_TILE, n_out=CONV_DIM, col_off=D_INNER // Z_TILE, name="in_xbc")
    dt = mm(u, w_in_t, w_transposed=True, tn=LANES, n_out=LANES, col_off=(D_INNER + CONV_DIM) // LANES, name="in_dt")
    gated, extras = zip(*(mixers_a[i](z[i], xbc[i], dt[i]) for i in paths))
    y = mm(gated, p["ssm_w_out"].reshape(D_INNER, D_MODEL), tn=512, name="ssm_out")
    x, u = ln(xs, y, g1, 0, 0, [(sc2, sh2)])
    h = glu_matmul(u[0], u[1], p["ffn_w_gate"], p["ffn_w_up"], layer=0, tm=TM, tn=512, name="ffn0_glu")
    y = mm(h, p["ffn_w_down"], layer=0, tm=TM_LONG_K, tn=512, name="ffn0_down")
    x, u, ukv = ln(x, y, g2, 0, 1, [(sc1b, sh1b), (sckv, shkv)])

    kv = mm(ukv, p["w_kv"], tn=1024, name="kv_proj")
    q = mm(u, p["attn_w_q"], layer=0, tn=1024, name="q_proj")
    o = [mixers_b[i](q[i], kv[i]) for i in paths]
    y = mm(o, p["attn_w_o"], layer=0, tn=1024, name="attn_out")
    x, u = ln(x, y, g1b, 1, 0, [(sc2b, sh2b)])
    h = glu_matmul(u[0], u[1], p["ffn_w_gate"], p["ffn_w_up"], layer=1, tm=TM, tn=512, name="ffn1_glu")
    y = mm(h, p["ffn_w_down"], layer=1, tm=TM_LONG_K, tn=512, name="ffn1_down")
    (x,) = ln(x, y, g2b, 1, 1, [])
    return [(x[i], extras[i], kv[i]) for i in paths]


def kernel(x_prompt, x_sample, state_conv, state_ssm, cache_k, cache_v, c_prompt, c_sample, w_ada, b_ada, ln_g, ln_b, ffn_w_gate, ffn_w_up, ffn_w_down, ssm_w_in, ssm_conv_w, ssm_conv_b, ssm_dt_bias, ssm_a_log, ssm_d, ssm_norm_w, ssm_w_out, kv_w_ada, kv_b_ada, w_kv, attn_w_q, attn_w_o, rel_bias):
    p = dict(ln_g=ln_g, ln_b=ln_b, ffn_w_gate=ffn_w_gate, ffn_w_up=ffn_w_up, ffn_w_down=ffn_w_down,
             ssm_w_in=ssm_w_in, ssm_w_out=ssm_w_out, w_kv=w_kv, attn_w_q=attn_w_q, attn_w_o=attn_w_o)
    s, b, d = SEQ, DEC_BATCH, D_MODEL

    c_all = jnp.concatenate([c_sample, c_prompt, jnp.zeros((ADA_ROWS - b - 1, d), F32)], axis=0)
    ada = [matmul(c_all, w_ada, layer=l, tm=ADA_ROWS, tn=1024, bias=b_ada[l][None], act="silu", name=f"ada{l}")
           for l in range(DEPTH)]
    adakv = matmul(c_all, kv_w_ada, tm=ADA_ROWS, tn=1024, bias=kv_b_ada[None], act="silu", name="ada_kv")
    split = lambda m, n, rows: [m[rows, i * d:(i + 1) * d] for i in range(n)]
    smp, prm = slice(0, b), slice(b, b + 1)

    conv_w, conv_b = ssm_conv_w[0], ssm_conv_b[0][None]
    dt_bias, a_log = _pad_lanes(ssm_dt_bias[0]), _pad_lanes(ssm_a_log[0])
    d_exp = jnp.repeat(ssm_d[0].astype(F32), SSM_HEAD_DIM)[None]
    norm_w = ssm_norm_w[0][None]

    def mixer_a_prompt(z, xbc, dt):
        gated, hfin = ssd_prompt(z, xbc, dt, conv_w, conv_b, dt_bias, a_log, d_exp, norm_w)
        return gated, (xbc[s - (CONV_WIDTH - 1):], hfin)

    bias_rows = _prompt_bias_rows(rel_bias)
    mixer_b_prompt = lambda q, kv: attn_prompt(q, kv, bias_rows)

    expand_mat = (lax.broadcasted_iota(jnp.int32, (LANES, D_INNER), 0)
                  == lax.broadcasted_iota(jnp.int32, (LANES, D_INNER), 1) // SSM_HEAD_DIM).astype(BF16)

    def mixer_a_decode(z, xbc, dt):
        st = state_conv[0]
        xa = decode_conv(st[:, 0], st[:, 1], st[:, 2], xbc, conv_w, conv_b)
        dtx, daexp, bcx = decode_prep(xa, dt, dt_bias, a_log, expand_mat)
        hnew, hc = decode_state(state_ssm.reshape(b, SSM_HEADS * SSM_HEAD_DIM, D_STATE),
                                dtx[:, None], daexp[:, None], xa[:, None])
        gated = decode_post(hc[:, 0], daexp, bcx, xa, z, d_exp, norm_w)
        new_conv = jnp.concatenate([st[:, 1:], xbc[:, None]], axis=1)
        return gated, (new_conv, hnew)

    bias_cached, bias_new = _decode_bias_tables(rel_bias)

    def mixer_b_decode(q, kv):
        o = attn_decode(q.reshape(b, N_GROUPS_B, N_HEADS_B, HEAD_DIM), kv.reshape(b, 2, N_HEADS_B, HEAD_DIM),
                        cache_k, cache_v, bias_cached, bias_new)
        return o.reshape(b, KV_DIM)

    adas = [(split(ada[0], 6, rows), split(ada[1], 6, rows), split(adakv, 2, rows)) for rows in (prm, smp)]
    (y_p, (conv_p, ssm_p), kv_p), (y_s, (conv_s, ssm_s), kv_s) = _trunk(
        (x_prompt.reshape(s, d), x_sample.reshape(b, d)), adas,
        (mixer_a_prompt, mixer_a_decode), (mixer_b_prompt, mixer_b_decode), p)

    n_keep = min(WINDOW_MAX, s)
    heads = (N_HEADS_B, HEAD_DIM)
    return (
        y_p.reshape(1, s, d),
        y_s.reshape(b, 1, d),
        conv_p.reshape(1, 1, CONV_WIDTH - 1, CONV_DIM),
        ssm_p.reshape(1, 1, SSM_HEADS, SSM_HEAD_DIM, D_STATE),
        kv_p[s - n_keep:, :KV_DIM].reshape((1, n_keep) + heads),
        kv_p[s - n_keep:, KV_DIM:].reshape((1, n_keep) + heads),
        conv_s.reshape(1, b, CONV_WIDTH - 1, CONV_DIM),
        ssm_s.reshape(1, b, SSM_HEADS, SSM_HEAD_DIM, D_STATE),
        kv_s[:, :KV_DIM].reshape((b, 1) + heads),
        kv_s[:, KV_DIM:].reshape((b, 1) + heads),
    )
```

```python
import functools
import math

import jax
import jax.numpy as jnp
from jax import lax
from jax.experimental import pallas as pl
from jax.experimental.pallas import tpu as pltpu

F32 = jnp.float32
BF16 = jnp.bfloat16

D_MODEL = 2048
SEQ = 8192
DEPTH = 2
DEC_BATCH = 128
PAST_LEN = 2048
D_INNER = 2 * D_MODEL
SSM_HEAD_DIM = 64
SSM_HEADS = D_INNER // SSM_HEAD_DIM
SSM_GROUPS = 8
HEADS_PER_GROUP = SSM_HEADS // SSM_GROUPS
D_STATE = 128
CONV_WIDTH = 4
CONV_DIM = D_INNER + 2 * SSM_GROUPS * D_STATE
SSD_CHUNK = 128
RMS_EPS = 1e-5
HEAD_DIM = 128
N_HEADS_B = D_MODEL // HEAD_DIM
DILATION_GROUPS = ((128, 1), (512, 4), (2048, 16))
N_GROUPS_B = len(DILATION_GROUPS)
WINDOW_MAX = max(w for w, _ in DILATION_GROUPS)
Q_DIM = N_GROUPS_B * N_HEADS_B * HEAD_DIM
KV_DIM = N_HEADS_B * HEAD_DIM
NUM_BUCKETS = 32
MAX_DISTANCE = WINDOW_MAX
D_FF = -(-8 * D_MODEL // (3 * 256)) * 256
ALPHA = (2 * DEPTH) ** 0.25
LN_EPS = 1e-5

LANES = 128
SUBLANES = 8
VMEM_LIMIT_BYTES = 56 * 1024 * 1024
NEG_BIG = -1e30
LOG2E = math.log2(math.e)


def _cparams(semantics):
    return pltpu.CompilerParams(dimension_semantics=semantics, vmem_limit_bytes=VMEM_LIMIT_BYTES)


def _silu(x):
    return x * (1.0 / (1.0 + jnp.exp(-x)))


def _split3(x):
    hi = x.astype(BF16)
    r1 = x - hi.astype(F32)
    mid = r1.astype(BF16)
    lo = (r1 - mid.astype(F32)).astype(BF16)
    return hi, mid, lo


CAST_ROWS = 256


def _cast_tile(w_ref, wbf_ref, transposed=False):
    k, tn = wbf_ref.shape
    if transposed:
        step = min(CAST_ROWS, tn)
        for c in range(tn // step):
            wbf_ref[:, c * step:(c + 1) * step] = w_ref[c * step:(c + 1) * step, :].T.astype(BF16)
        return
    rows = CAST_ROWS if k % CAST_ROWS == 0 else k
    def body(r, c):
        off = pl.multiple_of(r * rows, rows)
        wbf_ref[pl.ds(off, rows), :] = w_ref[pl.ds(off, rows), :].astype(BF16)
        return c
    lax.fori_loop(0, k // rows, body, 0)


def _mm_body(*refs, act, out_act, has_bias, has_rider, w_transposed):
    refs = list(refs)
    a_ref, w_ref = refs.pop(0), refs.pop(0)
    b_ref = refs.pop(0) if has_bias else None
    a2_ref = refs.pop(0) if has_rider else None
    o_ref = refs.pop(0)
    o2_ref = refs.pop(0) if has_rider else None
    wbf_ref = refs.pop(0)

    def product(lhs_ref):
        a = lhs_ref[...]
        if act == "silu":
            a = _silu(a.astype(F32))
        acc = jnp.dot(a.astype(BF16), wbf_ref[...], preferred_element_type=F32)
        if has_bias:
            acc = acc + b_ref[...]
        return _silu(acc) if out_act == "silu" else acc

    @pl.when(pl.program_id(1) == 0)
    def _():
        _cast_tile(w_ref, wbf_ref, w_transposed)
        if has_rider:
            o2_ref[...] = product(a2_ref).astype(o2_ref.dtype)

    o_ref[...] = product(a_ref).astype(o_ref.dtype)


def _weight_spec(w, layer, tn, col_off=0, transposed=False):
    if transposed:
        return pl.BlockSpec((tn, w.shape[1]), lambda j, i: (j + col_off, 0))
    k = w.shape[-2]
    if w.ndim == 2:
        return pl.BlockSpec((k, tn), lambda j, i: (0, j + col_off))
    return pl.BlockSpec((None, k, tn), lambda j, i: (layer, 0, j + col_off))


def matmul(a, w, *, tm, tn, layer=0, n_out=None, col_off=0, bias=None, act=None, out_act=None, rider=None,
           w_transposed=False, out_dtype=F32, name="mm"):
    m, k = a.shape
    n_out = (w.shape[0] if w_transposed else w.shape[-1]) if n_out is None else n_out
    grid = (pl.cdiv(n_out, tn), m // tm)
    in_specs = [
        pl.BlockSpec((tm, k), lambda j, i: (i, 0)),
        _weight_spec(w, layer, tn, col_off, w_transposed),
    ]
    args = [a, w]
    out_specs = [pl.BlockSpec((tm, tn), lambda j, i: (i, j))]
    out_shape = [jax.ShapeDtypeStruct((m, n_out), out_dtype)]
    if bias is not None:
        in_specs.append(pl.BlockSpec((1, tn), lambda j, i: (0, j + col_off)))
        args.append(bias)
    if rider is not None:
        m2 = rider.shape[0]
        in_specs.append(pl.BlockSpec((m2, k), lambda j, i: (0, 0)))
        args.append(rider)
        out_specs.append(pl.BlockSpec((m2, tn), lambda j, i: (0, j)))
        out_shape.append(jax.ShapeDtypeStruct((m2, n_out), out_dtype))
    outs = pl.pallas_call(
        functools.partial(_mm_body, act=act, out_act=out_act, has_bias=bias is not None,
                          has_rider=rider is not None, w_transposed=w_transposed),
        grid=grid,
        in_specs=in_specs,
        out_specs=out_specs,
        out_shape=out_shape,
        scratch_shapes=[pltpu.VMEM((k, tn), BF16)],
        compiler_params=_cparams(("arbitrary", "arbitrary")),
        name=name,
    )(*args)
    return outs if rider is not None else outs[0]


def _glu_body(a_ref, wg_ref, wu_ref, a2_ref, o_ref, o2_ref, wgbf_ref, wubf_ref):
    def product(lhs_ref):
        a = lhs_ref[...]
        g = jnp.dot(a, wgbf_ref[...], preferred_element_type=F32)
        u = jnp.dot(a, wubf_ref[...], preferred_element_type=F32)
        return (_silu(g) * u).astype(BF16)

    @pl.when(pl.program_id(1) == 0)
    def _():
        _cast_tile(wg_ref, wgbf_ref)
        _cast_tile(wu_ref, wubf_ref)
        o2_ref[...] = product(a2_ref)

    o_ref[...] = product(a_ref)


def glu_matmul(a, rider, w_gate, w_up, *, tm, tn, layer=0, name="glu"):
    m, k = a.shape
    m2 = rider.shape[0]
    n = w_gate.shape[-1]
    return pl.pallas_call(
        _glu_body,
        grid=(n // tn, m // tm),
        in_specs=[
            pl.BlockSpec((tm, k), lambda j, i: (i, 0)),
            _weight_spec(w_gate, layer, tn),
            _weight_spec(w_up, layer, tn),
            pl.BlockSpec((m2, k), lambda j, i: (0, 0)),
        ],
        out_specs=[pl.BlockSpec((tm, tn), lambda j, i: (i, j)), pl.BlockSpec((m2, tn), lambda j, i: (0, j))],
        out_shape=[jax.ShapeDtypeStruct((m, n), BF16), jax.ShapeDtypeStruct((m2, n), BF16)],
        scratch_shapes=[pltpu.VMEM((k, tn), BF16), pltpu.VMEM((k, tn), BF16)],
        compiler_params=_cparams(("arbitrary", "arbitrary")),
        name=name,
    )(a, w_gate, w_up, rider)


def _row_spec(arr, tm):
    if arr.shape[0] == 1:
        return pl.BlockSpec((1, arr.shape[1]), lambda i: (0, 0))
    return pl.BlockSpec((tm, arr.shape[1]), lambda i: (i, 0))


def _modulate_body(x_ref, sc_ref, sh_ref, o_ref):
    o_ref[...] = (x_ref[...] * (1.0 + sc_ref[...]) + sh_ref[...]).astype(o_ref.dtype)


def modulate(x, scale, shift, *, tm):
    m, d = x.shape
    return pl.pallas_call(
        _modulate_body,
        grid=(m // tm,),
        in_specs=[pl.BlockSpec((tm, d), lambda i: (i, 0)), _row_spec(scale, tm), _row_spec(shift, tm)],
        out_specs=pl.BlockSpec((tm, d), lambda i: (i, 0)),
        out_shape=jax.ShapeDtypeStruct((m, d), BF16),
        compiler_params=_cparams(("arbitrary",)),
        name="modulate",
    )(x, scale, shift)


def _ln_body(*refs, n_mod):
    x_ref, y_ref, gate_ref, g_ref, b_ref = refs[:5]
    mod_refs = refs[5:5 + 2 * n_mod]
    xo_ref = refs[5 + 2 * n_mod]
    u_refs = refs[6 + 2 * n_mod:]
    t = ALPHA * x_ref[...] + (1.0 + gate_ref[...]) * y_ref[...].astype(F32)
    mu = jnp.mean(t, axis=-1, keepdims=True)
    tc = t - mu
    var = jnp.mean(tc * tc, axis=-1, keepdims=True)
    xn = tc * lax.rsqrt(var + LN_EPS) * g_ref[...] + b_ref[...]
    xo_ref[...] = xn
    for q in range(n_mod):
        u_refs[q][...] = (xn * (1.0 + mod_refs[2 * q][...]) + mod_refs[2 * q + 1][...]).astype(BF16)


def post_ln(x, y, gate, ln_g, ln_b, mods, *, tm):
    m, d = x.shape
    row = pl.BlockSpec((tm, d), lambda i: (i, 0))
    vec = pl.BlockSpec((1, d), lambda i: (0, 0))
    in_specs = [row, row, _row_spec(gate, tm), vec, vec]
    args = [x, y, gate, ln_g, ln_b]
    for sc, sh in mods:
        in_specs += [_row_spec(sc, tm), _row_spec(sh, tm)]
        args += [sc, sh]
    outs = pl.pallas_call(
        functools.partial(_ln_body, n_mod=len(mods)),
        grid=(m // tm,),
        in_specs=in_specs,
        out_specs=[row] * (1 + len(mods)),
        out_shape=[jax.ShapeDtypeStruct((m, d), F32)] + [jax.ShapeDtypeStruct((m, d), BF16)] * len(mods),
        compiler_params=_cparams(("arbitrary",)),
        name="post_ln",
    )(*args)
    return outs


CONV_SLAB = 512
GROUP_X = D_INNER // SSM_GROUPS
PAIR = 2 * SSM_HEAD_DIM


def _softplus(x):
    return jnp.maximum(x, 0.0) + jnp.log1p(jnp.exp(-jnp.abs(x)))


def _dot_nt(a, b):
    return lax.dot_general(a, b, (((1,), (1,)), ((), ())), preferred_element_type=F32)


def _dot_tn(a, b):
    return lax.dot_general(a, b, (((0,), (0,)), ((), ())), preferred_element_type=F32)


def _head_spread_matrix():
    r = lax.broadcasted_iota(jnp.int32, (SSM_GROUPS, 2 * LANES, HEADS_PER_GROUP * LANES), 1)
    col = lax.broadcasted_iota(jnp.int32, (SSM_GROUPS, 2 * LANES, HEADS_PER_GROUP * LANES), 2)
    g = lax.broadcasted_iota(jnp.int32, (SSM_GROUPS, 2 * LANES, HEADS_PER_GROUP * LANES), 0)
    hit = jnp.logical_and(r % SSM_HEADS == g * HEADS_PER_GROUP + col // LANES, r < 3 * SSM_HEADS)
    return hit.astype(BF16)


def _ssd_prompt_body(sz_ref, xbc_ref, dt_ref, cw_ref, cb_ref, dtb_ref, alog_ref, dexp_ref, nw_ref, e_ref,
                     g_ref, hfin_ref,
                     ext_ref, act_ref, statet_ref, p_ref, wt_ref, cst_ref, dtt_ref):
    c = pl.program_id(0)
    L = SSD_CHUNK

    @pl.when(c == 0)
    def _():
        ext_ref[0:SUBLANES, :] = jnp.zeros((SUBLANES, CONV_DIM), F32)
        statet_ref[...] = jnp.zeros(statet_ref.shape, F32)

    ext_ref[SUBLANES:SUBLANES + L, :] = xbc_ref[...]
    for s in range(CONV_DIM // LANES):
        cols = slice(s * LANES, (s + 1) * LANES)
        acc = cb_ref[:, cols]
        for i in range(CONV_WIDTH):
            lo = SUBLANES - (CONV_WIDTH - 1) + i
            acc = acc + cw_ref[i:i + 1, cols] * ext_ref[lo:lo + L, cols]
        act_ref[:, cols] = _silu(acc)
    ext_ref[0:SUBLANES, :] = ext_ref[L:L + SUBLANES, :]

    lane = lax.broadcasted_iota(jnp.int32, (L, LANES), 1)
    row = lax.broadcasted_iota(jnp.int32, (L, LANES), 0)
    head_ok = lane < SSM_HEADS
    dt = jnp.where(head_ok, _softplus(dt_ref[...] + dtb_ref[...]), 0.0)
    da = dt * (-jnp.exp(alog_ref[...]))
    causal = row >= lane
    tril = jnp.where(causal, 1.0, 0.0).astype(BF16)
    cs = sum(jnp.dot(tril, p, preferred_element_type=F32) for p in _split3(da))
    hi, mid, lo = _split3(cs)
    p_ref[:, 0:LANES] = jnp.where(head_ok, hi.astype(F32), pltpu.roll(mid.astype(F32), SSM_HEADS, 1)).astype(BF16)
    p_ref[:, LANES:2 * LANES] = lo
    cst = cs.T
    cst_ref[...] = cst
    dtt = dt.T
    dtt_ref[...] = dtt
    wt_ref[...] = dtt * jnp.exp(jnp.broadcast_to(cst[:, L - 1:L], (LANES, L)) - cst)
    lane_lo = lane < SSM_HEAD_DIM

    b_off, c_off = D_INNER, D_INNER + SSM_GROUPS * D_STATE
    for g in range(SSM_GROUPS):
        bg = act_ref[:, b_off + g * D_STATE:b_off + (g + 1) * D_STATE]
        bg_bf = bg.astype(BF16)
        bgt = bg.T
        cg = act_ref[:, c_off + g * D_STATE:c_off + (g + 1) * D_STATE].astype(BF16)
        cb = _dot_nt(cg, bg_bf)
        yoff = jnp.dot(cg, statet_ref[g].astype(BF16), preferred_element_type=F32)
        colb = jnp.dot(p_ref[...], e_ref[g], preferred_element_type=F32)
        heads = slice(g * HEADS_PER_GROUP, (g + 1) * HEADS_PER_GROUP)
        cstg, dttg, wtg = cst_ref[heads, :], dtt_ref[heads, :], wt_ref[heads, :]
        gated = []
        for pr in range(HEADS_PER_GROUP // 2):
            lanes = slice(pr * PAIR, (pr + 1) * PAIR)
            cols = slice(g * GROUP_X + pr * PAIR, g * GROUP_X + (pr + 1) * PAIR)
            x2 = act_ref[:, cols]
            ms, ecols, ss = [], [], []
            for k in (2 * pr, 2 * pr + 1):
                col = colb[:, k * LANES:(k + 1) * LANES]
                rowv = jnp.broadcast_to(cstg[k:k + 1, :], (L, LANES))
                lmat = jnp.exp(jnp.where(causal, col - rowv, NEG_BIG))
                ms.append((cb * lmat * jnp.broadcast_to(dttg[k:k + 1, :], (L, LANES))).astype(BF16))
                ecols.append(jnp.exp(col))
                ss.append((bgt * jnp.broadcast_to(wtg[k:k + 1, :], (LANES, L))).astype(BF16))
            rhs = jnp.concatenate([jnp.where(lane_lo, x2, 0.0), jnp.where(lane_lo, 0.0, x2)], axis=0).astype(BF16)
            ydiag = jnp.dot(jnp.concatenate(ms, axis=1), rhs, preferred_element_type=F32)
            escale = jnp.where(lane_lo, ecols[0], ecols[1])
            y = ydiag + yoff[:, lanes] * escale + dexp_ref[:, cols] * x2
            gated.append(y * sz_ref[:, cols])
            snew = jnp.dot(jnp.concatenate(ss, axis=1), rhs, preferred_element_type=F32)
            decay = jnp.broadcast_to(escale[L - 1:L, :], (LANES, LANES))
            statet_ref[g, :, lanes] = statet_ref[g, :, lanes] * decay + snew
        sq = functools.reduce(lambda a, b: a + b, [t * t for t in gated])
        inv = lax.rsqrt(jnp.sum(sq, axis=-1, keepdims=True) * (1.0 / GROUP_X) + RMS_EPS)
        for pr, t in enumerate(gated):
            cols = slice(g * GROUP_X + pr * PAIR, g * GROUP_X + (pr + 1) * PAIR)
            g_ref[:, cols] = (t * inv * nw_ref[:, cols]).astype(g_ref.dtype)

    @pl.when(c == pl.num_programs(0) - 1)
    def _():
        for g in range(SSM_GROUPS):
            for q in range(GROUP_X // LANES):
                blk = slice(q * LANES, (q + 1) * LANES)
                hfin_ref[g, blk, :] = statet_ref[g, :, blk].T


def ssd_prompt(sz, xbc, dt_raw, conv_w, conv_b, dt_bias, a_log, d_exp, norm_w):
    s = sz.shape[0]
    L = SSD_CHUNK
    vec = lambda n: pl.BlockSpec((1, n), lambda c: (0, 0))
    return pl.pallas_call(
        _ssd_prompt_body,
        grid=(s // L,),
        in_specs=[
            pl.BlockSpec((L, D_INNER), lambda c: (c, 0)),
            pl.BlockSpec((L, CONV_DIM), lambda c: (c, 0)),
            pl.BlockSpec((L, LANES), lambda c: (c, 0)),
            pl.BlockSpec((CONV_WIDTH, CONV_DIM), lambda c: (0, 0)),
            vec(CONV_DIM), vec(LANES), vec(LANES), vec(D_INNER), vec(D_INNER),
            pl.BlockSpec((SSM_GROUPS, 2 * LANES, HEADS_PER_GROUP * LANES), lambda c: (0, 0, 0)),
        ],
        out_specs=[
            pl.BlockSpec((L, D_INNER), lambda c: (c, 0)),
            pl.BlockSpec((SSM_GROUPS, GROUP_X, D_STATE), lambda c: (0, 0, 0)),
        ],
        out_shape=[
            jax.ShapeDtypeStruct((s, D_INNER), BF16),
            jax.ShapeDtypeStruct((SSM_GROUPS, GROUP_X, D_STATE), F32),
        ],
        scratch_shapes=[
            pltpu.VMEM((L + 2 * SUBLANES, CONV_DIM), F32),
            pltpu.VMEM((L, CONV_DIM), F32),
            pltpu.VMEM((SSM_GROUPS, D_STATE, GROUP_X), F32),
            pltpu.VMEM((L, 2 * LANES), BF16),
            pltpu.VMEM((LANES, L), F32),
            pltpu.VMEM((LANES, L), F32), pltpu.VMEM((LANES, L), F32),
        ],
        compiler_params=_cparams(("arbitrary",)),
        name="ssd_prompt",
    )(sz, xbc, dt_raw, conv_w, conv_b, dt_bias, a_log, d_exp, norm_w, _head_spread_matrix())


def _decode_conv_body(s0_ref, s1_ref, s2_ref, xn_ref, cw_ref, cb_ref, o_ref):
    acc = cb_ref[...] + cw_ref[0:1, :] * s0_ref[...] + cw_ref[1:2, :] * s1_ref[...]
    acc = acc + cw_ref[2:3, :] * s2_ref[...] + cw_ref[3:4, :] * xn_ref[...]
    o_ref[...] = _silu(acc)


def decode_conv(s0, s1, s2, xnew, conv_w, conv_b):
    b, n = xnew.shape
    blk = pl.BlockSpec((b, CONV_SLAB), lambda j: (0, j))
    return pl.pallas_call(
        _decode_conv_body,
        grid=(n // CONV_SLAB,),
        in_specs=[blk, blk, blk, blk, pl.BlockSpec((CONV_WIDTH, CONV_SLAB), lambda j: (0, j)),
                  pl.BlockSpec((1, CONV_SLAB), lambda j: (0, j))],
        out_specs=blk,
        out_shape=jax.ShapeDtypeStruct((b, n), F32),
        compiler_params=_cparams(("arbitrary",)),
        name="decode_conv",
    )(s0, s1, s2, xnew, conv_w, conv_b)


def _decode_prep_body(xs_ref, b_ref, c_ref, dt_ref, dtb_ref, alog_ref, e_ref, dtx_ref, daexp_ref, bcx_ref):
    lane = lax.broadcasted_iota(jnp.int32, dt_ref.shape, 1)
    dt = jnp.where(lane < SSM_HEADS, _softplus(dt_ref[...] + dtb_ref[...]), 0.0)
    da = jnp.where(lane < SSM_HEADS, jnp.exp(dt * (-jnp.exp(alog_ref[...]))), 0.0)
    e = e_ref[...]
    expand = lambda v: sum(jnp.dot(p, e, preferred_element_type=F32) for p in _split3(v))
    dtx = expand(dt) * xs_ref[...]
    dtx_ref[...] = dtx
    daexp_ref[...] = expand(da)
    bcx_ref[...] = dtx * jnp.sum(b_ref[...] * c_ref[...], axis=-1, keepdims=True)


def decode_prep(xbc_act, dt_raw, dt_bias, a_log, expand_mat):
    b = xbc_act.shape[0]
    xblk = pl.BlockSpec((b, GROUP_X), lambda g: (0, g))
    vec = pl.BlockSpec((1, LANES), lambda g: (0, 0))
    nb = D_INNER // D_STATE
    out = jax.ShapeDtypeStruct((b, D_INNER), F32)
    return pl.pallas_call(
        _decode_prep_body,
        grid=(SSM_GROUPS,),
        in_specs=[
            xblk,
            pl.BlockSpec((b, D_STATE), lambda g: (0, nb + g)),
            pl.BlockSpec((b, D_STATE), lambda g: (0, nb + SSM_GROUPS + g)),
            pl.BlockSpec((b, LANES), lambda g: (0, 0)),
            vec, vec,
            pl.BlockSpec((LANES, GROUP_X), lambda g: (0, g)),
        ],
        out_specs=[xblk, xblk, xblk],
        out_shape=[out, out, out],
        compiler_params=_cparams(("arbitrary",)),
        name="decode_prep",
    )(xbc_act, xbc_act, xbc_act, dt_raw, dt_bias, a_log, expand_mat)


TILE_ROWS = 128
SPLIT_ROWS = 16


SEQS_PER_STEP = 2


def _decode_state_body(h_ref, dtx_ref, da_ref, b_ref, c_ref, ho_ref, hc_ref):
    n_tiles = (SSM_HEADS * SSM_HEAD_DIM) // TILE_ROWS
    rid = lambda w: lax.broadcasted_iota(jnp.int32, (SPLIT_ROWS, w), 0)
    pick = lambda r, pieces, w: functools.reduce(
        lambda acc, kv: jnp.where(r == kv[0], jnp.broadcast_to(kv[1].astype(F32), (SPLIT_ROWS, w)), acc),
        pieces, jnp.zeros((SPLIT_ROWS, w), F32))
    r = rid(D_INNER)
    rb = rid(SSM_GROUPS * D_STATE)
    r1 = rid(D_STATE)
    rd = jnp.where(jnp.logical_and(r1 >= 6, r1 <= 8), 1.0, 0.0).astype(BF16)
    for q in range(SEQS_PER_STEP):
        xh, xm, xl = _split3(dtx_ref[q])
        ah, am, al = _split3(da_ref[q])
        bh, bm, bl = _split3(b_ref[q])
        lmat = pick(r, [(0, xh), (1, xh), (2, xh), (3, xm), (4, xm), (5, xl), (6, ah), (7, am), (8, al)], D_INNER)
        ru = pick(rb, [(0, bh), (1, bm), (2, bl), (3, bh), (4, bm), (5, bh)], SSM_GROUPS * D_STATE).astype(BF16)
        crow = c_ref[q]
        for i in range(n_tiles):
            g = (i * TILE_ROWS) // GROUP_X
            lanes = slice(i * TILE_ROWS, (i + 1) * TILE_ROWS)
            lt = lmat[:, lanes].T.astype(BF16)
            upd = jnp.dot(lt, ru[:, g * D_STATE:(g + 1) * D_STATE], preferred_element_type=F32)
            dec = jnp.dot(lt, rd, preferred_element_type=F32)
            h = h_ref[q, lanes, :]
            ho_ref[q, lanes, :] = h * dec + upd
            t = (h * crow[:, g * D_STATE:(g + 1) * D_STATE]).T
            hc_ref[q, :, lanes] = jnp.sum(t, axis=0, keepdims=True)


def decode_state(h, dtx, daexp, xbc_act3):
    b = h.shape[0]
    nq = SEQS_PER_STEP
    rows = SSM_HEADS * SSM_HEAD_DIM
    hblk = pl.BlockSpec((nq, rows, D_STATE), lambda i: (i, 0, 0))
    rblk = pl.BlockSpec((nq, 1, D_INNER), lambda i: (i, 0, 0))
    bc = SSM_GROUPS * D_STATE
    return pl.pallas_call(
        _decode_state_body,
        grid=(b // nq,),
        in_specs=[
            hblk, rblk, rblk,
            pl.BlockSpec((nq, 1, bc), lambda i: (i, 0, D_INNER // bc)),
            pl.BlockSpec((nq, 1, bc), lambda i: (i, 0, D_INNER // bc + 1)),
        ],
        out_specs=[hblk, rblk],
        out_shape=[jax.ShapeDtypeStruct((b, rows, D_STATE), F32), jax.ShapeDtypeStruct((b, 1, D_INNER), F32)],
        compiler_params=_cparams(("arbitrary",)),
        name="decode_state",
    )(h, dtx, daexp, xbc_act3, xbc_act3)


def _decode_post_body(hc_ref, da_ref, bcx_ref, xs_ref, sz_ref, dexp_ref, nw_ref, o_ref):
    y = da_ref[...] * hc_ref[...] + bcx_ref[...] + dexp_ref[...] * xs_ref[...]
    gt = y * sz_ref[...]
    ms = jnp.mean(gt * gt, axis=-1, keepdims=True)
    o_ref[...] = (gt * lax.rsqrt(ms + RMS_EPS) * nw_ref[...]).astype(o_ref.dtype)


def decode_post(hc, daexp, bcx, xbc_act, sz, d_exp, norm_w):
    b = hc.shape[0]
    xblk = pl.BlockSpec((b, GROUP_X), lambda g: (0, g))
    vblk = pl.BlockSpec((1, GROUP_X), lambda g: (0, g))
    return pl.pallas_call(
        _decode_post_body,
        grid=(SSM_GROUPS,),
        in_specs=[xblk, xblk, xblk, xblk, xblk, vblk, vblk],
        out_specs=xblk,
        out_shape=jax.ShapeDtypeStruct((b, D_INNER), BF16),
        compiler_params=_cparams(("arbitrary",)),
        name="decode_post",
    )(hc, daexp, bcx, xbc_act, sz, d_exp, norm_w)


SUPER = WINDOW_MAX
ATT_N = 128
MERGE_ROWS = 256


def _rel_bucket(dist):
    max_exact = NUM_BUCKETS // 2
    df = jnp.maximum(dist, max_exact).astype(F32)
    large = max_exact + (jnp.log(df / max_exact) / math.log(MAX_DISTANCE / max_exact)
                         * (NUM_BUCKETS - max_exact)).astype(jnp.int32)
    return jnp.where(dist < max_exact, dist, jnp.minimum(large, NUM_BUCKETS - 1))


def _prompt_bias_rows(rel_bias):
    n = ATT_N
    back = n - jnp.arange(2 * n)
    rows = []
    for g, (_, dil) in enumerate(DILATION_GROUPS):
        tab = rel_bias[:, g * N_HEADS_B:(g + 1) * N_HEADS_B]
        bias = tab[_rel_bucket(jnp.clip(back, 0, n) * dil)].astype(F32) * LOG2E
        rows.append(jnp.where((back >= 0)[:, None], bias, NEG_BIG).T)
    return jnp.stack(rows, axis=0)[:, :, None, :]


def _rows(start, size, stride):
    return pl.ds(start, size) if stride == 1 else pl.ds(start, size, stride=stride)


def _attn_prompt_body(q0_ref, q1_ref, q2_ref, kc_ref, kp_ref, vc_ref, vp_ref, t_ref, o_ref, og_ref, lse_ref):
    first = pl.program_id(0) == 0
    n = ATT_N
    scale = HEAD_DIM ** -0.5 * LOG2E
    key_id = lax.broadcasted_iota(jnp.int32, (n, 2 * n), 1)
    q_refs = (q0_ref, q1_ref, q2_ref)
    for g, (win, dil) in enumerate(DILATION_GROUPS):
        nblk = SUPER // (n * dil)
        bias = pltpu.roll(jnp.broadcast_to(t_ref[g, 0], (n, 2 * n)), 0, 1, stride=1, stride_axis=0)
        for r in range(dil):
            prev = _rows((nblk - 1) * n * dil + r, n, dil)
            kprev, vprev = kp_ref[prev, :].astype(BF16), vp_ref[prev, :].astype(BF16)
            for c in range(nblk):
                start = c * n * dil + r
                cur = _rows(start, n, dil)
                qb = q_refs[g][cur, :].astype(BF16)
                kcur, vcur = kc_ref[cur, :].astype(BF16), vc_ref[cur, :].astype(BF16)
                keys = jnp.concatenate([kprev, kcur], axis=0)
                vals = jnp.concatenate([vprev, vcur], axis=0)
                kprev, vprev = kcur, vcur
                s = _dot_nt(qb, keys) * scale + bias
                if c == 0:
                    s = jnp.where(jnp.logical_and(first, key_id < n), NEG_BIG, s)
                m = jnp.max(s, axis=-1, keepdims=True)
                p = jnp.exp2(s - m)
                den = jnp.sum(p, axis=-1, keepdims=True)
                o = jnp.dot(p.astype(BF16), vals, preferred_element_type=F32) * (1.0 / den)
                og_ref[g, cur, :] = o
                lse_ref[g, cur, :] = jnp.broadcast_to(m + jnp.log2(den), (n, LANES))
    for i in range(SUPER // MERGE_ROWS):
        rows = slice(i * MERGE_ROWS, (i + 1) * MERGE_ROWS)
        ls = [lse_ref[g, rows, :] for g in range(N_GROUPS_B)]
        mx = functools.reduce(jnp.maximum, ls)
        es = [jnp.exp2(l - mx) for l in ls]
        tot = functools.reduce(lambda a, b: a + b, es)
        acc = functools.reduce(lambda a, b: a + b, [es[g] * og_ref[g, rows, :] for g in range(N_GROUPS_B)])
        o_ref[rows, :] = (acc * (1.0 / tot)).astype(o_ref.dtype)


def attn_prompt(q, kv, bias_rows):
    s = q.shape[0]
    blk = lambda f: pl.BlockSpec((SUPER, HEAD_DIM), f)
    return pl.pallas_call(
        _attn_prompt_body,
        grid=(s // SUPER, N_HEADS_B),
        in_specs=[
            blk(lambda c, h: (c, h)),
            blk(lambda c, h: (c, N_HEADS_B + h)),
            blk(lambda c, h: (c, 2 * N_HEADS_B + h)),
            blk(lambda c, h: (c, h)),
            blk(lambda c, h: (jnp.maximum(c - 1, 0), h)),
            blk(lambda c, h: (c, N_HEADS_B + h)),
            blk(lambda c, h: (jnp.maximum(c - 1, 0), N_HEADS_B + h)),
            pl.BlockSpec((N_GROUPS_B, 1, 1, 2 * ATT_N), lambda c, h: (0, h, 0, 0)),
        ],
        out_specs=blk(lambda c, h: (c, h)),
        out_shape=jax.ShapeDtypeStruct((s, KV_DIM), BF16),
        scratch_shapes=[
            pltpu.VMEM((N_GROUPS_B, SUPER, HEAD_DIM), F32),
            pltpu.VMEM((N_GROUPS_B, SUPER, LANES), F32),
        ],
        compiler_params=_cparams(("arbitrary", "arbitrary")),
        name="attn_prompt",
    )(q, q, q, kv, kv, kv, kv, bias_rows)


N_BUF = min(WINDOW_MAX, PAST_LEN)
ROW_MAJOR = 16
ROW_MID = 4


def _decode_bias_tables(rel_bias):
    n = ATT_N
    j = n - jnp.arange(n)
    cached, new = [], []
    for g, (_, dil) in enumerate(DILATION_GROUPS):
        tab = rel_bias[:, g * N_HEADS_B:(g + 1) * N_HEADS_B]
        cached.append(tab[_rel_bucket(j * dil)])
        new.append(tab[_rel_bucket(jnp.zeros((), jnp.int32))])
    cached = jnp.broadcast_to(jnp.stack(cached)[..., None], (N_GROUPS_B, n, N_HEADS_B, LANES)).astype(F32)
    new = jnp.broadcast_to(jnp.stack(new)[..., None], (N_GROUPS_B, N_HEADS_B, LANES)).astype(F32)
    return cached, new


def _attn_decode_body(q_ref, kvn_ref, k0_ref, v0_ref, k1_ref, v1_ref, k2_ref, v2_ref, bc_ref, bn_ref, o_ref):
    n = ATT_N
    scale = HEAD_DIM ** -0.5
    kv_refs = ((k0_ref, v0_ref), (k1_ref, v1_ref), (k2_ref, v2_ref))
    for b in range(SEQS_PER_STEP):
        kn, vn = kvn_ref[b, 0], kvn_ref[b, 1]
        outs, lses = [], []
        for g in range(N_GROUPS_B):
            q = q_ref[b, g] * scale
            k = kv_refs[g][0][b].reshape(n, N_HEADS_B, HEAD_DIM)
            v = kv_refs[g][1][b].reshape(n, N_HEADS_B, HEAD_DIM)
            s = jnp.sum(k * q[None], axis=-1, keepdims=True) + bc_ref[g]
            s0 = jnp.sum(kn * q, axis=-1, keepdims=True) + bn_ref[g]
            m = jnp.maximum(jnp.max(s, axis=0), s0)
            p = jnp.exp(s - m[None])
            p0 = jnp.exp(s0 - m)
            den = jnp.sum(p, axis=0) + p0
            outs.append((jnp.sum(p * v, axis=0) + p0 * vn) * (1.0 / den))
            lses.append(m + jnp.log(den))
        mx = functools.reduce(jnp.maximum, lses)
        es = [jnp.exp(l - mx) for l in lses]
        tot = functools.reduce(lambda x, y: x + y, es)
        acc = functools.reduce(lambda x, y: x + y, [e * o for e, o in zip(es, outs)])
        o_ref[b] = (acc * (1.0 / tot)).astype(o_ref.dtype)


def attn_decode(q, kv_new, cache_k, cache_v, bias_cached, bias_new):
    b = q.shape[0]
    view = (b, N_BUF // ROW_MAJOR, ROW_MAJOR // ROW_MID, ROW_MID, N_HEADS_B, HEAD_DIM)
    ck, cv = cache_k.reshape(view), cache_v.reshape(view)
    n = ATT_N
    nq = SEQS_PER_STEP
    hd = (N_HEADS_B, HEAD_DIM)
    g0 = pl.BlockSpec((nq, n // ROW_MAJOR, ROW_MAJOR // ROW_MID, ROW_MID) + hd,
                      lambda i: (i, N_BUF // n - 1, 0, 0, 0, 0))
    g1 = pl.BlockSpec((nq, n * ROW_MID // ROW_MAJOR, ROW_MAJOR // ROW_MID, None) + hd,
                      lambda i: (i, N_BUF // (n * ROW_MID) - 1, 0, 0, 0, 0))
    g2 = pl.BlockSpec((nq, n, None, None) + hd, lambda i: (i, 0, 0, 0, 0, 0))
    return pl.pallas_call(
        _attn_decode_body,
        grid=(b // nq,),
        in_specs=[
            pl.BlockSpec((nq, N_GROUPS_B) + hd, lambda i: (i, 0, 0, 0)),
            pl.BlockSpec((nq, 2) + hd, lambda i: (i, 0, 0, 0)),
            g0, g0, g1, g1, g2, g2,
            pl.BlockSpec((N_GROUPS_B, n) + (N_HEADS_B, LANES), lambda i: (0, 0, 0, 0)),
            pl.BlockSpec((N_GROUPS_B, N_HEADS_B, LANES), lambda i: (0, 0, 0)),
        ],
        out_specs=pl.BlockSpec((nq,) + hd, lambda i: (i, 0, 0)),
        out_shape=jax.ShapeDtypeStruct((b,) + hd, BF16),
        compiler_params=_cparams(("arbitrary",)),
        name="attn_decode",
    )(q, kv_new, ck, cv, ck, cv, ck, cv, bias_cached, bias_new)


ADA_ROWS = 136
Z_TILE = 1024


def _pad_lanes(v):
    return jnp.pad(v.astype(F32), (0, LANES - v.shape[0]))[None]


TM = 1024
TM_LONG_K = 512
TM_EW = 256


def _trunk(xs, adas, mixers_a, mixers_b, p):
    tms = (TM_EW, xs[1].shape[0])
    vec = lambda a, l, s: a[l, s][None]
    sh1, sc1, g1, sh2, sc2, g2 = zip(*(a[0] for a in adas))
    sh1b, sc1b, g1b, sh2b, sc2b, g2b = zip(*(a[1] for a in adas))
    shkv, sckv = zip(*(a[2] for a in adas))
    paths = range(2)

    def ln(x, y, gate, l, s, mods):
        outs = [post_ln(x[i], y[i], gate[i], vec(p["ln_g"], l, s), vec(p["ln_b"], l, s),
                        [(sc[i], sh[i]) for sc, sh in mods], tm=tms[i]) for i in paths]
        return zip(*outs)

    def mm(a, w, **kw):
        tm = kw.pop("tm", TM)
        return matmul(a[0], w, rider=a[1], tm=tm, **kw)

    u = [modulate(xs[i], sc1[i], sh1[i], tm=tms[i]) for i in paths]
    w_in_t = jnp.swapaxes(p["ssm_w_in"], 1, 2).reshape(-1, D_MODEL)
    sz = mm(u, w_in_t, w_transposed=True, out_act="silu", tn=Z_TILE, n_out=D_INNER, name="in_z")
    xbc = mm(u, w_in_t, w_transposed=True, tn=Z_TILE, n_out=CONV_DIM, col_off=D_INNER // Z_TILE, name="in_xbc")
    dt = mm(u, w_in_t, w_transposed=True, tn=LANES, n_out=LANES, col_off=(D_INNER + CONV_DIM) // LANES, name="in_dt")
    gated, extras = zip(*(mixers_a[i](sz[i], xbc[i], dt[i]) for i in paths))
    y = mm(gated, p["ssm_w_out"].reshape(D_INNER, D_MODEL), tn=512, out_dtype=BF16, name="ssm_out")
    x, u = ln(xs, y, g1, 0, 0, [(sc2, sh2)])
    h = glu_matmul(u[0], u[1], p["ffn_w_gate"], p["ffn_w_up"], layer=0, tm=TM, tn=512, name="ffn0_glu")
    y = mm(h, p["ffn_w_down"], layer=0, tm=TM_LONG_K, tn=512, out_dtype=BF16, name="ffn0_down")
    x, u, ukv = ln(x, y, g2, 0, 1, [(sc1b, sh1b), (sckv, shkv)])

    kv = mm(ukv, p["w_kv"], tn=1024, name="kv_proj")
    q = mm(u, p["attn_w_q"], layer=0, tn=1024, name="q_proj")
    o = [mixers_b[i](q[i], kv[i]) for i in paths]
    y = mm(o, p["attn_w_o"], layer=0, tn=1024, out_dtype=BF16, name="attn_out")
    x, u = ln(x, y, g1b, 1, 0, [(sc2b, sh2b)])
    h = glu_matmul(u[0], u[1], p["ffn_w_gate"], p["ffn_w_up"], layer=1, tm=TM, tn=512, name="ffn1_glu")
    y = mm(h, p["ffn_w_down"], layer=1, tm=TM_LONG_K, tn=512, out_dtype=BF16, name="ffn1_down")
    (x,) = ln(x, y, g2b, 1, 1, [])
    return [(x[i], extras[i], kv[i]) for i in paths]


def kernel(x_prompt, x_sample, state_conv, state_ssm, cache_k, cache_v, c_prompt, c_sample, w_ada, b_ada, ln_g, ln_b, ffn_w_gate, ffn_w_up, ffn_w_down, ssm_w_in, ssm_conv_w, ssm_conv_b, ssm_dt_bias, ssm_a_log, ssm_d, ssm_norm_w, ssm_w_out, kv_w_ada, kv_b_ada, w_kv, attn_w_q, attn_w_o, rel_bias):
    p = dict(ln_g=ln_g, ln_b=ln_b, ffn_w_gate=ffn_w_gate, ffn_w_up=ffn_w_up, ffn_w_down=ffn_w_down,
             ssm_w_in=ssm_w_in, ssm_w_out=ssm_w_out, w_kv=w_kv, attn_w_q=attn_w_q, attn_w_o=attn_w_o)
    s, b, d = SEQ, DEC_BATCH, D_MODEL

    c_all = jnp.concatenate([c_sample, c_prompt, jnp.zeros((ADA_ROWS - b - 1, d), F32)], axis=0)
    ada = [matmul(c_all, w_ada, layer=l, tm=ADA_ROWS, tn=1024, bias=b_ada[l][None], act="silu", name=f"ada{l}")
           for l in range(DEPTH)]
    adakv = matmul(c_all, kv_w_ada, tm=ADA_ROWS, tn=1024, bias=kv_b_ada[None], act="silu", name="ada_kv")
    split = lambda m, n, rows: [m[rows, i * d:(i + 1) * d] for i in range(n)]
    smp, prm = slice(0, b), slice(b, b + 1)

    conv_w, conv_b = ssm_conv_w[0], ssm_conv_b[0][None]
    dt_bias, a_log = _pad_lanes(ssm_dt_bias[0]), _pad_lanes(ssm_a_log[0])
    d_exp = jnp.repeat(ssm_d[0].astype(F32), SSM_HEAD_DIM)[None]
    norm_w = ssm_norm_w[0][None]

    def mixer_a_prompt(sz, xbc, dt):
        gated, hfin = ssd_prompt(sz, xbc, dt, conv_w, conv_b, dt_bias, a_log, d_exp, norm_w)
        return gated, (xbc[s - (CONV_WIDTH - 1):], hfin)

    bias_rows = _prompt_bias_rows(rel_bias)
    mixer_b_prompt = lambda q, kv: attn_prompt(q, kv, bias_rows)

    expand_mat = (lax.broadcasted_iota(jnp.int32, (LANES, D_INNER), 0)
                  == lax.broadcasted_iota(jnp.int32, (LANES, D_INNER), 1) // SSM_HEAD_DIM).astype(BF16)

    def mixer_a_decode(sz, xbc, dt):
        st = state_conv[0]
        xa = decode_conv(st[:, 0], st[:, 1], st[:, 2], xbc, conv_w, conv_b)
        dtx, daexp, bcx = decode_prep(xa, dt, dt_bias, a_log, expand_mat)
        hnew, hc = decode_state(state_ssm.reshape(b, SSM_HEADS * SSM_HEAD_DIM, D_STATE),
                                dtx[:, None], daexp[:, None], xa[:, None])
        gated = decode_post(hc[:, 0], daexp, bcx, xa, sz, d_exp, norm_w)
        new_conv = jnp.concatenate([st[:, 1:], xbc[:, None]], axis=1)
        return gated, (new_conv, hnew)

    bias_cached, bias_new = _decode_bias_tables(rel_bias)

    def mixer_b_decode(q, kv):
        o = attn_decode(q.reshape(b, N_GROUPS_B, N_HEADS_B, HEAD_DIM), kv.reshape(b, 2, N_HEADS_B, HEAD_DIM),
                        cache_k, cache_v, bias_cached, bias_new)
        return o.reshape(b, KV_DIM)

    adas = [(split(ada[0], 6, rows), split(ada[1], 6, rows), split(adakv, 2, rows)) for rows in (prm, smp)]
    (y_p, (conv_p, ssm_p), kv_p), (y_s, (conv_s, ssm_s), kv_s) = _trunk(
        (x_prompt.reshape(s, d), x_sample.reshape(b, d)), adas,
        (mixer_a_prompt, mixer_a_decode), (mixer_b_prompt, mixer_b_decode), p)

    n_keep = min(WINDOW_MAX, s)
    heads = (N_HEADS_B, HEAD_DIM)
    return (
        y_p.reshape(1, s, d),
        y_s.reshape(b, 1, d),
        conv_p.reshape(1, 1, CONV_WIDTH - 1, CONV_DIM),
        ssm_p.reshape(1, 1, SSM_HEADS, SSM_HEAD_DIM, D_STATE),
        kv_p[s - n_keep:, :KV_DIM].reshape((1, n_keep) + heads),
        kv_p[s - n_keep:, KV_DIM:].reshape((1, n_keep) + heads),
        conv_s.reshape(1, b, CONV_WIDTH - 1, CONV_DIM),
        ssm_s.reshape(1, b, SSM_HEADS, SSM_HEAD_DIM, D_STATE),
        kv_s[:, :KV_DIM].reshape((b, 1) + heads),
        kv_s[:, KV_DIM:].reshape((b, 1) + heads),
    )
```

```python
import functools
import math

import jax
import jax.numpy as jnp
from jax import lax
from jax.experimental import pallas as pl
from jax.experimental.pallas import tpu as pltpu

F32 = jnp.float32
BF16 = jnp.bfloat16

D_MODEL = 2048
SEQ = 8192
DEPTH = 2
DEC_BATCH = 128
PAST_LEN = 2048
D_INNER = 2 * D_MODEL
SSM_HEAD_DIM = 64
SSM_HEADS = D_INNER // SSM_HEAD_DIM
SSM_GROUPS = 8
HEADS_PER_GROUP = SSM_HEADS // SSM_GROUPS
D_STATE = 128
CONV_WIDTH = 4
CONV_DIM = D_INNER + 2 * SSM_GROUPS * D_STATE
SSD_CHUNK = 128
RMS_EPS = 1e-5
HEAD_DIM = 128
N_HEADS_B = D_MODEL // HEAD_DIM
DILATION_GROUPS = ((128, 1), (512, 4), (2048, 16))
N_GROUPS_B = len(DILATION_GROUPS)
WINDOW_MAX = max(w for w, _ in DILATION_GROUPS)
Q_DIM = N_GROUPS_B * N_HEADS_B * HEAD_DIM
KV_DIM = N_HEADS_B * HEAD_DIM
NUM_BUCKETS = 32
MAX_DISTANCE = WINDOW_MAX
D_FF = -(-8 * D_MODEL // (3 * 256)) * 256
ALPHA = (2 * DEPTH) ** 0.25
LN_EPS = 1e-5

LANES = 128
SUBLANES = 8
VMEM_LIMIT_BYTES = 56 * 1024 * 1024
NEG_BIG = -1e30
LOG2E = math.log2(math.e)


def _cparams(semantics):
    return pltpu.CompilerParams(dimension_semantics=semantics, vmem_limit_bytes=VMEM_LIMIT_BYTES)


def _silu(x):
    return x * (1.0 / (1.0 + jnp.exp2(x * -LOG2E)))


def _split3(x):
    hi = x.astype(BF16)
    r1 = x - hi.astype(F32)
    mid = r1.astype(BF16)
    lo = (r1 - mid.astype(F32)).astype(BF16)
    return hi, mid, lo


CAST_ROWS = 256


def _cast_tile(w_ref, wbf_ref, transposed=False):
    k, tn = wbf_ref.shape
    if transposed:
        step = min(CAST_ROWS, tn)
        for c in range(tn // step):
            wbf_ref[:, c * step:(c + 1) * step] = w_ref[c * step:(c + 1) * step, :].T.astype(BF16)
        return
    rows = CAST_ROWS if k % CAST_ROWS == 0 else k
    def body(r, c):
        off = pl.multiple_of(r * rows, rows)
        wbf_ref[pl.ds(off, rows), :] = w_ref[pl.ds(off, rows), :].astype(BF16)
        return c
    lax.fori_loop(0, k // rows, body, 0)


def _mm_body(*refs, act, out_act, has_bias, has_rider, w_transposed):
    refs = list(refs)
    a_ref, w_ref = refs.pop(0), refs.pop(0)
    b_ref = refs.pop(0) if has_bias else None
    a2_ref = refs.pop(0) if has_rider else None
    o_ref = refs.pop(0)
    o2_ref = refs.pop(0) if has_rider else None
    wbf_ref = refs.pop(0)

    def product(lhs_ref):
        a = lhs_ref[...]
        if act == "silu":
            a = _silu(a.astype(F32))
        acc = jnp.dot(a.astype(BF16), wbf_ref[...], preferred_element_type=F32)
        if has_bias:
            acc = acc + b_ref[...]
        return _silu(acc) if out_act == "silu" else acc

    @pl.when(pl.program_id(1) == 0)
    def _():
        _cast_tile(w_ref, wbf_ref, w_transposed)
        if has_rider:
            o2_ref[...] = product(a2_ref).astype(o2_ref.dtype)

    o_ref[...] = product(a_ref).astype(o_ref.dtype)


def _weight_spec(w, layer, tn, col_off=0, transposed=False):
    if transposed:
        return pl.BlockSpec((tn, w.shape[1]), lambda j, i: (j + col_off, 0))
    k = w.shape[-2]
    if w.ndim == 2:
        return pl.BlockSpec((k, tn), lambda j, i: (0, j + col_off))
    return pl.BlockSpec((None, k, tn), lambda j, i: (layer, 0, j + col_off))


def matmul(a, w, *, tm, tn, layer=0, n_out=None, col_off=0, bias=None, act=None, out_act=None, rider=None,
           w_transposed=False, out_dtype=F32, name="mm"):
    m, k = a.shape
    n_out = (w.shape[0] if w_transposed else w.shape[-1]) if n_out is None else n_out
    grid = (pl.cdiv(n_out, tn), m // tm)
    in_specs = [
        pl.BlockSpec((tm, k), lambda j, i: (i, 0)),
        _weight_spec(w, layer, tn, col_off, w_transposed),
    ]
    args = [a, w]
    out_specs = [pl.BlockSpec((tm, tn), lambda j, i: (i, j))]
    out_shape = [jax.ShapeDtypeStruct((m, n_out), out_dtype)]
    if bias is not None:
        in_specs.append(pl.BlockSpec((1, tn), lambda j, i: (0, j + col_off)))
        args.append(bias)
    if rider is not None:
        m2 = rider.shape[0]
        in_specs.append(pl.BlockSpec((m2, k), lambda j, i: (0, 0)))
        args.append(rider)
        out_specs.append(pl.BlockSpec((m2, tn), lambda j, i: (0, j)))
        out_shape.append(jax.ShapeDtypeStruct((m2, n_out), out_dtype))
    outs = pl.pallas_call(
        functools.partial(_mm_body, act=act, out_act=out_act, has_bias=bias is not None,
                          has_rider=rider is not None, w_transposed=w_transposed),
        grid=grid,
        in_specs=in_specs,
        out_specs=out_specs,
        out_shape=out_shape,
        scratch_shapes=[pltpu.VMEM((k, tn), BF16)],
        compiler_params=_cparams(("arbitrary", "arbitrary")),
        name=name,
    )(*args)
    return outs if rider is not None else outs[0]


def _glu_body(a_ref, wg_ref, wu_ref, a2_ref, o_ref, o2_ref, wgbf_ref, wubf_ref):
    def product(lhs_ref):
        a = lhs_ref[...]
        g = jnp.dot(a, wgbf_ref[...], preferred_element_type=F32)
        u = jnp.dot(a, wubf_ref[...], preferred_element_type=F32)
        return (_silu(g) * u).astype(BF16)

    @pl.when(pl.program_id(1) == 0)
    def _():
        _cast_tile(wg_ref, wgbf_ref)
        _cast_tile(wu_ref, wubf_ref)
        o2_ref[...] = product(a2_ref)

    o_ref[...] = product(a_ref)


def glu_matmul(a, rider, w_gate, w_up, *, tm, tn, layer=0, name="glu"):
    m, k = a.shape
    m2 = rider.shape[0]
    n = w_gate.shape[-1]
    return pl.pallas_call(
        _glu_body,
        grid=(n // tn, m // tm),
        in_specs=[
            pl.BlockSpec((tm, k), lambda j, i: (i, 0)),
            _weight_spec(w_gate, layer, tn),
            _weight_spec(w_up, layer, tn),
            pl.BlockSpec((m2, k), lambda j, i: (0, 0)),
        ],
        out_specs=[pl.BlockSpec((tm, tn), lambda j, i: (i, j)), pl.BlockSpec((m2, tn), lambda j, i: (0, j))],
        out_shape=[jax.ShapeDtypeStruct((m, n), BF16), jax.ShapeDtypeStruct((m2, n), BF16)],
        scratch_shapes=[pltpu.VMEM((k, tn), BF16), pltpu.VMEM((k, tn), BF16)],
        compiler_params=_cparams(("arbitrary", "arbitrary")),
        name=name,
    )(a, w_gate, w_up, rider)


def _row_spec(arr, tm):
    if arr.shape[0] == 1:
        return pl.BlockSpec((1, arr.shape[1]), lambda i: (0, 0))
    return pl.BlockSpec((tm, arr.shape[1]), lambda i: (i, 0))


def _modulate_body(x_ref, sc_ref, sh_ref, o_ref):
    o_ref[...] = (x_ref[...] * (1.0 + sc_ref[...]) + sh_ref[...]).astype(o_ref.dtype)


def modulate(x, scale, shift, *, tm):
    m, d = x.shape
    return pl.pallas_call(
        _modulate_body,
        grid=(m // tm,),
        in_specs=[pl.BlockSpec((tm, d), lambda i: (i, 0)), _row_spec(scale, tm), _row_spec(shift, tm)],
        out_specs=pl.BlockSpec((tm, d), lambda i: (i, 0)),
        out_shape=jax.ShapeDtypeStruct((m, d), BF16),
        compiler_params=_cparams(("arbitrary",)),
        name="modulate",
    )(x, scale, shift)


def _ln_body(*refs, n_mod):
    x_ref, y_ref, gate_ref, g_ref, b_ref = refs[:5]
    mod_refs = refs[5:5 + 2 * n_mod]
    xo_ref = refs[5 + 2 * n_mod]
    u_refs = refs[6 + 2 * n_mod:]
    t = ALPHA * x_ref[...] + (1.0 + gate_ref[...]) * y_ref[...].astype(F32)
    mu = jnp.mean(t, axis=-1, keepdims=True)
    tc = t - mu
    var = jnp.mean(tc * tc, axis=-1, keepdims=True)
    xn = tc * lax.rsqrt(var + LN_EPS) * g_ref[...] + b_ref[...]
    xo_ref[...] = xn
    for q in range(n_mod):
        u_refs[q][...] = (xn * (1.0 + mod_refs[2 * q][...]) + mod_refs[2 * q + 1][...]).astype(BF16)


def post_ln(x, y, gate, ln_g, ln_b, mods, *, tm):
    m, d = x.shape
    row = pl.BlockSpec((tm, d), lambda i: (i, 0))
    vec = pl.BlockSpec((1, d), lambda i: (0, 0))
    in_specs = [row, row, _row_spec(gate, tm), vec, vec]
    args = [x, y, gate, ln_g, ln_b]
    for sc, sh in mods:
        in_specs += [_row_spec(sc, tm), _row_spec(sh, tm)]
        args += [sc, sh]
    outs = pl.pallas_call(
        functools.partial(_ln_body, n_mod=len(mods)),
        grid=(m // tm,),
        in_specs=in_specs,
        out_specs=[row] * (1 + len(mods)),
        out_shape=[jax.ShapeDtypeStruct((m, d), F32)] + [jax.ShapeDtypeStruct((m, d), BF16)] * len(mods),
        compiler_params=_cparams(("arbitrary",)),
        name="post_ln",
    )(*args)
    return outs


CONV_SLAB = 512
GROUP_X = D_INNER // SSM_GROUPS
PAIR = 2 * SSM_HEAD_DIM


def _softplus(x):
    return jnp.maximum(x, 0.0) + jnp.log1p(jnp.exp(-jnp.abs(x)))


def _dot_nt(a, b):
    return lax.dot_general(a, b, (((1,), (1,)), ((), ())), preferred_element_type=F32)


def _dot_tn(a, b):
    return lax.dot_general(a, b, (((0,), (0,)), ((), ())), preferred_element_type=F32)


def _head_spread_matrix():
    r = lax.broadcasted_iota(jnp.int32, (SSM_GROUPS, 2 * LANES, HEADS_PER_GROUP * LANES), 1)
    col = lax.broadcasted_iota(jnp.int32, (SSM_GROUPS, 2 * LANES, HEADS_PER_GROUP * LANES), 2)
    g = lax.broadcasted_iota(jnp.int32, (SSM_GROUPS, 2 * LANES, HEADS_PER_GROUP * LANES), 0)
    hit = jnp.logical_and(r % SSM_HEADS == g * HEADS_PER_GROUP + col // LANES, r < 3 * SSM_HEADS)
    return hit.astype(BF16)


def _ssd_prompt_body(sz_ref, xbc_ref, dt_ref, cw_ref, cb_ref, dtb_ref, alog_ref, dexp_ref, nw_ref, e_ref,
                     g_ref, hfin_ref,
                     ext_ref, act_ref, statet_ref, p_ref, wt_ref, rowp_ref):
    c = pl.program_id(0)
    L = SSD_CHUNK

    @pl.when(c == 0)
    def _():
        ext_ref[0:SUBLANES, :] = jnp.zeros((SUBLANES, CONV_DIM), F32)
        statet_ref[...] = jnp.zeros(statet_ref.shape, F32)

    ext_ref[SUBLANES:SUBLANES + L, :] = xbc_ref[...]
    for s in range(CONV_DIM // LANES):
        cols = slice(s * LANES, (s + 1) * LANES)
        acc = cb_ref[:, cols]
        for i in range(CONV_WIDTH):
            lo = SUBLANES - (CONV_WIDTH - 1) + i
            acc = acc + cw_ref[i:i + 1, cols] * ext_ref[lo:lo + L, cols]
        act_ref[:, cols] = _silu(acc)
    ext_ref[0:SUBLANES, :] = ext_ref[L:L + SUBLANES, :]

    lane = lax.broadcasted_iota(jnp.int32, (L, LANES), 1)
    row = lax.broadcasted_iota(jnp.int32, (L, LANES), 0)
    head_ok = lane < SSM_HEADS
    dt = jnp.where(head_ok, _softplus(dt_ref[...] + dtb_ref[...]), 0.0)
    da = dt * (-jnp.exp(alog_ref[...]))
    causal = row >= lane
    tril = jnp.where(causal, 1.0, 0.0).astype(BF16)
    cs = sum(jnp.dot(tril, p, preferred_element_type=F32) for p in _split3(da)) * LOG2E
    hi, mid, lo = _split3(cs)
    p_ref[:, 0:LANES] = jnp.where(head_ok, hi.astype(F32), pltpu.roll(mid.astype(F32), SSM_HEADS, 1)).astype(BF16)
    p_ref[:, LANES:2 * LANES] = lo
    cst = cs.T
    dtt = dt.T
    rowp_ref[...] = cst - jnp.log2(dtt)
    wt_ref[...] = dtt * jnp.exp2(jnp.broadcast_to(cst[:, L - 1:L], (LANES, L)) - cst)
    lane_lo = lane < SSM_HEAD_DIM

    b_off, c_off = D_INNER, D_INNER + SSM_GROUPS * D_STATE
    for g in range(SSM_GROUPS):
        bg = act_ref[:, b_off + g * D_STATE:b_off + (g + 1) * D_STATE]
        bg_bf = bg.astype(BF16)
        bgt = bg.T
        cg = act_ref[:, c_off + g * D_STATE:c_off + (g + 1) * D_STATE].astype(BF16)
        cb = _dot_nt(cg, bg_bf)
        yoff = jnp.dot(cg, statet_ref[g].astype(BF16), preferred_element_type=F32)
        colb = jnp.dot(p_ref[...], e_ref[g], preferred_element_type=F32)
        heads = slice(g * HEADS_PER_GROUP, (g + 1) * HEADS_PER_GROUP)
        rowpg, wtg = rowp_ref[heads, :], wt_ref[heads, :]
        gated = []
        for pr in range(HEADS_PER_GROUP // 2):
            lanes = slice(pr * PAIR, (pr + 1) * PAIR)
            cols = slice(g * GROUP_X + pr * PAIR, g * GROUP_X + (pr + 1) * PAIR)
            x2 = act_ref[:, cols]
            ms, ecols, ss = [], [], []
            for k in (2 * pr, 2 * pr + 1):
                col = colb[:, k * LANES:(k + 1) * LANES]
                rowv = jnp.broadcast_to(rowpg[k:k + 1, :], (L, LANES))
                lmat = jnp.exp2(jnp.where(causal, col - rowv, NEG_BIG))
                ms.append((cb * lmat).astype(BF16))
                ecols.append(jnp.exp2(col))
                ss.append((bgt * jnp.broadcast_to(wtg[k:k + 1, :], (LANES, L))).astype(BF16))
            rhs = jnp.concatenate([jnp.where(lane_lo, x2, 0.0), jnp.where(lane_lo, 0.0, x2)], axis=0).astype(BF16)
            ydiag = jnp.dot(jnp.concatenate(ms, axis=1), rhs, preferred_element_type=F32)
            escale = jnp.where(lane_lo, ecols[0], ecols[1])
            y = ydiag + yoff[:, lanes] * escale + dexp_ref[:, cols] * x2
            gated.append(y * sz_ref[:, cols])
            snew = jnp.dot(jnp.concatenate(ss, axis=1), rhs, preferred_element_type=F32)
            decay = jnp.broadcast_to(escale[L - 1:L, :], (LANES, LANES))
            statet_ref[g, :, lanes] = statet_ref[g, :, lanes] * decay + snew
        sq = functools.reduce(lambda a, b: a + b, [t * t for t in gated])
        inv = lax.rsqrt(jnp.sum(sq, axis=-1, keepdims=True) * (1.0 / GROUP_X) + RMS_EPS)
        for pr, t in enumerate(gated):
            cols = slice(g * GROUP_X + pr * PAIR, g * GROUP_X + (pr + 1) * PAIR)
            g_ref[:, cols] = (t * inv * nw_ref[:, cols]).astype(g_ref.dtype)

    @pl.when(c == pl.num_programs(0) - 1)
    def _():
        for g in range(SSM_GROUPS):
            for q in range(GROUP_X // LANES):
                blk = slice(q * LANES, (q + 1) * LANES)
                hfin_ref[g, blk, :] = statet_ref[g, :, blk].T


def ssd_prompt(sz, xbc, dt_raw, conv_w, conv_b, dt_bias, a_log, d_exp, norm_w):
    s = sz.shape[0]
    L = SSD_CHUNK
    vec = lambda n: pl.BlockSpec((1, n), lambda c: (0, 0))
    return pl.pallas_call(
        _ssd_prompt_body,
        grid=(s // L,),
        in_specs=[
            pl.BlockSpec((L, D_INNER), lambda c: (c, 0)),
            pl.BlockSpec((L, CONV_DIM), lambda c: (c, 0)),
            pl.BlockSpec((L, LANES), lambda c: (c, 0)),
            pl.BlockSpec((CONV_WIDTH, CONV_DIM), lambda c: (0, 0)),
            vec(CONV_DIM), vec(LANES), vec(LANES), vec(D_INNER), vec(D_INNER),
            pl.BlockSpec((SSM_GROUPS, 2 * LANES, HEADS_PER_GROUP * LANES), lambda c: (0, 0, 0)),
        ],
        out_specs=[
            pl.BlockSpec((L, D_INNER), lambda c: (c, 0)),
            pl.BlockSpec((SSM_GROUPS, GROUP_X, D_STATE), lambda c: (0, 0, 0)),
        ],
        out_shape=[
            jax.ShapeDtypeStruct((s, D_INNER), BF16),
            jax.ShapeDtypeStruct((SSM_GROUPS, GROUP_X, D_STATE), F32),
        ],
        scratch_shapes=[
            pltpu.VMEM((L + 2 * SUBLANES, CONV_DIM), F32),
            pltpu.VMEM((L, CONV_DIM), F32),
            pltpu.VMEM((SSM_GROUPS, D_STATE, GROUP_X), F32),
            pltpu.VMEM((L, 2 * LANES), BF16),
            pltpu.VMEM((LANES, L), F32),
            pltpu.VMEM((LANES, L), F32),
        ],
        compiler_params=_cparams(("arbitrary",)),
        name="ssd_prompt",
    )(sz, xbc, dt_raw, conv_w, conv_b, dt_bias, a_log, d_exp, norm_w, _head_spread_matrix())


def _decode_conv_body(s0_ref, s1_ref, s2_ref, xn_ref, cw_ref, cb_ref, o_ref):
    acc = cb_ref[...] + cw_ref[0:1, :] * s0_ref[...] + cw_ref[1:2, :] * s1_ref[...]
    acc = acc + cw_ref[2:3, :] * s2_ref[...] + cw_ref[3:4, :] * xn_ref[...]
    o_ref[...] = _silu(acc)


def decode_conv(s0, s1, s2, xnew, conv_w, conv_b):
    b, n = xnew.shape
    blk = pl.BlockSpec((b, CONV_SLAB), lambda j: (0, j))
    return pl.pallas_call(
        _decode_conv_body,
        grid=(n // CONV_SLAB,),
        in_specs=[blk, blk, blk, blk, pl.BlockSpec((CONV_WIDTH, CONV_SLAB), lambda j: (0, j)),
                  pl.BlockSpec((1, CONV_SLAB), lambda j: (0, j))],
        out_specs=blk,
        out_shape=jax.ShapeDtypeStruct((b, n), F32),
        compiler_params=_cparams(("arbitrary",)),
        name="decode_conv",
    )(s0, s1, s2, xnew, conv_w, conv_b)


def _decode_prep_body(xs_ref, b_ref, c_ref, dt_ref, dtb_ref, alog_ref, e_ref, dtx_ref, daexp_ref, bcx_ref):
    lane = lax.broadcasted_iota(jnp.int32, dt_ref.shape, 1)
    dt = jnp.where(lane < SSM_HEADS, _softplus(dt_ref[...] + dtb_ref[...]), 0.0)
    da = jnp.where(lane < SSM_HEADS, jnp.exp(dt * (-jnp.exp(alog_ref[...]))), 0.0)
    e = e_ref[...]
    expand = lambda v: sum(jnp.dot(p, e, preferred_element_type=F32) for p in _split3(v))
    dtx = expand(dt) * xs_ref[...]
    dtx_ref[...] = dtx
    daexp_ref[...] = expand(da)
    bcx_ref[...] = dtx * jnp.sum(b_ref[...] * c_ref[...], axis=-1, keepdims=True)


def decode_prep(xbc_act, dt_raw, dt_bias, a_log, expand_mat):
    b = xbc_act.shape[0]
    xblk = pl.BlockSpec((b, GROUP_X), lambda g: (0, g))
    vec = pl.BlockSpec((1, LANES), lambda g: (0, 0))
    nb = D_INNER // D_STATE
    out = jax.ShapeDtypeStruct((b, D_INNER), F32)
    return pl.pallas_call(
        _decode_prep_body,
        grid=(SSM_GROUPS,),
        in_specs=[
            xblk,
            pl.BlockSpec((b, D_STATE), lambda g: (0, nb + g)),
            pl.BlockSpec((b, D_STATE), lambda g: (0, nb + SSM_GROUPS + g)),
            pl.BlockSpec((b, LANES), lambda g: (0, 0)),
            vec, vec,
            pl.BlockSpec((LANES, GROUP_X), lambda g: (0, g)),
        ],
        out_specs=[xblk, xblk, xblk],
        out_shape=[out, out, out],
        compiler_params=_cparams(("arbitrary",)),
        name="decode_prep",
    )(xbc_act, xbc_act, xbc_act, dt_raw, dt_bias, a_log, expand_mat)


TILE_ROWS = 128
SPLIT_ROWS = 16


SEQS_PER_STEP = 2


STATE_ROWS = SSM_HEADS * SSM_HEAD_DIM
STATE_SEQS_PER_STEP = 4


def _decode_state_body(h_ref, dtx_ref, da_ref, b_ref, c_ref, ho_ref, hc_ref):
    rid = lambda w: lax.broadcasted_iota(jnp.int32, (SPLIT_ROWS, w), 0)
    pick = lambda r, pieces, w: functools.reduce(
        lambda acc, kv: jnp.where(r == kv[0], jnp.broadcast_to(kv[1].astype(F32), (SPLIT_ROWS, w)), acc),
        pieces, jnp.zeros((SPLIT_ROWS, w), F32))
    r = rid(D_INNER)
    rb = rid(SSM_GROUPS * D_STATE)
    r1 = rid(D_STATE)
    rd = jnp.where(jnp.logical_and(r1 >= 6, r1 <= 8), 1.0, 0.0).astype(BF16)
    for q in range(STATE_SEQS_PER_STEP):
        xh, xm, xl = _split3(dtx_ref[q])
        ah, am, al = _split3(da_ref[q])
        bh, bm, bl = _split3(b_ref[q])
        lmat = pick(r, [(0, xh), (1, xh), (2, xh), (3, xm), (4, xm), (5, xl), (6, ah), (7, am), (8, al)], D_INNER)
        ru = pick(rb, [(0, bh), (1, bm), (2, bl), (3, bh), (4, bm), (5, bh)], SSM_GROUPS * D_STATE).astype(BF16)
        crow = c_ref[q]
        for i in range(STATE_ROWS // TILE_ROWS):
            g = (i * TILE_ROWS) // GROUP_X
            lanes = slice(i * TILE_ROWS, (i + 1) * TILE_ROWS)
            lt = lmat[:, lanes].T.astype(BF16)
            upd = jnp.dot(lt, ru[:, g * D_STATE:(g + 1) * D_STATE], preferred_element_type=F32)
            dec = jnp.dot(lt, rd, preferred_element_type=F32)
            h = h_ref[q, lanes, :]
            ho_ref[q, lanes, :] = h * dec + upd
            t = (h * crow[:, g * D_STATE:(g + 1) * D_STATE]).T
            hc_ref[q, :, lanes] = jnp.sum(t, axis=0, keepdims=True)


def decode_state(h, dtx, daexp, xbc_act3):
    b = h.shape[0]
    nq = STATE_SEQS_PER_STEP
    hblk = pl.BlockSpec((nq, STATE_ROWS, D_STATE), lambda i: (i, 0, 0))
    rblk = pl.BlockSpec((nq, 1, D_INNER), lambda i: (i, 0, 0))
    bc = SSM_GROUPS * D_STATE
    return pl.pallas_call(
        _decode_state_body,
        grid=(b // nq,),
        in_specs=[
            hblk, rblk, rblk,
            pl.BlockSpec((nq, 1, bc), lambda i: (i, 0, D_INNER // bc)),
            pl.BlockSpec((nq, 1, bc), lambda i: (i, 0, D_INNER // bc + 1)),
        ],
        out_specs=[hblk, rblk],
        out_shape=[jax.ShapeDtypeStruct((b, STATE_ROWS, D_STATE), F32), jax.ShapeDtypeStruct((b, 1, D_INNER), F32)],
        compiler_params=_cparams(("arbitrary",)),
        name="decode_state",
    )(h, dtx, daexp, xbc_act3, xbc_act3)


def _decode_post_body(hc_ref, da_ref, bcx_ref, xs_ref, sz_ref, dexp_ref, nw_ref, o_ref):
    y = da_ref[...] * hc_ref[...] + bcx_ref[...] + dexp_ref[...] * xs_ref[...]
    gt = y * sz_ref[...]
    ms = jnp.mean(gt * gt, axis=-1, keepdims=True)
    o_ref[...] = (gt * lax.rsqrt(ms + RMS_EPS) * nw_ref[...]).astype(o_ref.dtype)


def decode_post(hc, daexp, bcx, xbc_act, sz, d_exp, norm_w):
    b = hc.shape[0]
    xblk = pl.BlockSpec((b, GROUP_X), lambda g: (0, g))
    vblk = pl.BlockSpec((1, GROUP_X), lambda g: (0, g))
    return pl.pallas_call(
        _decode_post_body,
        grid=(SSM_GROUPS,),
        in_specs=[xblk, xblk, xblk, xblk, xblk, vblk, vblk],
        out_specs=xblk,
        out_shape=jax.ShapeDtypeStruct((b, D_INNER), BF16),
        compiler_params=_cparams(("arbitrary",)),
        name="decode_post",
    )(hc, daexp, bcx, xbc_act, sz, d_exp, norm_w)


SUPER = WINDOW_MAX
ATT_N = 128
MERGE_ROWS = 256


def _rel_bucket(dist):
    max_exact = NUM_BUCKETS // 2
    df = jnp.maximum(dist, max_exact).astype(F32)
    large = max_exact + (jnp.log(df / max_exact) / math.log(MAX_DISTANCE / max_exact)
                         * (NUM_BUCKETS - max_exact)).astype(jnp.int32)
    return jnp.where(dist < max_exact, dist, jnp.minimum(large, NUM_BUCKETS - 1))


def _prompt_bias_rows(rel_bias):
    n = ATT_N
    back = n - jnp.arange(2 * n)
    rows = []
    for g, (_, dil) in enumerate(DILATION_GROUPS):
        tab = rel_bias[:, g * N_HEADS_B:(g + 1) * N_HEADS_B]
        bias = tab[_rel_bucket(jnp.clip(back, 0, n) * dil)].astype(F32) * LOG2E
        rows.append(jnp.where((back >= 0)[:, None], bias, NEG_BIG).T)
    return jnp.stack(rows, axis=0)[:, :, None, :]


def _rows(start, size, stride):
    return pl.ds(start, size) if stride == 1 else pl.ds(start, size, stride=stride)


def _attn_prompt_body(q0_ref, q1_ref, q2_ref, kc_ref, kp_ref, vc_ref, vp_ref, t_ref, o_ref, og_ref, lse_ref):
    first = pl.program_id(0) == 0
    n = ATT_N
    scale = HEAD_DIM ** -0.5 * LOG2E
    key_id = lax.broadcasted_iota(jnp.int32, (n, 2 * n), 1)
    q_refs = (q0_ref, q1_ref, q2_ref)
    for g, (win, dil) in enumerate(DILATION_GROUPS):
        nblk = SUPER // (n * dil)
        bias = pltpu.roll(jnp.broadcast_to(t_ref[g, 0], (n, 2 * n)), 0, 1, stride=1, stride_axis=0)
        for r in range(dil):
            prev = _rows((nblk - 1) * n * dil + r, n, dil)
            kprev, vprev = kp_ref[prev, :].astype(BF16), vp_ref[prev, :].astype(BF16)
            for c in range(nblk):
                start = c * n * dil + r
                cur = _rows(start, n, dil)
                qb = q_refs[g][cur, :].astype(BF16)
                kcur, vcur = kc_ref[cur, :].astype(BF16), vc_ref[cur, :].astype(BF16)
                keys = jnp.concatenate([kprev, kcur], axis=0)
                vals = jnp.concatenate([vprev, vcur], axis=0)
                kprev, vprev = kcur, vcur
                s = _dot_nt(qb, keys) * scale + bias
                if c == 0:
                    s = jnp.where(jnp.logical_and(first, key_id < n), NEG_BIG, s)
                m = jnp.max(s, axis=-1, keepdims=True)
                p = jnp.exp2(s - m)
                den = jnp.sum(p, axis=-1, keepdims=True)
                o = jnp.dot(p.astype(BF16), vals, preferred_element_type=F32) * (1.0 / den)
                og_ref[g, cur, :] = o
                lse_ref[g, cur, :] = jnp.broadcast_to(m + jnp.log2(den), (n, LANES))
    for i in range(SUPER // MERGE_ROWS):
        rows = slice(i * MERGE_ROWS, (i + 1) * MERGE_ROWS)
        ls = [lse_ref[g, rows, :] for g in range(N_GROUPS_B)]
        mx = functools.reduce(jnp.maximum, ls)
        es = [jnp.exp2(l - mx) for l in ls]
        tot = functools.reduce(lambda a, b: a + b, es)
        acc = functools.reduce(lambda a, b: a + b, [es[g] * og_ref[g, rows, :] for g in range(N_GROUPS_B)])
        o_ref[rows, :] = (acc * (1.0 / tot)).astype(o_ref.dtype)


def attn_prompt(q, kv, bias_rows):
    s = q.shape[0]
    blk = lambda f: pl.BlockSpec((SUPER, HEAD_DIM), f)
    return pl.pallas_call(
        _attn_prompt_body,
        grid=(s // SUPER, N_HEADS_B),
        in_specs=[
            blk(lambda c, h: (c, h)),
            blk(lambda c, h: (c, N_HEADS_B + h)),
            blk(lambda c, h: (c, 2 * N_HEADS_B + h)),
            blk(lambda c, h: (c, h)),
            blk(lambda c, h: (jnp.maximum(c - 1, 0), h)),
            blk(lambda c, h: (c, N_HEADS_B + h)),
            blk(lambda c, h: (jnp.maximum(c - 1, 0), N_HEADS_B + h)),
            pl.BlockSpec((N_GROUPS_B, 1, 1, 2 * ATT_N), lambda c, h: (0, h, 0, 0)),
        ],
        out_specs=blk(lambda c, h: (c, h)),
        out_shape=jax.ShapeDtypeStruct((s, KV_DIM), BF16),
        scratch_shapes=[
            pltpu.VMEM((N_GROUPS_B, SUPER, HEAD_DIM), F32),
            pltpu.VMEM((N_GROUPS_B, SUPER, LANES), F32),
        ],
        compiler_params=_cparams(("arbitrary", "arbitrary")),
        name="attn_prompt",
    )(q, q, q, kv, kv, kv, kv, bias_rows)


N_BUF = min(WINDOW_MAX, PAST_LEN)
ROW_MAJOR = 16
ROW_MID = 4


def _decode_bias_tables(rel_bias):
    n = ATT_N
    j = n - jnp.arange(n)
    cached, new = [], []
    for g, (_, dil) in enumerate(DILATION_GROUPS):
        tab = rel_bias[:, g * N_HEADS_B:(g + 1) * N_HEADS_B]
        cached.append(tab[_rel_bucket(j * dil)])
        new.append(tab[_rel_bucket(jnp.zeros((), jnp.int32))])
    cached = jnp.broadcast_to(jnp.stack(cached)[..., None], (N_GROUPS_B, n, N_HEADS_B, LANES)).astype(F32)
    new = jnp.broadcast_to(jnp.stack(new)[..., None], (N_GROUPS_B, N_HEADS_B, LANES)).astype(F32)
    return cached, new


def _attn_decode_body(q_ref, kvn_ref, k0_ref, v0_ref, k1_ref, v1_ref, k2_ref, v2_ref, bc_ref, bn_ref, o_ref):
    n = ATT_N
    scale = HEAD_DIM ** -0.5
    kv_refs = ((k0_ref, v0_ref), (k1_ref, v1_ref), (k2_ref, v2_ref))
    for b in range(SEQS_PER_STEP):
        kn, vn = kvn_ref[b, 0], kvn_ref[b, 1]
        outs, lses = [], []
        for g in range(N_GROUPS_B):
            q = q_ref[b, g] * scale
            k = kv_refs[g][0][b].reshape(n, N_HEADS_B, HEAD_DIM)
            v = kv_refs[g][1][b].reshape(n, N_HEADS_B, HEAD_DIM)
            s = jnp.sum(k * q[None], axis=-1, keepdims=True) + bc_ref[g]
            s0 = jnp.sum(kn * q, axis=-1, keepdims=True) + bn_ref[g]
            m = jnp.maximum(jnp.max(s, axis=0), s0)
            p = jnp.exp(s - m[None])
            p0 = jnp.exp(s0 - m)
            den = jnp.sum(p, axis=0) + p0
            outs.append((jnp.sum(p * v, axis=0) + p0 * vn) * (1.0 / den))
            lses.append(m + jnp.log(den))
        mx = functools.reduce(jnp.maximum, lses)
        es = [jnp.exp(l - mx) for l in lses]
        tot = functools.reduce(lambda x, y: x + y, es)
        acc = functools.reduce(lambda x, y: x + y, [e * o for e, o in zip(es, outs)])
        o_ref[b] = (acc * (1.0 / tot)).astype(o_ref.dtype)


def attn_decode(q, kv_new, cache_k, cache_v, bias_cached, bias_new):
    b = q.shape[0]
    view = (b, N_BUF // ROW_MAJOR, ROW_MAJOR // ROW_MID, ROW_MID, N_HEADS_B, HEAD_DIM)
    ck, cv = cache_k.reshape(view), cache_v.reshape(view)
    n = ATT_N
    nq = SEQS_PER_STEP
    hd = (N_HEADS_B, HEAD_DIM)
    g0 = pl.BlockSpec((nq, n // ROW_MAJOR, ROW_MAJOR // ROW_MID, ROW_MID) + hd,
                      lambda i: (i, N_BUF // n - 1, 0, 0, 0, 0))
    g1 = pl.BlockSpec((nq, n * ROW_MID // ROW_MAJOR, ROW_MAJOR // ROW_MID, None) + hd,
                      lambda i: (i, N_BUF // (n * ROW_MID) - 1, 0, 0, 0, 0))
    g2 = pl.BlockSpec((nq, n, None, None) + hd, lambda i: (i, 0, 0, 0, 0, 0))
    return pl.pallas_call(
        _attn_decode_body,
        grid=(b // nq,),
        in_specs=[
            pl.BlockSpec((nq, N_GROUPS_B) + hd, lambda i: (i, 0, 0, 0)),
            pl.BlockSpec((nq, 2) + hd, lambda i: (i, 0, 0, 0)),
            g0, g0, g1, g1, g2, g2,
            pl.BlockSpec((N_GROUPS_B, n) + (N_HEADS_B, LANES), lambda i: (0, 0, 0, 0)),
            pl.BlockSpec((N_GROUPS_B, N_HEADS_B, LANES), lambda i: (0, 0, 0)),
        ],
        out_specs=pl.BlockSpec((nq,) + hd, lambda i: (i, 0, 0)),
        out_shape=jax.ShapeDtypeStruct((b,) + hd, BF16),
        compiler_params=_cparams(("arbitrary",)),
        name="attn_decode",
    )(q, kv_new, ck, cv, ck, cv, ck, cv, bias_cached, bias_new)


ADA_ROWS = 136
Z_TILE = 1024


def _pad_lanes(v):
    return jnp.pad(v.astype(F32), (0, LANES - v.shape[0]))[None]


TM = 1024
TM_LONG_K = 512
TM_GLU = 2048
TM_EW = 256


def _trunk(xs, adas, mixers_a, mixers_b, p):
    tms = (TM_EW, xs[1].shape[0])
    vec = lambda a, l, s: a[l, s][None]
    sh1, sc1, g1, sh2, sc2, g2 = zip(*(a[0] for a in adas))
    sh1b, sc1b, g1b, sh2b, sc2b, g2b = zip(*(a[1] for a in adas))
    shkv, sckv = zip(*(a[2] for a in adas))
    paths = range(2)

    def ln(x, y, gate, l, s, mods):
        outs = [post_ln(x[i], y[i], gate[i], vec(p["ln_g"], l, s), vec(p["ln_b"], l, s),
                        [(sc[i], sh[i]) for sc, sh in mods], tm=tms[i]) for i in paths]
        return zip(*outs)

    def mm(a, w, **kw):
        tm = kw.pop("tm", TM)
        return matmul(a[0], w, rider=a[1], tm=tm, **kw)

    u = [modulate(xs[i], sc1[i], sh1[i], tm=tms[i]) for i in paths]
    w_in_t = jnp.swapaxes(p["ssm_w_in"], 1, 2).reshape(-1, D_MODEL)
    sz = mm(u, w_in_t, w_transposed=True, out_act="silu", tn=Z_TILE, n_out=D_INNER, name="in_z")
    xbc = mm(u, w_in_t, w_transposed=True, tn=Z_TILE, n_out=CONV_DIM, col_off=D_INNER // Z_TILE, name="in_xbc")
    dt = mm(u, w_in_t, w_transposed=True, tn=LANES, n_out=LANES, col_off=(D_INNER + CONV_DIM) // LANES, name="in_dt")
    gated, extras = zip(*(mixers_a[i](sz[i], xbc[i], dt[i]) for i in paths))
    y = mm(gated, p["ssm_w_out"].reshape(D_INNER, D_MODEL), tn=512, out_dtype=BF16, name="ssm_out")
    x, u = ln(xs, y, g1, 0, 0, [(sc2, sh2)])
    h = glu_matmul(u[0], u[1], p["ffn_w_gate"], p["ffn_w_up"], layer=0, tm=TM_GLU, tn=512, name="ffn0_glu")
    y = mm(h, p["ffn_w_down"], layer=0, tm=TM_LONG_K, tn=512, out_dtype=BF16, name="ffn0_down")
    x, u, ukv = ln(x, y, g2, 0, 1, [(sc1b, sh1b), (sckv, shkv)])

    kv = mm(ukv, p["w_kv"], tn=1024, name="kv_proj")
    q = mm(u, p["attn_w_q"], layer=0, tn=1024, name="q_proj")
    o = [mixers_b[i](q[i], kv[i]) for i in paths]
    y = mm(o, p["attn_w_o"], layer=0, tn=1024, out_dtype=BF16, name="attn_out")
    x, u = ln(x, y, g1b, 1, 0, [(sc2b, sh2b)])
    h = glu_matmul(u[0], u[1], p["ffn_w_gate"], p["ffn_w_up"], layer=1, tm=TM_GLU, tn=512, name="ffn1_glu")
    y = mm(h, p["ffn_w_down"], layer=1, tm=TM_LONG_K, tn=512, out_dtype=BF16, name="ffn1_down")
    (x,) = ln(x, y, g2b, 1, 1, [])
    return [(x[i], extras[i], kv[i]) for i in paths]


def kernel(x_prompt, x_sample, state_conv, state_ssm, cache_k, cache_v, c_prompt, c_sample, w_ada, b_ada, ln_g, ln_b, ffn_w_gate, ffn_w_up, ffn_w_down, ssm_w_in, ssm_conv_w, ssm_conv_b, ssm_dt_bias, ssm_a_log, ssm_d, ssm_norm_w, ssm_w_out, kv_w_ada, kv_b_ada, w_kv, attn_w_q, attn_w_o, rel_bias):
    p = dict(ln_g=ln_g, ln_b=ln_b, ffn_w_gate=ffn_w_gate, ffn_w_up=ffn_w_up, ffn_w_down=ffn_w_down,
             ssm_w_in=ssm_w_in, ssm_w_out=ssm_w_out, w_kv=w_kv, attn_w_q=attn_w_q, attn_w_o=attn_w_o)
    s, b, d = SEQ, DEC_BATCH, D_MODEL

    c_all = jnp.concatenate([c_sample, c_prompt, jnp.zeros((ADA_ROWS - b - 1, d), F32)], axis=0)
    ada = [matmul(c_all, w_ada, layer=l, tm=ADA_ROWS, tn=1024, bias=b_ada[l][None], act="silu", name=f"ada{l}")
           for l in range(DEPTH)]
    adakv = matmul(c_all, kv_w_ada, tm=ADA_ROWS, tn=1024, bias=kv_b_ada[None], act="silu", name="ada_kv")
    split = lambda m, n, rows: [m[rows, i * d:(i + 1) * d] for i in range(n)]
    smp, prm = slice(0, b), slice(b, b + 1)

    conv_w, conv_b = ssm_conv_w[0], ssm_conv_b[0][None]
    dt_bias, a_log = _pad_lanes(ssm_dt_bias[0]), _pad_lanes(ssm_a_log[0])
    d_exp = jnp.repeat(ssm_d[0].astype(F32), SSM_HEAD_DIM)[None]
    norm_w = ssm_norm_w[0][None]

    def mixer_a_prompt(sz, xbc, dt):
        gated, hfin = ssd_prompt(sz, xbc, dt, conv_w, conv_b, dt_bias, a_log, d_exp, norm_w)
        return gated, (xbc[s - (CONV_WIDTH - 1):], hfin)

    bias_rows = _prompt_bias_rows(rel_bias)
    mixer_b_prompt = lambda q, kv: attn_prompt(q, kv, bias_rows)

    expand_mat = (lax.broadcasted_iota(jnp.int32, (LANES, D_INNER), 0)
                  == lax.broadcasted_iota(jnp.int32, (LANES, D_INNER), 1) // SSM_HEAD_DIM).astype(BF16)

    def mixer_a_decode(sz, xbc, dt):
        st = state_conv[0]
        xa = decode_conv(st[:, 0], st[:, 1], st[:, 2], xbc, conv_w, conv_b)
        dtx, daexp, bcx = decode_prep(xa, dt, dt_bias, a_log, expand_mat)
        hnew, hc = decode_state(state_ssm.reshape(b, STATE_ROWS, D_STATE), dtx[:, None], daexp[:, None], xa[:, None])
        gated = decode_post(hc[:, 0], daexp, bcx, xa, sz, d_exp, norm_w)
        new_conv = jnp.concatenate([st[:, 1:], xbc[:, None]], axis=1)
        return gated, (new_conv, hnew)

    bias_cached, bias_new = _decode_bias_tables(rel_bias)

    def mixer_b_decode(q, kv):
        o = attn_decode(q.reshape(b, N_GROUPS_B, N_HEADS_B, HEAD_DIM), kv.reshape(b, 2, N_HEADS_B, HEAD_DIM),
                        cache_k, cache_v, bias_cached, bias_new)
        return o.reshape(b, KV_DIM)

    adas = [(split(ada[0], 6, rows), split(ada[1], 6, rows), split(adakv, 2, rows)) for rows in (prm, smp)]
    (y_p, (conv_p, ssm_p), kv_p), (y_s, (conv_s, ssm_s), kv_s) = _trunk(
        (x_prompt.reshape(s, d), x_sample.reshape(b, d)), adas,
        (mixer_a_prompt, mixer_a_decode), (mixer_b_prompt, mixer_b_decode), p)

    n_keep = min(WINDOW_MAX, s)
    heads = (N_HEADS_B, HEAD_DIM)
    return (
        y_p.reshape(1, s, d),
        y_s.reshape(b, 1, d),
        conv_p.reshape(1, 1, CONV_WIDTH - 1, CONV_DIM),
        ssm_p.reshape(1, 1, SSM_HEADS, SSM_HEAD_DIM, D_STATE),
        kv_p[s - n_keep:, :KV_DIM].reshape((1, n_keep) + heads),
        kv_p[s - n_keep:, KV_DIM:].reshape((1, n_keep) + heads),
        conv_s.reshape(1, b, CONV_WIDTH - 1, CONV_DIM),
        ssm_s.reshape(1, b, SSM_HEADS, SSM_HEAD_DIM, D_STATE),
        kv_s[:, :KV_DIM].reshape((b, 1) + heads),
        kv_s[:, KV_DIM:].reshape((b, 1) + heads),
    )
```

```python
import functools
import math

import jax
import jax.numpy as jnp
from jax import lax
from jax.experimental import pallas as pl
from jax.experimental.pallas import tpu as pltpu

F32 = jnp.float32
BF16 = jnp.bfloat16

D_MODEL = 2048
SEQ = 8192
DEPTH = 2
DEC_BATCH = 128
PAST_LEN = 2048
D_INNER = 2 * D_MODEL
SSM_HEAD_DIM = 64
SSM_HEADS = D_INNER // SSM_HEAD_DIM
SSM_GROUPS = 8
HEADS_PER_GROUP = SSM_HEADS // SSM_GROUPS
D_STATE = 128
CONV_WIDTH = 4
CONV_DIM = D_INNER + 2 * SSM_GROUPS * D_STATE
SSD_CHUNK = 128
RMS_EPS = 1e-5
HEAD_DIM = 128
N_HEADS_B = D_MODEL // HEAD_DIM
DILATION_GROUPS = ((128, 1), (512, 4), (2048, 16))
N_GROUPS_B = len(DILATION_GROUPS)
WINDOW_MAX = max(w for w, _ in DILATION_GROUPS)
Q_DIM = N_GROUPS_B * N_HEADS_B * HEAD_DIM
KV_DIM = N_HEADS_B * HEAD_DIM
NUM_BUCKETS = 32
MAX_DISTANCE = WINDOW_MAX
D_FF = -(-8 * D_MODEL // (3 * 256)) * 256
ALPHA = (2 * DEPTH) ** 0.25
LN_EPS = 1e-5

LANES = 128
SUBLANES = 8
VMEM_LIMIT_BYTES = 56 * 1024 * 1024
NEG_BIG = -1e30
LOG2E = math.log2(math.e)


def _cparams(semantics):
    return pltpu.CompilerParams(dimension_semantics=semantics, vmem_limit_bytes=VMEM_LIMIT_BYTES)


def _silu(x):
    return x * (1.0 / (1.0 + jnp.exp2(x * -LOG2E)))


def _split3(x):
    hi = x.astype(BF16)
    r1 = x - hi.astype(F32)
    mid = r1.astype(BF16)
    lo = (r1 - mid.astype(F32)).astype(BF16)
    return hi, mid, lo


CAST_ROWS = 256


def _cast_tile(w_ref, wbf_ref, transposed=False):
    k, tn = wbf_ref.shape
    if transposed:
        step = min(CAST_ROWS, tn)
        for c in range(tn // step):
            wbf_ref[:, c * step:(c + 1) * step] = w_ref[c * step:(c + 1) * step, :].T.astype(BF16)
        return
    rows = CAST_ROWS if k % CAST_ROWS == 0 else k
    def body(r, c):
        off = pl.multiple_of(r * rows, rows)
        wbf_ref[pl.ds(off, rows), :] = w_ref[pl.ds(off, rows), :].astype(BF16)
        return c
    lax.fori_loop(0, k // rows, body, 0)


def _mm_body(*refs, act, out_act, has_bias, has_rider, w_transposed):
    refs = list(refs)
    a_ref, w_ref = refs.pop(0), refs.pop(0)
    b_ref = refs.pop(0) if has_bias else None
    a2_ref = refs.pop(0) if has_rider else None
    o_ref = refs.pop(0)
    o2_ref = refs.pop(0) if has_rider else None
    wbf_ref = refs.pop(0)

    def product(lhs_ref):
        a = lhs_ref[...]
        if act == "silu":
            a = _silu(a.astype(F32))
        acc = jnp.dot(a.astype(BF16), wbf_ref[...], preferred_element_type=F32)
        if has_bias:
            acc = acc + b_ref[...]
        return _silu(acc) if out_act == "silu" else acc

    @pl.when(pl.program_id(1) == 0)
    def _():
        _cast_tile(w_ref, wbf_ref, w_transposed)
        if has_rider:
            o2_ref[...] = product(a2_ref).astype(o2_ref.dtype)

    o_ref[...] = product(a_ref).astype(o_ref.dtype)


def _weight_spec(w, layer, tn, col_off=0, transposed=False):
    if transposed:
        return pl.BlockSpec((tn, w.shape[1]), lambda j, i: (j + col_off, 0))
    k = w.shape[-2]
    if w.ndim == 2:
        return pl.BlockSpec((k, tn), lambda j, i: (0, j + col_off))
    return pl.BlockSpec((None, k, tn), lambda j, i: (layer, 0, j + col_off))


def matmul(a, w, *, tm, tn, layer=0, n_out=None, col_off=0, bias=None, act=None, out_act=None, rider=None,
           w_transposed=False, out_dtype=F32, name="mm"):
    m, k = a.shape
    n_out = (w.shape[0] if w_transposed else w.shape[-1]) if n_out is None else n_out
    grid = (pl.cdiv(n_out, tn), m // tm)
    in_specs = [
        pl.BlockSpec((tm, k), lambda j, i: (i, 0)),
        _weight_spec(w, layer, tn, col_off, w_transposed),
    ]
    args = [a, w]
    out_specs = [pl.BlockSpec((tm, tn), lambda j, i: (i, j))]
    out_shape = [jax.ShapeDtypeStruct((m, n_out), out_dtype)]
    if bias is not None:
        in_specs.append(pl.BlockSpec((1, tn), lambda j, i: (0, j + col_off)))
        args.append(bias)
    if rider is not None:
        m2 = rider.shape[0]
        in_specs.append(pl.BlockSpec((m2, k), lambda j, i: (0, 0)))
        args.append(rider)
        out_specs.append(pl.BlockSpec((m2, tn), lambda j, i: (0, j)))
        out_shape.append(jax.ShapeDtypeStruct((m2, n_out), out_dtype))
    outs = pl.pallas_call(
        functools.partial(_mm_body, act=act, out_act=out_act, has_bias=bias is not None,
                          has_rider=rider is not None, w_transposed=w_transposed),
        grid=grid,
        in_specs=in_specs,
        out_specs=out_specs,
        out_shape=out_shape,
        scratch_shapes=[pltpu.VMEM((k, tn), BF16)],
        compiler_params=_cparams(("arbitrary", "arbitrary")),
        name=name,
    )(*args)
    return outs if rider is not None else outs[0]


def _glu_body(a_ref, wg_ref, wu_ref, a2_ref, o_ref, o2_ref, wgbf_ref, wubf_ref):
    def product(lhs_ref):
        a = lhs_ref[...]
        g = jnp.dot(a, wgbf_ref[...], preferred_element_type=F32)
        u = jnp.dot(a, wubf_ref[...], preferred_element_type=F32)
        return (_silu(g) * u).astype(BF16)

    @pl.when(pl.program_id(1) == 0)
    def _():
        _cast_tile(wg_ref, wgbf_ref)
        _cast_tile(wu_ref, wubf_ref)
        o2_ref[...] = product(a2_ref)

    o_ref[...] = product(a_ref)


def glu_matmul(a, rider, w_gate, w_up, *, tm, tn, layer=0, name="glu"):
    m, k = a.shape
    m2 = rider.shape[0]
    n = w_gate.shape[-1]
    return pl.pallas_call(
        _glu_body,
        grid=(n // tn, m // tm),
        in_specs=[
            pl.BlockSpec((tm, k), lambda j, i: (i, 0)),
            _weight_spec(w_gate, layer, tn),
            _weight_spec(w_up, layer, tn),
            pl.BlockSpec((m2, k), lambda j, i: (0, 0)),
        ],
        out_specs=[pl.BlockSpec((tm, tn), lambda j, i: (i, j)), pl.BlockSpec((m2, tn), lambda j, i: (0, j))],
        out_shape=[jax.ShapeDtypeStruct((m, n), BF16), jax.ShapeDtypeStruct((m2, n), BF16)],
        scratch_shapes=[pltpu.VMEM((k, tn), BF16), pltpu.VMEM((k, tn), BF16)],
        compiler_params=_cparams(("arbitrary", "arbitrary")),
        name=name,
    )(a, w_gate, w_up, rider)


def _row_spec(arr, tm):
    if arr.shape[0] == 1:
        return pl.BlockSpec((1, arr.shape[1]), lambda i: (0, 0))
    return pl.BlockSpec((tm, arr.shape[1]), lambda i: (i, 0))


def _modulate_body(x_ref, sc_ref, sh_ref, o_ref):
    o_ref[...] = (x_ref[...] * (1.0 + sc_ref[...]) + sh_ref[...]).astype(o_ref.dtype)


def modulate(x, scale, shift, *, tm):
    m, d = x.shape
    return pl.pallas_call(
        _modulate_body,
        grid=(m // tm,),
        in_specs=[pl.BlockSpec((tm, d), lambda i: (i, 0)), _row_spec(scale, tm), _row_spec(shift, tm)],
        out_specs=pl.BlockSpec((tm, d), lambda i: (i, 0)),
        out_shape=jax.ShapeDtypeStruct((m, d), BF16),
        compiler_params=_cparams(("arbitrary",)),
        name="modulate",
    )(x, scale, shift)


def _ln_body(*refs, n_mod):
    x_ref, y_ref, gate_ref, g_ref, b_ref = refs[:5]
    mod_refs = refs[5:5 + 2 * n_mod]
    xo_ref = refs[5 + 2 * n_mod]
    u_refs = refs[6 + 2 * n_mod:]
    t = ALPHA * x_ref[...] + (1.0 + gate_ref[...]) * y_ref[...].astype(F32)
    mu = jnp.mean(t, axis=-1, keepdims=True)
    tc = t - mu
    var = jnp.mean(tc * tc, axis=-1, keepdims=True)
    xn = tc * lax.rsqrt(var + LN_EPS) * g_ref[...] + b_ref[...]
    xo_ref[...] = xn
    for q in range(n_mod):
        u_refs[q][...] = (xn * (1.0 + mod_refs[2 * q][...]) + mod_refs[2 * q + 1][...]).astype(BF16)


def post_ln(x, y, gate, ln_g, ln_b, mods, *, tm):
    m, d = x.shape
    row = pl.BlockSpec((tm, d), lambda i: (i, 0))
    vec = pl.BlockSpec((1, d), lambda i: (0, 0))
    in_specs = [row, row, _row_spec(gate, tm), vec, vec]
    args = [x, y, gate, ln_g, ln_b]
    for sc, sh in mods:
        in_specs += [_row_spec(sc, tm), _row_spec(sh, tm)]
        args += [sc, sh]
    outs = pl.pallas_call(
        functools.partial(_ln_body, n_mod=len(mods)),
        grid=(m // tm,),
        in_specs=in_specs,
        out_specs=[row] * (1 + len(mods)),
        out_shape=[jax.ShapeDtypeStruct((m, d), F32)] + [jax.ShapeDtypeStruct((m, d), BF16)] * len(mods),
        compiler_params=_cparams(("arbitrary",)),
        name="post_ln",
    )(*args)
    return outs


CONV_SLAB = 512
GROUP_X = D_INNER // SSM_GROUPS
PAIR = 2 * SSM_HEAD_DIM


def _softplus(x):
    return jnp.maximum(x, 0.0) + jnp.log1p(jnp.exp(-jnp.abs(x)))


def _dot_nt(a, b):
    return lax.dot_general(a, b, (((1,), (1,)), ((), ())), preferred_element_type=F32)


def _dot_tn(a, b):
    return lax.dot_general(a, b, (((0,), (0,)), ((), ())), preferred_element_type=F32)


def _head_spread_matrix():
    r = lax.broadcasted_iota(jnp.int32, (SSM_GROUPS, 2 * LANES, HEADS_PER_GROUP * LANES), 1)
    col = lax.broadcasted_iota(jnp.int32, (SSM_GROUPS, 2 * LANES, HEADS_PER_GROUP * LANES), 2)
    g = lax.broadcasted_iota(jnp.int32, (SSM_GROUPS, 2 * LANES, HEADS_PER_GROUP * LANES), 0)
    hit = jnp.logical_and(r % SSM_HEADS == g * HEADS_PER_GROUP + col // LANES, r < 3 * SSM_HEADS)
    return hit.astype(BF16)


def _ssd_prompt_body(sz_ref, xbc_ref, dt_ref, cw_ref, cb_ref, dtb_ref, alog_ref, dexp_ref, nw_ref, e_ref,
                     g_ref, hfin_ref,
                     ext_ref, act_ref, statet_ref, p_ref, wt_ref, rowp_ref):
    c = pl.program_id(0)
    L = SSD_CHUNK

    @pl.when(c == 0)
    def _():
        ext_ref[0:SUBLANES, :] = jnp.zeros((SUBLANES, CONV_DIM), F32)
        statet_ref[...] = jnp.zeros(statet_ref.shape, F32)

    ext_ref[SUBLANES:SUBLANES + L, :] = xbc_ref[...]
    for s in range(CONV_DIM // LANES):
        cols = slice(s * LANES, (s + 1) * LANES)
        acc = cb_ref[:, cols]
        for i in range(CONV_WIDTH):
            lo = SUBLANES - (CONV_WIDTH - 1) + i
            acc = acc + cw_ref[i:i + 1, cols] * ext_ref[lo:lo + L, cols]
        act_ref[:, cols] = _silu(acc)
    ext_ref[0:SUBLANES, :] = ext_ref[L:L + SUBLANES, :]

    lane = lax.broadcasted_iota(jnp.int32, (L, LANES), 1)
    row = lax.broadcasted_iota(jnp.int32, (L, LANES), 0)
    head_ok = lane < SSM_HEADS
    dt = jnp.where(head_ok, _softplus(dt_ref[...] + dtb_ref[...]), 0.0)
    da = dt * (-jnp.exp(alog_ref[...]))
    causal = row >= lane
    tril = jnp.where(causal, 1.0, 0.0).astype(BF16)
    cs = sum(jnp.dot(tril, p, preferred_element_type=F32) for p in _split3(da)) * LOG2E
    hi, mid, lo = _split3(cs)
    p_ref[:, 0:LANES] = jnp.where(head_ok, hi.astype(F32), pltpu.roll(mid.astype(F32), SSM_HEADS, 1)).astype(BF16)
    p_ref[:, LANES:2 * LANES] = lo
    cst = cs.T
    dtt = dt.T
    rowp_ref[...] = cst - jnp.log2(dtt)
    wt_ref[...] = dtt * jnp.exp2(jnp.broadcast_to(cst[:, L - 1:L], (LANES, L)) - cst)
    lane_lo = lane < SSM_HEAD_DIM

    b_off, c_off = D_INNER, D_INNER + SSM_GROUPS * D_STATE
    for g in range(SSM_GROUPS):
        bg = act_ref[:, b_off + g * D_STATE:b_off + (g + 1) * D_STATE]
        bg_bf = bg.astype(BF16)
        bgt = bg.T
        cg = act_ref[:, c_off + g * D_STATE:c_off + (g + 1) * D_STATE].astype(BF16)
        cb = _dot_nt(cg, bg_bf)
        yoff = jnp.dot(cg, statet_ref[g].astype(BF16), preferred_element_type=F32)
        colb = jnp.dot(p_ref[...], e_ref[g], preferred_element_type=F32)
        heads = slice(g * HEADS_PER_GROUP, (g + 1) * HEADS_PER_GROUP)
        rowpg, wtg = rowp_ref[heads, :], wt_ref[heads, :]
        gated = []
        for pr in range(HEADS_PER_GROUP // 2):
            lanes = slice(pr * PAIR, (pr + 1) * PAIR)
            cols = slice(g * GROUP_X + pr * PAIR, g * GROUP_X + (pr + 1) * PAIR)
            x2 = act_ref[:, cols]
            ms, ecols, ss = [], [], []
            for k in (2 * pr, 2 * pr + 1):
                col = colb[:, k * LANES:(k + 1) * LANES]
                rowv = jnp.broadcast_to(rowpg[k:k + 1, :], (L, LANES))
                lmat = jnp.exp2(jnp.where(causal, col - rowv, NEG_BIG))
                ms.append((cb * lmat).astype(BF16))
                ecols.append(jnp.exp2(col))
                ss.append((bgt * jnp.broadcast_to(wtg[k:k + 1, :], (LANES, L))).astype(BF16))
            rhs = jnp.concatenate([jnp.where(lane_lo, x2, 0.0), jnp.where(lane_lo, 0.0, x2)], axis=0).astype(BF16)
            ydiag = jnp.dot(jnp.concatenate(ms, axis=1), rhs, preferred_element_type=F32)
            escale = jnp.where(lane_lo, ecols[0], ecols[1])
            y = ydiag + yoff[:, lanes] * escale + dexp_ref[:, cols] * x2
            gated.append(y * sz_ref[:, cols])
            snew = jnp.dot(jnp.concatenate(ss, axis=1), rhs, preferred_element_type=F32)
            decay = jnp.broadcast_to(escale[L - 1:L, :], (LANES, LANES))
            statet_ref[g, :, lanes] = statet_ref[g, :, lanes] * decay + snew
        sq = functools.reduce(lambda a, b: a + b, [t * t for t in gated])
        inv = lax.rsqrt(jnp.sum(sq, axis=-1, keepdims=True) * (1.0 / GROUP_X) + RMS_EPS)
        for pr, t in enumerate(gated):
            cols = slice(g * GROUP_X + pr * PAIR, g * GROUP_X + (pr + 1) * PAIR)
            g_ref[:, cols] = (t * inv * nw_ref[:, cols]).astype(g_ref.dtype)

    @pl.when(c == pl.num_programs(0) - 1)
    def _():
        for g in range(SSM_GROUPS):
            for q in range(GROUP_X // LANES):
                blk = slice(q * LANES, (q + 1) * LANES)
                hfin_ref[g, blk, :] = statet_ref[g, :, blk].T


def ssd_prompt(sz, xbc, dt_raw, conv_w, conv_b, dt_bias, a_log, d_exp, norm_w):
    s = sz.shape[0]
    L = SSD_CHUNK
    vec = lambda n: pl.BlockSpec((1, n), lambda c: (0, 0))
    return pl.pallas_call(
        _ssd_prompt_body,
        grid=(s // L,),
        in_specs=[
            pl.BlockSpec((L, D_INNER), lambda c: (c, 0)),
            pl.BlockSpec((L, CONV_DIM), lambda c: (c, 0)),
            pl.BlockSpec((L, LANES), lambda c: (c, 0)),
            pl.BlockSpec((CONV_WIDTH, CONV_DIM), lambda c: (0, 0)),
            vec(CONV_DIM), vec(LANES), vec(LANES), vec(D_INNER), vec(D_INNER),
            pl.BlockSpec((SSM_GROUPS, 2 * LANES, HEADS_PER_GROUP * LANES), lambda c: (0, 0, 0)),
        ],
        out_specs=[
            pl.BlockSpec((L, D_INNER), lambda c: (c, 0)),
            pl.BlockSpec((SSM_GROUPS, GROUP_X, D_STATE), lambda c: (0, 0, 0)),
        ],
        out_shape=[
            jax.ShapeDtypeStruct((s, D_INNER), BF16),
            jax.ShapeDtypeStruct((SSM_GROUPS, GROUP_X, D_STATE), F32),
        ],
        scratch_shapes=[
            pltpu.VMEM((L + 2 * SUBLANES, CONV_DIM), F32),
            pltpu.VMEM((L, CONV_DIM), F32),
            pltpu.VMEM((SSM_GROUPS, D_STATE, GROUP_X), F32),
            pltpu.VMEM((L, 2 * LANES), BF16),
            pltpu.VMEM((LANES, L), F32),
            pltpu.VMEM((LANES, L), F32),
        ],
        compiler_params=_cparams(("arbitrary",)),
        name="ssd_prompt",
    )(sz, xbc, dt_raw, conv_w, conv_b, dt_bias, a_log, d_exp, norm_w, _head_spread_matrix())


def _decode_conv_body(s0_ref, s1_ref, s2_ref, xn_ref, cw_ref, cb_ref, o_ref):
    acc = cb_ref[...] + cw_ref[0:1, :] * s0_ref[...] + cw_ref[1:2, :] * s1_ref[...]
    acc = acc + cw_ref[2:3, :] * s2_ref[...] + cw_ref[3:4, :] * xn_ref[...]
    o_ref[...] = _silu(acc)


def decode_conv(s0, s1, s2, xnew, conv_w, conv_b):
    b, n = xnew.shape
    blk = pl.BlockSpec((b, CONV_SLAB), lambda j: (0, j))
    return pl.pallas_call(
        _decode_conv_body,
        grid=(n // CONV_SLAB,),
        in_specs=[blk, blk, blk, blk, pl.BlockSpec((CONV_WIDTH, CONV_SLAB), lambda j: (0, j)),
                  pl.BlockSpec((1, CONV_SLAB), lambda j: (0, j))],
        out_specs=blk,
        out_shape=jax.ShapeDtypeStruct((b, n), F32),
        compiler_params=_cparams(("arbitrary",)),
        name="decode_conv",
    )(s0, s1, s2, xnew, conv_w, conv_b)


def _decode_prep_body(xs_ref, b_ref, c_ref, dt_ref, dtb_ref, alog_ref, e_ref, dtx_ref, daexp_ref, bcx_ref):
    lane = lax.broadcasted_iota(jnp.int32, dt_ref.shape, 1)
    dt = jnp.where(lane < SSM_HEADS, _softplus(dt_ref[...] + dtb_ref[...]), 0.0)
    da = jnp.where(lane < SSM_HEADS, jnp.exp(dt * (-jnp.exp(alog_ref[...]))), 0.0)
    e = e_ref[...]
    expand = lambda v: sum(jnp.dot(p, e, preferred_element_type=F32) for p in _split3(v))
    dtx = expand(dt) * xs_ref[...]
    dtx_ref[...] = dtx
    daexp_ref[...] = expand(da)
    bcx_ref[...] = dtx * jnp.sum(b_ref[...] * c_ref[...], axis=-1, keepdims=True)


def decode_prep(xbc_act, dt_raw, dt_bias, a_log, expand_mat):
    b = xbc_act.shape[0]
    xblk = pl.BlockSpec((b, GROUP_X), lambda g: (0, g))
    vec = pl.BlockSpec((1, LANES), lambda g: (0, 0))
    nb = D_INNER // D_STATE
    out = jax.ShapeDtypeStruct((b, D_INNER), F32)
    return pl.pallas_call(
        _decode_prep_body,
        grid=(SSM_GROUPS,),
        in_specs=[
            xblk,
            pl.BlockSpec((b, D_STATE), lambda g: (0, nb + g)),
            pl.BlockSpec((b, D_STATE), lambda g: (0, nb + SSM_GROUPS + g)),
            pl.BlockSpec((b, LANES), lambda g: (0, 0)),
            vec, vec,
            pl.BlockSpec((LANES, GROUP_X), lambda g: (0, g)),
        ],
        out_specs=[xblk, xblk, xblk],
        out_shape=[out, out, out],
        compiler_params=_cparams(("arbitrary",)),
        name="decode_prep",
    )(xbc_act, xbc_act, xbc_act, dt_raw, dt_bias, a_log, expand_mat)


TILE_ROWS = 128
SPLIT_ROWS = 16


SEQS_PER_STEP = 2


STATE_ROWS = SSM_HEADS * SSM_HEAD_DIM
STATE_SEQS_PER_STEP = 4


def _decode_state_body(h_ref, dtx_ref, da_ref, b_ref, c_ref, ho_ref, hc_ref):
    rid = lambda w: lax.broadcasted_iota(jnp.int32, (SPLIT_ROWS, w), 0)
    pick = lambda r, pieces, w: functools.reduce(
        lambda acc, kv: jnp.where(r == kv[0], jnp.broadcast_to(kv[1].astype(F32), (SPLIT_ROWS, w)), acc),
        pieces, jnp.zeros((SPLIT_ROWS, w), F32))
    r = rid(D_INNER)
    rb = rid(SSM_GROUPS * D_STATE)
    r1 = rid(D_STATE)
    rd = jnp.where(jnp.logical_and(r1 >= 6, r1 <= 8), 1.0, 0.0).astype(BF16)
    for q in range(STATE_SEQS_PER_STEP):
        xh, xm, xl = _split3(dtx_ref[q])
        ah, am, al = _split3(da_ref[q])
        bh, bm, bl = _split3(b_ref[q])
        lmat = pick(r, [(0, xh), (1, xh), (2, xh), (3, xm), (4, xm), (5, xl), (6, ah), (7, am), (8, al)], D_INNER)
        ru = pick(rb, [(0, bh), (1, bm), (2, bl), (3, bh), (4, bm), (5, bh)], SSM_GROUPS * D_STATE).astype(BF16)
        crow = c_ref[q]
        for i in range(STATE_ROWS // TILE_ROWS):
            g = (i * TILE_ROWS) // GROUP_X
            lanes = slice(i * TILE_ROWS, (i + 1) * TILE_ROWS)
            lt = lmat[:, lanes].T.astype(BF16)
            upd = jnp.dot(lt, ru[:, g * D_STATE:(g + 1) * D_STATE], preferred_element_type=F32)
            dec = jnp.dot(lt, rd, preferred_element_type=F32)
            h = h_ref[q, lanes, :]
            ho_ref[q, lanes, :] = h * dec + upd
            t = (h * crow[:, g * D_STATE:(g + 1) * D_STATE]).T
            hc_ref[q, :, lanes] = jnp.sum(t, axis=0, keepdims=True)


def decode_state(h, dtx, daexp, xbc_act3):
    b = h.shape[0]
    nq = STATE_SEQS_PER_STEP
    hblk = pl.BlockSpec((nq, STATE_ROWS, D_STATE), lambda i: (i, 0, 0))
    rblk = pl.BlockSpec((nq, 1, D_INNER), lambda i: (i, 0, 0))
    bc = SSM_GROUPS * D_STATE
    return pl.pallas_call(
        _decode_state_body,
        grid=(b // nq,),
        in_specs=[
            hblk, rblk, rblk,
            pl.BlockSpec((nq, 1, bc), lambda i: (i, 0, D_INNER // bc)),
            pl.BlockSpec((nq, 1, bc), lambda i: (i, 0, D_INNER // bc + 1)),
        ],
        out_specs=[hblk, rblk],
        out_shape=[jax.ShapeDtypeStruct((b, STATE_ROWS, D_STATE), F32), jax.ShapeDtypeStruct((b, 1, D_INNER), F32)],
        compiler_params=_cparams(("arbitrary",)),
        name="decode_state",
    )(h, dtx, daexp, xbc_act3, xbc_act3)


def _decode_post_body(hc_ref, da_ref, bcx_ref, xs_ref, sz_ref, dexp_ref, nw_ref, o_ref):
    y = da_ref[...] * hc_ref[...] + bcx_ref[...] + dexp_ref[...] * xs_ref[...]
    gt = y * sz_ref[...]
    ms = jnp.mean(gt * gt, axis=-1, keepdims=True)
    o_ref[...] = (gt * lax.rsqrt(ms + RMS_EPS) * nw_ref[...]).astype(o_ref.dtype)


def decode_post(hc, daexp, bcx, xbc_act, sz, d_exp, norm_w):
    b = hc.shape[0]
    xblk = pl.BlockSpec((b, GROUP_X), lambda g: (0, g))
    vblk = pl.BlockSpec((1, GROUP_X), lambda g: (0, g))
    return pl.pallas_call(
        _decode_post_body,
        grid=(SSM_GROUPS,),
        in_specs=[xblk, xblk, xblk, xblk, xblk, vblk, vblk],
        out_specs=xblk,
        out_shape=jax.ShapeDtypeStruct((b, D_INNER), BF16),
        compiler_params=_cparams(("arbitrary",)),
        name="decode_post",
    )(hc, daexp, bcx, xbc_act, sz, d_exp, norm_w)


SUPER = WINDOW_MAX
ATT_N = 128
MERGE_ROWS = 256


def _rel_bucket(dist):
    max_exact = NUM_BUCKETS // 2
    df = jnp.maximum(dist, max_exact).astype(F32)
    large = max_exact + (jnp.log(df / max_exact) / math.log(MAX_DISTANCE / max_exact)
                         * (NUM_BUCKETS - max_exact)).astype(jnp.int32)
    return jnp.where(dist < max_exact, dist, jnp.minimum(large, NUM_BUCKETS - 1))


def _prompt_bias_rows(rel_bias):
    n = ATT_N
    back = n - jnp.arange(2 * n)
    rows = []
    for g, (_, dil) in enumerate(DILATION_GROUPS):
        tab = rel_bias[:, g * N_HEADS_B:(g + 1) * N_HEADS_B]
        bias = tab[_rel_bucket(jnp.clip(back, 0, n) * dil)].astype(F32) * LOG2E
        rows.append(jnp.where((back >= 0)[:, None], bias, NEG_BIG).T)
    return jnp.stack(rows, axis=0)[:, :, None, :]


def _rows(start, size, stride):
    return pl.ds(start, size) if stride == 1 else pl.ds(start, size, stride=stride)


def _attn_prompt_body(q0_ref, q1_ref, q2_ref, kc_ref, kp_ref, vc_ref, vp_ref, t_ref, o_ref, og_ref, lse_ref):
    first = pl.program_id(0) == 0
    n = ATT_N
    scale = HEAD_DIM ** -0.5 * LOG2E
    key_id = lax.broadcasted_iota(jnp.int32, (n, 2 * n), 1)
    q_refs = (q0_ref, q1_ref, q2_ref)
    for g, (win, dil) in enumerate(DILATION_GROUPS):
        nblk = SUPER // (n * dil)
        bias = pltpu.roll(jnp.broadcast_to(t_ref[g, 0], (n, 2 * n)), 0, 1, stride=1, stride_axis=0)
        for r in range(dil):
            prev = _rows((nblk - 1) * n * dil + r, n, dil)
            kprev, vprev = kp_ref[prev, :].astype(BF16), vp_ref[prev, :].astype(BF16)
            for c in range(nblk):
                start = c * n * dil + r
                cur = _rows(start, n, dil)
                qb = q_refs[g][cur, :].astype(BF16)
                kcur, vcur = kc_ref[cur, :].astype(BF16), vc_ref[cur, :].astype(BF16)
                keys = jnp.concatenate([kprev, kcur], axis=0)
                vals = jnp.concatenate([vprev, vcur], axis=0)
                kprev, vprev = kcur, vcur
                s = _dot_nt(qb, keys) * scale + bias
                if c == 0:
                    s = jnp.where(jnp.logical_and(first, key_id < n), NEG_BIG, s)
                m = jnp.max(s, axis=-1, keepdims=True)
                p = jnp.exp2(s - m)
                den = jnp.sum(p, axis=-1, keepdims=True)
                o = jnp.dot(p.astype(BF16), vals, preferred_element_type=F32) * (1.0 / den)
                og_ref[g, cur, :] = o
                lse_ref[g, cur, :] = jnp.broadcast_to(m + jnp.log2(den), (n, LANES))
    for i in range(SUPER // MERGE_ROWS):
        rows = slice(i * MERGE_ROWS, (i + 1) * MERGE_ROWS)
        ls = [lse_ref[g, rows, :] for g in range(N_GROUPS_B)]
        mx = functools.reduce(jnp.maximum, ls)
        es = [jnp.exp2(l - mx) for l in ls]
        tot = functools.reduce(lambda a, b: a + b, es)
        acc = functools.reduce(lambda a, b: a + b, [es[g] * og_ref[g, rows, :] for g in range(N_GROUPS_B)])
        o_ref[rows, :] = (acc * (1.0 / tot)).astype(o_ref.dtype)


def attn_prompt(q, kv, bias_rows):
    s = q.shape[0]
    blk = lambda f: pl.BlockSpec((SUPER, HEAD_DIM), f)
    return pl.pallas_call(
        _attn_prompt_body,
        grid=(s // SUPER, N_HEADS_B),
        in_specs=[
            blk(lambda c, h: (c, h)),
            blk(lambda c, h: (c, N_HEADS_B + h)),
            blk(lambda c, h: (c, 2 * N_HEADS_B + h)),
            blk(lambda c, h: (c, h)),
            blk(lambda c, h: (jnp.maximum(c - 1, 0), h)),
            blk(lambda c, h: (c, N_HEADS_B + h)),
            blk(lambda c, h: (jnp.maximum(c - 1, 0), N_HEADS_B + h)),
            pl.BlockSpec((N_GROUPS_B, 1, 1, 2 * ATT_N), lambda c, h: (0, h, 0, 0)),
        ],
        out_specs=blk(lambda c, h: (c, h)),
        out_shape=jax.ShapeDtypeStruct((s, KV_DIM), BF16),
        scratch_shapes=[
            pltpu.VMEM((N_GROUPS_B, SUPER, HEAD_DIM), F32),
            pltpu.VMEM((N_GROUPS_B, SUPER, LANES), F32),
        ],
        compiler_params=_cparams(("arbitrary", "arbitrary")),
        name="attn_prompt",
    )(q, q, q, kv, kv, kv, kv, bias_rows)


N_BUF = min(WINDOW_MAX, PAST_LEN)
ROW_MAJOR = 16
ROW_MID = 4


def _decode_bias_tables(rel_bias):
    n = ATT_N
    j = n - jnp.arange(n)
    cached, new = [], []
    for g, (_, dil) in enumerate(DILATION_GROUPS):
        tab = rel_bias[:, g * N_HEADS_B:(g + 1) * N_HEADS_B]
        cached.append(tab[_rel_bucket(j * dil)])
        new.append(tab[_rel_bucket(jnp.zeros((), jnp.int32))])
    cached = jnp.broadcast_to(jnp.stack(cached)[..., None], (N_GROUPS_B, n, N_HEADS_B, LANES)).astype(F32)
    new = jnp.broadcast_to(jnp.stack(new)[..., None], (N_GROUPS_B, N_HEADS_B, LANES)).astype(F32)
    return cached, new


def _attn_decode_body(q_ref, kvn_ref, k0_ref, v0_ref, k1_ref, v1_ref, k2_ref, v2_ref, bc_ref, bn_ref, o_ref):
    n = ATT_N
    scale = HEAD_DIM ** -0.5
    rows = lambda x: x.reshape((-1, N_HEADS_B, HEAD_DIM))

    def windows(r0, r1, r2, b):
        w0 = r0[b]
        w1 = r1[b]
        return (rows(w0),
                jnp.concatenate([rows(w1), rows(w0[:, :, 0])], axis=0),
                jnp.concatenate([r2[b], w1[:, 0], w0[:, 0, 0]], axis=0))

    for b in range(SEQS_PER_STEP):
        kn, vn = kvn_ref[b, 0], kvn_ref[b, 1]
        ks = windows(k0_ref, k1_ref, k2_ref, b)
        vs = windows(v0_ref, v1_ref, v2_ref, b)
        outs, lses = [], []
        for g in range(N_GROUPS_B):
            q = q_ref[b, g] * scale
            k, v = ks[g], vs[g]
            s = jnp.sum(k * q[None], axis=-1, keepdims=True) + bc_ref[g]
            s0 = jnp.sum(kn * q, axis=-1, keepdims=True) + bn_ref[g]
            m = jnp.maximum(jnp.max(s, axis=0), s0)
            p = jnp.exp(s - m[None])
            p0 = jnp.exp(s0 - m)
            den = jnp.sum(p, axis=0) + p0
            outs.append((jnp.sum(p * v, axis=0) + p0 * vn) * (1.0 / den))
            lses.append(m + jnp.log(den))
        mx = functools.reduce(jnp.maximum, lses)
        es = [jnp.exp(l - mx) for l in lses]
        tot = functools.reduce(lambda x, y: x + y, es)
        acc = functools.reduce(lambda x, y: x + y, [e * o for e, o in zip(es, outs)])
        o_ref[b] = (acc * (1.0 / tot)).astype(o_ref.dtype)


def attn_decode(q, kv_new, cache_k, cache_v, bias_cached, bias_new):
    b = q.shape[0]
    view = (b, N_BUF // ROW_MAJOR, ROW_MAJOR // ROW_MID, ROW_MID, N_HEADS_B, HEAD_DIM)
    ck, cv = cache_k.reshape(view), cache_v.reshape(view)
    n = ATT_N
    nq = SEQS_PER_STEP
    hd = (N_HEADS_B, HEAD_DIM)
    m0 = n // ROW_MAJOR
    m1 = (n * ROW_MID - n) // ROW_MAJOR
    m2 = N_BUF // ROW_MAJOR - m0 - m1
    assert m2 % m1 == 0 and (m1 + m2) % m0 == 0
    g0 = pl.BlockSpec((nq, m0, ROW_MAJOR // ROW_MID, ROW_MID) + hd, lambda i: (i, (m1 + m2) // m0, 0, 0, 0, 0))
    g1 = pl.BlockSpec((nq, m1, ROW_MAJOR // ROW_MID, None) + hd, lambda i: (i, m2 // m1, 0, 0, 0, 0))
    g2 = pl.BlockSpec((nq, m2, None, None) + hd, lambda i: (i, 0, 0, 0, 0, 0))
    return pl.pallas_call(
        _attn_decode_body,
        grid=(b // nq,),
        in_specs=[
            pl.BlockSpec((nq, N_GROUPS_B) + hd, lambda i: (i, 0, 0, 0)),
            pl.BlockSpec((nq, 2) + hd, lambda i: (i, 0, 0, 0)),
            g0, g0, g1, g1, g2, g2,
            pl.BlockSpec((N_GROUPS_B, n) + (N_HEADS_B, LANES), lambda i: (0, 0, 0, 0)),
            pl.BlockSpec((N_GROUPS_B, N_HEADS_B, LANES), lambda i: (0, 0, 0)),
        ],
        out_specs=pl.BlockSpec((nq,) + hd, lambda i: (i, 0, 0)),
        out_shape=jax.ShapeDtypeStruct((b,) + hd, BF16),
        compiler_params=_cparams(("arbitrary",)),
        name="attn_decode",
    )(q, kv_new, ck, cv, ck, cv, ck, cv, bias_cached, bias_new)


ADA_ROWS = 136
Z_TILE = 1024


def _pad_lanes(v):
    return jnp.pad(v.astype(F32), (0, LANES - v.shape[0]))[None]


TM = 1024
TM_LONG_K = 512
TM_GLU = 2048
TM_EW = 256


def _trunk(xs, adas, mixers_a, mixers_b, p):
    tms = (TM_EW, xs[1].shape[0])
    vec = lambda a, l, s: a[l, s][None]
    sh1, sc1, g1, sh2, sc2, g2 = zip(*(a[0] for a in adas))
    sh1b, sc1b, g1b, sh2b, sc2b, g2b = zip(*(a[1] for a in adas))
    shkv, sckv = zip(*(a[2] for a in adas))
    paths = range(2)

    def ln(x, y, gate, l, s, mods):
        outs = [post_ln(x[i], y[i], gate[i], vec(p["ln_g"], l, s), vec(p["ln_b"], l, s),
                        [(sc[i], sh[i]) for sc, sh in mods], tm=tms[i]) for i in paths]
        return zip(*outs)

    def mm(a, w, **kw):
        tm = kw.pop("tm", TM)
        return matmul(a[0], w, rider=a[1], tm=tm, **kw)

    u = [modulate(xs[i], sc1[i], sh1[i], tm=tms[i]) for i in paths]
    w_in_t = jnp.swapaxes(p["ssm_w_in"], 1, 2).reshape(-1, D_MODEL)
    sz = mm(u, w_in_t, w_transposed=True, out_act="silu", tn=Z_TILE, n_out=D_INNER, name="in_z")
    xbc = mm(u, w_in_t, w_transposed=True, tn=Z_TILE, n_out=CONV_DIM, col_off=D_INNER // Z_TILE, name="in_xbc")
    dt = mm(u, w_in_t, w_transposed=True, tn=LANES, n_out=LANES, col_off=(D_INNER + CONV_DIM) // LANES, name="in_dt")
    gated, extras = zip(*(mixers_a[i](sz[i], xbc[i], dt[i]) for i in paths))
    y = mm(gated, p["ssm_w_out"].reshape(D_INNER, D_MODEL), tn=512, out_dtype=BF16, name="ssm_out")
    x, u = ln(xs, y, g1, 0, 0, [(sc2, sh2)])
    h = glu_matmul(u[0], u[1], p["ffn_w_gate"], p["ffn_w_up"], layer=0, tm=TM_GLU, tn=512, name="ffn0_glu")
    y = mm(h, p["ffn_w_down"], layer=0, tm=TM_LONG_K, tn=512, out_dtype=BF16, name="ffn0_down")
    x, u, ukv = ln(x, y, g2, 0, 1, [(sc1b, sh1b), (sckv, shkv)])

    kv = mm(ukv, p["w_kv"], tn=1024, name="kv_proj")
    q = mm(u, p["attn_w_q"], layer=0, tn=1024, name="q_proj")
    o = [mixers_b[i](q[i], kv[i]) for i in paths]
    y = mm(o, p["attn_w_o"], layer=0, tn=1024, out_dtype=BF16, name="attn_out")
    x, u = ln(x, y, g1b, 1, 0, [(sc2b, sh2b)])
    h = glu_matmul(u[0], u[1], p["ffn_w_gate"], p["ffn_w_up"], layer=1, tm=TM_GLU, tn=512, name="ffn1_glu")
    y = mm(h, p["ffn_w_down"], layer=1, tm=TM_LONG_K, tn=512, out_dtype=BF16, name="ffn1_down")
    (x,) = ln(x, y, g2b, 1, 1, [])
    return [(x[i], extras[i], kv[i]) for i in paths]


def kernel(x_prompt, x_sample, state_conv, state_ssm, cache_k, cache_v, c_prompt, c_sample, w_ada, b_ada, ln_g, ln_b, ffn_w_gate, ffn_w_up, ffn_w_down, ssm_w_in, ssm_conv_w, ssm_conv_b, ssm_dt_bias, ssm_a_log, ssm_d, ssm_norm_w, ssm_w_out, kv_w_ada, kv_b_ada, w_kv, attn_w_q, attn_w_o, rel_bias):
    p = dict(ln_g=ln_g, ln_b=ln_b, ffn_w_gate=ffn_w_gate, ffn_w_up=ffn_w_up, ffn_w_down=ffn_w_down,
             ssm_w_in=ssm_w_in, ssm_w_out=ssm_w_out, w_kv=w_kv, attn_w_q=attn_w_q, attn_w_o=attn_w_o)
    s, b, d = SEQ, DEC_BATCH, D_MODEL

    c_all = jnp.concatenate([c_sample, c_prompt, jnp.zeros((ADA_ROWS - b - 1, d), F32)], axis=0)
    ada = [matmul(c_all, w_ada, layer=l, tm=ADA_ROWS, tn=2048, bias=b_ada[l][None], act="silu", name=f"ada{l}")
           for l in range(DEPTH)]
    adakv = matmul(c_all, kv_w_ada, tm=ADA_ROWS, tn=2048, bias=kv_b_ada[None], act="silu", name="ada_kv")
    split = lambda m, n, rows: [m[rows, i * d:(i + 1) * d] for i in range(n)]
    smp, prm = slice(0, b), slice(b, b + 1)

    conv_w, conv_b = ssm_conv_w[0], ssm_conv_b[0][None]
    dt_bias, a_log = _pad_lanes(ssm_dt_bias[0]), _pad_lanes(ssm_a_log[0])
    d_exp = jnp.repeat(ssm_d[0].astype(F32), SSM_HEAD_DIM)[None]
    norm_w = ssm_norm_w[0][None]

    def mixer_a_prompt(sz, xbc, dt):
        gated, hfin = ssd_prompt(sz, xbc, dt, conv_w, conv_b, dt_bias, a_log, d_exp, norm_w)
        return gated, (xbc[s - (CONV_WIDTH - 1):], hfin)

    bias_rows = _prompt_bias_rows(rel_bias)
    mixer_b_prompt = lambda q, kv: attn_prompt(q, kv, bias_rows)

    expand_mat = (lax.broadcasted_iota(jnp.int32, (LANES, D_INNER), 0)
                  == lax.broadcasted_iota(jnp.int32, (LANES, D_INNER), 1) // SSM_HEAD_DIM).astype(BF16)

    def mixer_a_decode(sz, xbc, dt):
        st = state_conv[0]
        xa = decode_conv(st[:, 0], st[:, 1], st[:, 2], xbc, conv_w, conv_b)
        dtx, daexp, bcx = decode_prep(xa, dt, dt_bias, a_log, expand_mat)
        hnew, hc = decode_state(state_ssm.reshape(b, STATE_ROWS, D_STATE), dtx[:, None], daexp[:, None], xa[:, None])
        gated = decode_post(hc[:, 0], daexp, bcx, xa, sz, d_exp, norm_w)
        new_conv = jnp.concatenate([st[:, 1:], xbc[:, None]], axis=1)
        return gated, (new_conv, hnew)

    bias_cached, bias_new = _decode_bias_tables(rel_bias)

    def mixer_b_decode(q, kv):
        o = attn_decode(q.reshape(b, N_GROUPS_B, N_HEADS_B, HEAD_DIM), kv.reshape(b, 2, N_HEADS_B, HEAD_DIM),
                        cache_k, cache_v, bias_cached, bias_new)
        return o.reshape(b, KV_DIM)

    adas = [(split(ada[0], 6, rows), split(ada[1], 6, rows), split(adakv, 2, rows)) for rows in (prm, smp)]
    (y_p, (conv_p, ssm_p), kv_p), (y_s, (conv_s, ssm_s), kv_s) = _trunk(
        (x_prompt.reshape(s, d), x_sample.reshape(b, d)), adas,
        (mixer_a_prompt, mixer_a_decode), (mixer_b_prompt, mixer_b_decode), p)

    n_keep = min(WINDOW_MAX, s)
    heads = (N_HEADS_B, HEAD_DIM)
    return (
        y_p.reshape(1, s, d),
        y_s.reshape(b, 1, d),
        conv_p.reshape(1, 1, CONV_WIDTH - 1, CONV_DIM),
        ssm_p.reshape(1, 1, SSM_HEADS, SSM_HEAD_DIM, D_STATE),
        kv_p[s - n_keep:, :KV_DIM].reshape((1, n_keep) + heads),
        kv_p[s - n_keep:, KV_DIM:].reshape((1, n_keep) + heads),
        conv_s.reshape(1, b, CONV_WIDTH - 1, CONV_DIM),
        ssm_s.reshape(1, b, SSM_HEADS, SSM_HEAD_DIM, D_STATE),
        kv_s[:, :KV_DIM].reshape((b, 1) + heads),
        kv_s[:, KV_DIM:].reshape((b, 1) + heads),
    )
```

```python
import functools
import math

import jax
import jax.numpy as jnp
from jax import lax
from jax.experimental import pallas as pl
from jax.experimental.pallas import tpu as pltpu

F32 = jnp.float32
BF16 = jnp.bfloat16

D_MODEL = 2048
SEQ = 8192
DEPTH = 2
DEC_BATCH = 128
PAST_LEN = 2048
D_INNER = 2 * D_MODEL
SSM_HEAD_DIM = 64
SSM_HEADS = D_INNER // SSM_HEAD_DIM
SSM_GROUPS = 8
HEADS_PER_GROUP = SSM_HEADS // SSM_GROUPS
D_STATE = 128
CONV_WIDTH = 4
CONV_DIM = D_INNER + 2 * SSM_GROUPS * D_STATE
SSD_CHUNK = 128
RMS_EPS = 1e-5
HEAD_DIM = 128
N_HEADS_B = D_MODEL // HEAD_DIM
DILATION_GROUPS = ((128, 1), (512, 4), (2048, 16))
N_GROUPS_B = len(DILATION_GROUPS)
WINDOW_MAX = max(w for w, _ in DILATION_GROUPS)
Q_DIM = N_GROUPS_B * N_HEADS_B * HEAD_DIM
KV_DIM = N_HEADS_B * HEAD_DIM
NUM_BUCKETS = 32
MAX_DISTANCE = WINDOW_MAX
D_FF = -(-8 * D_MODEL // (3 * 256)) * 256
ALPHA = (2 * DEPTH) ** 0.25
LN_EPS = 1e-5

LANES = 128
SUBLANES = 8
VMEM_LIMIT_BYTES = 56 * 1024 * 1024
NEG_BIG = -1e30
LOG2E = math.log2(math.e)


def _cparams(semantics):
    return pltpu.CompilerParams(dimension_semantics=semantics, vmem_limit_bytes=VMEM_LIMIT_BYTES)


def _silu(x):
    return x * (1.0 / (1.0 + jnp.exp2(x * -LOG2E)))


def _split3(x):
    hi = x.astype(BF16)
    r1 = x - hi.astype(F32)
    mid = r1.astype(BF16)
    lo = (r1 - mid.astype(F32)).astype(BF16)
    return hi, mid, lo


CAST_ROWS = 256


def _cast_tile(w_ref, wbf_ref, transposed=False):
    k, tn = wbf_ref.shape
    if transposed:
        step = min(CAST_ROWS, tn)
        for c in range(tn // step):
            wbf_ref[:, c * step:(c + 1) * step] = w_ref[c * step:(c + 1) * step, :].T.astype(BF16)
        return
    rows = CAST_ROWS if k % CAST_ROWS == 0 else k
    def body(r, c):
        off = pl.multiple_of(r * rows, rows)
        wbf_ref[pl.ds(off, rows), :] = w_ref[pl.ds(off, rows), :].astype(BF16)
        return c
    lax.fori_loop(0, k // rows, body, 0)


def _mm_body(*refs, act, out_act, has_bias, has_rider, w_transposed):
    refs = list(refs)
    a_ref, w_ref = refs.pop(0), refs.pop(0)
    b_ref = refs.pop(0) if has_bias else None
    a2_ref = refs.pop(0) if has_rider else None
    o_ref = refs.pop(0)
    o2_ref = refs.pop(0) if has_rider else None
    wbf_ref = refs.pop(0)

    def product(lhs_ref):
        a = lhs_ref[...]
        if act == "silu":
            a = _silu(a.astype(F32))
        acc = jnp.dot(a.astype(BF16), wbf_ref[...], preferred_element_type=F32)
        if has_bias:
            acc = acc + b_ref[...]
        return _silu(acc) if out_act == "silu" else acc

    @pl.when(pl.program_id(1) == 0)
    def _():
        _cast_tile(w_ref, wbf_ref, w_transposed)
        if has_rider:
            o2_ref[...] = product(a2_ref).astype(o2_ref.dtype)

    o_ref[...] = product(a_ref).astype(o_ref.dtype)


def _weight_spec(w, layer, tn, col_off=0, transposed=False):
    if transposed:
        return pl.BlockSpec((tn, w.shape[1]), lambda j, i: (j + col_off, 0))
    k = w.shape[-2]
    if w.ndim == 2:
        return pl.BlockSpec((k, tn), lambda j, i: (0, j + col_off))
    return pl.BlockSpec((None, k, tn), lambda j, i: (layer, 0, j + col_off))


def matmul(a, w, *, tm, tn, layer=0, n_out=None, col_off=0, bias=None, act=None, out_act=None, rider=None,
           w_transposed=False, out_dtype=F32, name="mm"):
    m, k = a.shape
    n_out = (w.shape[0] if w_transposed else w.shape[-1]) if n_out is None else n_out
    grid = (pl.cdiv(n_out, tn), m // tm)
    in_specs = [
        pl.BlockSpec((tm, k), lambda j, i: (i, 0)),
        _weight_spec(w, layer, tn, col_off, w_transposed),
    ]
    args = [a, w]
    out_specs = [pl.BlockSpec((tm, tn), lambda j, i: (i, j))]
    out_shape = [jax.ShapeDtypeStruct((m, n_out), out_dtype)]
    if bias is not None:
        in_specs.append(pl.BlockSpec((1, tn), lambda j, i: (0, j + col_off)))
        args.append(bias)
    if rider is not None:
        m2 = rider.shape[0]
        in_specs.append(pl.BlockSpec((m2, k), lambda j, i: (0, 0)))
        args.append(rider)
        out_specs.append(pl.BlockSpec((m2, tn), lambda j, i: (0, j)))
        out_shape.append(jax.ShapeDtypeStruct((m2, n_out), out_dtype))
    outs = pl.pallas_call(
        functools.partial(_mm_body, act=act, out_act=out_act, has_bias=bias is not None,
                          has_rider=rider is not None, w_transposed=w_transposed),
        grid=grid,
        in_specs=in_specs,
        out_specs=out_specs,
        out_shape=out_shape,
        scratch_shapes=[pltpu.VMEM((k, tn), BF16)],
        compiler_params=_cparams(("arbitrary", "arbitrary")),
        name=name,
    )(*args)
    return outs if rider is not None else outs[0]


def _glu_body(a_ref, wg_ref, wu_ref, a2_ref, o_ref, o2_ref, wgbf_ref, wubf_ref):
    def product(lhs_ref):
        a = lhs_ref[...]
        g = jnp.dot(a, wgbf_ref[...], preferred_element_type=F32)
        u = jnp.dot(a, wubf_ref[...], preferred_element_type=F32)
        return (_silu(g) * u).astype(BF16)

    @pl.when(pl.program_id(1) == 0)
    def _():
        _cast_tile(wg_ref, wgbf_ref)
        _cast_tile(wu_ref, wubf_ref)
        o2_ref[...] = product(a2_ref)

    o_ref[...] = product(a_ref)


def glu_matmul(a, rider, w_gate, w_up, *, tm, tn, layer=0, name="glu"):
    m, k = a.shape
    m2 = rider.shape[0]
    n = w_gate.shape[-1]
    return pl.pallas_call(
        _glu_body,
        grid=(n // tn, m // tm),
        in_specs=[
            pl.BlockSpec((tm, k), lambda j, i: (i, 0)),
            _weight_spec(w_gate, layer, tn),
            _weight_spec(w_up, layer, tn),
            pl.BlockSpec((m2, k), lambda j, i: (0, 0)),
        ],
        out_specs=[pl.BlockSpec((tm, tn), lambda j, i: (i, j)), pl.BlockSpec((m2, tn), lambda j, i: (0, j))],
        out_shape=[jax.ShapeDtypeStruct((m, n), BF16), jax.ShapeDtypeStruct((m2, n), BF16)],
        scratch_shapes=[pltpu.VMEM((k, tn), BF16), pltpu.VMEM((k, tn), BF16)],
        compiler_params=_cparams(("arbitrary", "arbitrary")),
        name=name,
    )(a, w_gate, w_up, rider)


ADA_ROWS = 136


class Mod:
    def __init__(self, table, col, prompt):
        self.table, self.col, self.prompt = table, col, prompt

    def spec(self, tm):
        col = self.col
        if self.prompt:
            return pl.BlockSpec((SUBLANES, D_MODEL), lambda i: (DEC_BATCH // SUBLANES, col))
        return pl.BlockSpec((tm, D_MODEL), lambda i: (i, col))


def _mod_rows(ref):
    return ref[0:1, :] if ref.shape[0] == SUBLANES else ref[...]


def _modulate_body(x_ref, sc_ref, sh_ref, o_ref):
    o_ref[...] = (x_ref[...] * (1.0 + _mod_rows(sc_ref)) + _mod_rows(sh_ref)).astype(o_ref.dtype)


def modulate(x, scale, shift, *, tm):
    m, d = x.shape
    return pl.pallas_call(
        _modulate_body,
        grid=(m // tm,),
        in_specs=[pl.BlockSpec((tm, d), lambda i: (i, 0)), scale.spec(tm), shift.spec(tm)],
        out_specs=pl.BlockSpec((tm, d), lambda i: (i, 0)),
        out_shape=jax.ShapeDtypeStruct((m, d), BF16),
        compiler_params=_cparams(("arbitrary",)),
        name="modulate",
    )(x, scale.table, shift.table)


def _ln_body(*refs, n_mod):
    x_ref, y_ref, gate_ref, g_ref, b_ref = refs[:5]
    mod_refs = refs[5:5 + 2 * n_mod]
    xo_ref = refs[5 + 2 * n_mod]
    u_refs = refs[6 + 2 * n_mod:]
    t = ALPHA * x_ref[...] + (1.0 + _mod_rows(gate_ref)) * y_ref[...].astype(F32)
    mu = jnp.mean(t, axis=-1, keepdims=True)
    tc = t - mu
    var = jnp.mean(tc * tc, axis=-1, keepdims=True)
    xn = tc * lax.rsqrt(var + LN_EPS) * g_ref[...] + b_ref[...]
    xo_ref[...] = xn
    for q in range(n_mod):
        u_refs[q][...] = (xn * (1.0 + _mod_rows(mod_refs[2 * q])) + _mod_rows(mod_refs[2 * q + 1])).astype(BF16)


def post_ln(x, y, gate, ln_g, ln_b, mods, *, tm):
    m, d = x.shape
    row = pl.BlockSpec((tm, d), lambda i: (i, 0))
    vec = pl.BlockSpec((1, d), lambda i: (0, 0))
    in_specs = [row, row, gate.spec(tm), vec, vec]
    args = [x, y, gate.table, ln_g, ln_b]
    for sc, sh in mods:
        in_specs += [sc.spec(tm), sh.spec(tm)]
        args += [sc.table, sh.table]
    outs = pl.pallas_call(
        functools.partial(_ln_body, n_mod=len(mods)),
        grid=(m // tm,),
        in_specs=in_specs,
        out_specs=[row] * (1 + len(mods)),
        out_shape=[jax.ShapeDtypeStruct((m, d), F32)] + [jax.ShapeDtypeStruct((m, d), BF16)] * len(mods),
        compiler_params=_cparams(("arbitrary",)),
        name="post_ln",
    )(*args)
    return outs


CONV_SLAB = 512
GROUP_X = D_INNER // SSM_GROUPS
PAIR = 2 * SSM_HEAD_DIM


def _softplus(x):
    return jnp.maximum(x, 0.0) + jnp.log1p(jnp.exp(-jnp.abs(x)))


def _dot_nt(a, b):
    return lax.dot_general(a, b, (((1,), (1,)), ((), ())), preferred_element_type=F32)


def _dot_tn(a, b):
    return lax.dot_general(a, b, (((0,), (0,)), ((), ())), preferred_element_type=F32)


def _head_spread_matrix():
    r = lax.broadcasted_iota(jnp.int32, (SSM_GROUPS, 2 * LANES, HEADS_PER_GROUP * LANES), 1)
    col = lax.broadcasted_iota(jnp.int32, (SSM_GROUPS, 2 * LANES, HEADS_PER_GROUP * LANES), 2)
    g = lax.broadcasted_iota(jnp.int32, (SSM_GROUPS, 2 * LANES, HEADS_PER_GROUP * LANES), 0)
    hit = jnp.logical_and(r % SSM_HEADS == g * HEADS_PER_GROUP + col // LANES, r < 3 * SSM_HEADS)
    return hit.astype(BF16)


def _ssd_prompt_body(sz_ref, xbc_ref, dt_ref, cw_ref, cb_ref, dtb_ref, alog_ref, dexp_ref, nw_ref, e_ref,
                     g_ref, hfin_ref,
                     ext_ref, act_ref, statet_ref, p_ref, wt_ref, rowp_ref):
    c = pl.program_id(0)
    L = SSD_CHUNK

    @pl.when(c == 0)
    def _():
        ext_ref[0:SUBLANES, :] = jnp.zeros((SUBLANES, CONV_DIM), F32)
        statet_ref[...] = jnp.zeros(statet_ref.shape, F32)

    ext_ref[SUBLANES:SUBLANES + L, :] = xbc_ref[...]
    for s in range(CONV_DIM // LANES):
        cols = slice(s * LANES, (s + 1) * LANES)
        acc = cb_ref[:, cols]
        for i in range(CONV_WIDTH):
            lo = SUBLANES - (CONV_WIDTH - 1) + i
            acc = acc + cw_ref[i:i + 1, cols] * ext_ref[lo:lo + L, cols]
        act_ref[:, cols] = _silu(acc)
    ext_ref[0:SUBLANES, :] = ext_ref[L:L + SUBLANES, :]

    lane = lax.broadcasted_iota(jnp.int32, (L, LANES), 1)
    row = lax.broadcasted_iota(jnp.int32, (L, LANES), 0)
    head_ok = lane < SSM_HEADS
    dt = jnp.where(head_ok, _softplus(dt_ref[...] + dtb_ref[...]), 0.0)
    da = dt * (-jnp.exp(alog_ref[...]))
    causal = row >= lane
    tril = jnp.where(causal, 1.0, 0.0).astype(BF16)
    cs = sum(jnp.dot(tril, p, preferred_element_type=F32) for p in _split3(da)) * LOG2E
    hi, mid, lo = _split3(cs)
    p_ref[:, 0:LANES] = jnp.where(head_ok, hi.astype(F32), pltpu.roll(mid.astype(F32), SSM_HEADS, 1)).astype(BF16)
    p_ref[:, LANES:2 * LANES] = lo
    cst = cs.T
    dtt = dt.T
    rowp_ref[...] = cst - jnp.log2(dtt)
    wt_ref[...] = dtt * jnp.exp2(jnp.broadcast_to(cst[:, L - 1:L], (LANES, L)) - cst)
    lane_lo = lane < SSM_HEAD_DIM

    b_off, c_off = D_INNER, D_INNER + SSM_GROUPS * D_STATE
    for g in range(SSM_GROUPS):
        bg = act_ref[:, b_off + g * D_STATE:b_off + (g + 1) * D_STATE]
        bg_bf = bg.astype(BF16)
        bgt = bg.T
        cg = act_ref[:, c_off + g * D_STATE:c_off + (g + 1) * D_STATE].astype(BF16)
        cb = _dot_nt(cg, bg_bf)
        yoff = jnp.dot(cg, statet_ref[g].astype(BF16), preferred_element_type=F32)
        colb = jnp.dot(p_ref[...], e_ref[g], preferred_element_type=F32)
        heads = slice(g * HEADS_PER_GROUP, (g + 1) * HEADS_PER_GROUP)
        rowpg, wtg = rowp_ref[heads, :], wt_ref[heads, :]
        gated = []
        for pr in range(HEADS_PER_GROUP // 2):
            lanes = slice(pr * PAIR, (pr + 1) * PAIR)
            cols = slice(g * GROUP_X + pr * PAIR, g * GROUP_X + (pr + 1) * PAIR)
            x2 = act_ref[:, cols]
            ms, ecols, ss = [], [], []
            for k in (2 * pr, 2 * pr + 1):
                col = colb[:, k * LANES:(k + 1) * LANES]
                rowv = jnp.broadcast_to(rowpg[k:k + 1, :], (L, LANES))
                lmat = jnp.exp2(jnp.where(causal, col - rowv, NEG_BIG))
                ms.append((cb * lmat).astype(BF16))
                ecols.append(jnp.exp2(col))
                ss.append((bgt * jnp.broadcast_to(wtg[k:k + 1, :], (LANES, L))).astype(BF16))
            rhs = jnp.concatenate([jnp.where(lane_lo, x2, 0.0), jnp.where(lane_lo, 0.0, x2)], axis=0).astype(BF16)
            ydiag = jnp.dot(jnp.concatenate(ms, axis=1), rhs, preferred_element_type=F32)
            escale = jnp.where(lane_lo, ecols[0], ecols[1])
            y = ydiag + yoff[:, lanes] * escale + dexp_ref[:, cols] * x2
            gated.append(y * sz_ref[:, cols])
            snew = jnp.dot(jnp.concatenate(ss, axis=1), rhs, preferred_element_type=F32)
            decay = jnp.broadcast_to(escale[L - 1:L, :], (LANES, LANES))
            statet_ref[g, :, lanes] = statet_ref[g, :, lanes] * decay + snew
        sq = functools.reduce(lambda a, b: a + b, [t * t for t in gated])
        inv = lax.rsqrt(jnp.sum(sq, axis=-1, keepdims=True) * (1.0 / GROUP_X) + RMS_EPS)
        for pr, t in enumerate(gated):
            cols = slice(g * GROUP_X + pr * PAIR, g * GROUP_X + (pr + 1) * PAIR)
            g_ref[:, cols] = (t * inv * nw_ref[:, cols]).astype(g_ref.dtype)

    @pl.when(c == pl.num_programs(0) - 1)
    def _():
        for g in range(SSM_GROUPS):
            for q in range(GROUP_X // LANES):
                blk = slice(q * LANES, (q + 1) * LANES)
                hfin_ref[g, blk, :] = statet_ref[g, :, blk].T


def ssd_prompt(sz, xbc, dt_raw, conv_w, conv_b, dt_bias, a_log, d_exp, norm_w):
    s = sz.shape[0]
    L = SSD_CHUNK
    vec = lambda n: pl.BlockSpec((1, n), lambda c: (0, 0))
    return pl.pallas_call(
        _ssd_prompt_body,
        grid=(s // L,),
        in_specs=[
            pl.BlockSpec((L, D_INNER), lambda c: (c, 0)),
            pl.BlockSpec((L, CONV_DIM), lambda c: (c, 0)),
            pl.BlockSpec((L, LANES), lambda c: (c, 0)),
            pl.BlockSpec((CONV_WIDTH, CONV_DIM), lambda c: (0, 0)),
            vec(CONV_DIM), vec(LANES), vec(LANES), vec(D_INNER), vec(D_INNER),
            pl.BlockSpec((SSM_GROUPS, 2 * LANES, HEADS_PER_GROUP * LANES), lambda c: (0, 0, 0)),
        ],
        out_specs=[
            pl.BlockSpec((L, D_INNER), lambda c: (c, 0)),
            pl.BlockSpec((SSM_GROUPS, GROUP_X, D_STATE), lambda c: (0, 0, 0)),
        ],
        out_shape=[
            jax.ShapeDtypeStruct((s, D_INNER), BF16),
            jax.ShapeDtypeStruct((SSM_GROUPS, GROUP_X, D_STATE), F32),
        ],
        scratch_shapes=[
            pltpu.VMEM((L + 2 * SUBLANES, CONV_DIM), F32),
            pltpu.VMEM((L, CONV_DIM), F32),
            pltpu.VMEM((SSM_GROUPS, D_STATE, GROUP_X), F32),
            pltpu.VMEM((L, 2 * LANES), BF16),
            pltpu.VMEM((LANES, L), F32),
            pltpu.VMEM((LANES, L), F32),
        ],
        compiler_params=_cparams(("arbitrary",)),
        name="ssd_prompt",
    )(sz, xbc, dt_raw, conv_w, conv_b, dt_bias, a_log, d_exp, norm_w, _head_spread_matrix())


def _decode_conv_body(s0_ref, s1_ref, s2_ref, xn_ref, cw_ref, cb_ref, o_ref):
    acc = cb_ref[...] + cw_ref[0:1, :] * s0_ref[...] + cw_ref[1:2, :] * s1_ref[...]
    acc = acc + cw_ref[2:3, :] * s2_ref[...] + cw_ref[3:4, :] * xn_ref[...]
    o_ref[...] = _silu(acc)


def decode_conv(state, xnew, conv_w, conv_b):
    b, n = xnew.shape
    per = n // CONV_SLAB
    blk = pl.BlockSpec((b, CONV_SLAB), lambda j: (0, j))
    past = lambda t: pl.BlockSpec((b, CONV_SLAB), lambda j: (0, t * per + j))
    return pl.pallas_call(
        _decode_conv_body,
        grid=(per,),
        in_specs=[past(0), past(1), past(2), blk, pl.BlockSpec((CONV_WIDTH, CONV_SLAB), lambda j: (0, j)),
                  pl.BlockSpec((1, CONV_SLAB), lambda j: (0, j))],
        out_specs=blk,
        out_shape=jax.ShapeDtypeStruct((b, n), F32),
        compiler_params=_cparams(("arbitrary",)),
        name="decode_conv",
    )(state, state, state, xnew, conv_w, conv_b)


def _decode_prep_body(xs_ref, b_ref, c_ref, dt_ref, dtb_ref, alog_ref, e_ref, dtx_ref, daexp_ref, bcx_ref):
    lane = lax.broadcasted_iota(jnp.int32, dt_ref.shape, 1)
    dt = jnp.where(lane < SSM_HEADS, _softplus(dt_ref[...] + dtb_ref[...]), 0.0)
    da = jnp.where(lane < SSM_HEADS, jnp.exp(dt * (-jnp.exp(alog_ref[...]))), 0.0)
    e = e_ref[...]
    expand = lambda v: sum(jnp.dot(p, e, preferred_element_type=F32) for p in _split3(v))
    dtx = expand(dt) * xs_ref[...]
    dtx_ref[...] = dtx
    daexp_ref[...] = expand(da)
    bcx_ref[...] = dtx * jnp.sum(b_ref[...] * c_ref[...], axis=-1, keepdims=True)


def decode_prep(xbc_act, dt_raw, dt_bias, a_log, expand_mat):
    b = xbc_act.shape[0]
    xblk = pl.BlockSpec((b, GROUP_X), lambda g: (0, g))
    vec = pl.BlockSpec((1, LANES), lambda g: (0, 0))
    nb = D_INNER // D_STATE
    out = jax.ShapeDtypeStruct((b, D_INNER), F32)
    return pl.pallas_call(
        _decode_prep_body,
        grid=(SSM_GROUPS,),
        in_specs=[
            xblk,
            pl.BlockSpec((b, D_STATE), lambda g: (0, nb + g)),
            pl.BlockSpec((b, D_STATE), lambda g: (0, nb + SSM_GROUPS + g)),
            pl.BlockSpec((b, LANES), lambda g: (0, 0)),
            vec, vec,
            pl.BlockSpec((LANES, GROUP_X), lambda g: (0, g)),
        ],
        out_specs=[xblk, xblk, xblk],
        out_shape=[out, out, out],
        compiler_params=_cparams(("arbitrary",)),
        name="decode_prep",
    )(xbc_act, xbc_act, xbc_act, dt_raw, dt_bias, a_log, expand_mat)


TILE_ROWS = 128
SPLIT_ROWS = 16


SEQS_PER_STEP = 2


STATE_ROWS = SSM_HEADS * SSM_HEAD_DIM
STATE_SEQS_PER_STEP = 4


def _decode_state_body(h_ref, dtx_ref, da_ref, b_ref, c_ref, ho_ref, hc_ref):
    rid = lambda w: lax.broadcasted_iota(jnp.int32, (SPLIT_ROWS, w), 0)
    pick = lambda r, pieces, w: functools.reduce(
        lambda acc, kv: jnp.where(r == kv[0], jnp.broadcast_to(kv[1].astype(F32), (SPLIT_ROWS, w)), acc),
        pieces, jnp.zeros((SPLIT_ROWS, w), F32))
    r = rid(D_INNER)
    rb = rid(SSM_GROUPS * D_STATE)
    r1 = rid(D_STATE)
    rd = jnp.where(jnp.logical_and(r1 >= 6, r1 <= 8), 1.0, 0.0).astype(BF16)
    for q in range(STATE_SEQS_PER_STEP):
        xh, xm, xl = _split3(dtx_ref[q])
        ah, am, al = _split3(da_ref[q])
        bh, bm, bl = _split3(b_ref[q])
        lmat = pick(r, [(0, xh), (1, xh), (2, xh), (3, xm), (4, xm), (5, xl), (6, ah), (7, am), (8, al)], D_INNER)
        ru = pick(rb, [(0, bh), (1, bm), (2, bl), (3, bh), (4, bm), (5, bh)], SSM_GROUPS * D_STATE).astype(BF16)
        crow = c_ref[q]
        for i in range(STATE_ROWS // TILE_ROWS):
            g = (i * TILE_ROWS) // GROUP_X
            lanes = slice(i * TILE_ROWS, (i + 1) * TILE_ROWS)
            lt = lmat[:, lanes].T.astype(BF16)
            upd = jnp.dot(lt, ru[:, g * D_STATE:(g + 1) * D_STATE], preferred_element_type=F32)
            dec = jnp.dot(lt, rd, preferred_element_type=F32)
            h = h_ref[q, lanes, :]
            ho_ref[q, lanes, :] = h * dec + upd
            t = (h * crow[:, g * D_STATE:(g + 1) * D_STATE]).T
            hc_ref[q, :, lanes] = jnp.sum(t, axis=0, keepdims=True)


def decode_state(h, dtx, daexp, xbc_act3):
    b = h.shape[0]
    nq = STATE_SEQS_PER_STEP
    hblk = pl.BlockSpec((nq, STATE_ROWS, D_STATE), lambda i: (i, 0, 0))
    rblk = pl.BlockSpec((nq, 1, D_INNER), lambda i: (i, 0, 0))
    bc = SSM_GROUPS * D_STATE
    return pl.pallas_call(
        _decode_state_body,
        grid=(b // nq,),
        in_specs=[
            hblk, rblk, rblk,
            pl.BlockSpec((nq, 1, bc), lambda i: (i, 0, D_INNER // bc)),
            pl.BlockSpec((nq, 1, bc), lambda i: (i, 0, D_INNER // bc + 1)),
        ],
        out_specs=[hblk, rblk],
        out_shape=[jax.ShapeDtypeStruct((b, STATE_ROWS, D_STATE), F32), jax.ShapeDtypeStruct((b, 1, D_INNER), F32)],
        compiler_params=_cparams(("arbitrary",)),
        name="decode_state",
    )(h, dtx, daexp, xbc_act3, xbc_act3)


def _decode_post_body(hc_ref, da_ref, bcx_ref, xs_ref, sz_ref, dexp_ref, nw_ref, o_ref):
    y = da_ref[...] * hc_ref[...] + bcx_ref[...] + dexp_ref[...] * xs_ref[...]
    gt = y * sz_ref[...]
    ms = jnp.mean(gt * gt, axis=-1, keepdims=True)
    o_ref[...] = (gt * lax.rsqrt(ms + RMS_EPS) * nw_ref[...]).astype(o_ref.dtype)


def decode_post(hc, daexp, bcx, xbc_act, sz, d_exp, norm_w):
    b = hc.shape[0]
    xblk = pl.BlockSpec((b, GROUP_X), lambda g: (0, g))
    vblk = pl.BlockSpec((1, GROUP_X), lambda g: (0, g))
    return pl.pallas_call(
        _decode_post_body,
        grid=(SSM_GROUPS,),
        in_specs=[xblk, xblk, xblk, xblk, xblk, vblk, vblk],
        out_specs=xblk,
        out_shape=jax.ShapeDtypeStruct((b, D_INNER), BF16),
        compiler_params=_cparams(("arbitrary",)),
        name="decode_post",
    )(hc, daexp, bcx, xbc_act, sz, d_exp, norm_w)


SUPER = WINDOW_MAX
ATT_N = 128
MERGE_ROWS = 256


def _rel_bucket(dist):
    max_exact = NUM_BUCKETS // 2
    df = jnp.maximum(dist, max_exact).astype(F32)
    large = max_exact + (jnp.log(df / max_exact) / math.log(MAX_DISTANCE / max_exact)
                         * (NUM_BUCKETS - max_exact)).astype(jnp.int32)
    return jnp.where(dist < max_exact, dist, jnp.minimum(large, NUM_BUCKETS - 1))


def _prompt_bias_rows(rel_bias):
    n = ATT_N
    back = n - jnp.arange(2 * n)
    rows = []
    for g, (_, dil) in enumerate(DILATION_GROUPS):
        tab = rel_bias[:, g * N_HEADS_B:(g + 1) * N_HEADS_B]
        bias = tab[_rel_bucket(jnp.clip(back, 0, n) * dil)].astype(F32) * LOG2E
        rows.append(jnp.where((back >= 0)[:, None], bias, NEG_BIG).T)
    return jnp.stack(rows, axis=0)[:, :, None, :]


def _rows(start, size, stride):
    return pl.ds(start, size) if stride == 1 else pl.ds(start, size, stride=stride)


def _attn_prompt_body(q0_ref, q1_ref, q2_ref, kc_ref, kp_ref, vc_ref, vp_ref, t_ref, o_ref, og_ref, lse_ref):
    first = pl.program_id(0) == 0
    n = ATT_N
    scale = HEAD_DIM ** -0.5 * LOG2E
    key_id = lax.broadcasted_iota(jnp.int32, (n, 2 * n), 1)
    q_refs = (q0_ref, q1_ref, q2_ref)
    for g, (win, dil) in enumerate(DILATION_GROUPS):
        nblk = SUPER // (n * dil)
        bias = pltpu.roll(jnp.broadcast_to(t_ref[g, 0], (n, 2 * n)), 0, 1, stride=1, stride_axis=0)
        for r in range(dil):
            prev = _rows((nblk - 1) * n * dil + r, n, dil)
            kprev, vprev = kp_ref[prev, :].astype(BF16), vp_ref[prev, :].astype(BF16)
            for c in range(nblk):
                start = c * n * dil + r
                cur = _rows(start, n, dil)
                qb = q_refs[g][cur, :].astype(BF16)
                kcur, vcur = kc_ref[cur, :].astype(BF16), vc_ref[cur, :].astype(BF16)
                keys = jnp.concatenate([kprev, kcur], axis=0)
                vals = jnp.concatenate([vprev, vcur], axis=0)
                kprev, vprev = kcur, vcur
                s = _dot_nt(qb, keys) * scale + bias
                if c == 0:
                    s = jnp.where(jnp.logical_and(first, key_id < n), NEG_BIG, s)
                m = jnp.max(s, axis=-1, keepdims=True)
                p = jnp.exp2(s - m)
                den = jnp.sum(p, axis=-1, keepdims=True)
                o = jnp.dot(p.astype(BF16), vals, preferred_element_type=F32) * (1.0 / den)
                og_ref[g, cur, :] = o
                lse_ref[g, cur, :] = jnp.broadcast_to(m + jnp.log2(den), (n, LANES))
    for i in range(SUPER // MERGE_ROWS):
        rows = slice(i * MERGE_ROWS, (i + 1) * MERGE_ROWS)
        ls = [lse_ref[g, rows, :] for g in range(N_GROUPS_B)]
        mx = functools.reduce(jnp.maximum, ls)
        es = [jnp.exp2(l - mx) for l in ls]
        tot = functools.reduce(lambda a, b: a + b, es)
        acc = functools.reduce(lambda a, b: a + b, [es[g] * og_ref[g, rows, :] for g in range(N_GROUPS_B)])
        o_ref[rows, :] = (acc * (1.0 / tot)).astype(o_ref.dtype)


def attn_prompt(q, kv, bias_rows):
    s = q.shape[0]
    blk = lambda f: pl.BlockSpec((SUPER, HEAD_DIM), f)
    return pl.pallas_call(
        _attn_prompt_body,
        grid=(s // SUPER, N_HEADS_B),
        in_specs=[
            blk(lambda c, h: (c, h)),
            blk(lambda c, h: (c, N_HEADS_B + h)),
            blk(lambda c, h: (c, 2 * N_HEADS_B + h)),
            blk(lambda c, h: (c, h)),
            blk(lambda c, h: (jnp.maximum(c - 1, 0), h)),
            blk(lambda c, h: (c, N_HEADS_B + h)),
            blk(lambda c, h: (jnp.maximum(c - 1, 0), N_HEADS_B + h)),
            pl.BlockSpec((N_GROUPS_B, 1, 1, 2 * ATT_N), lambda c, h: (0, h, 0, 0)),
        ],
        out_specs=blk(lambda c, h: (c, h)),
        out_shape=jax.ShapeDtypeStruct((s, KV_DIM), BF16),
        scratch_shapes=[
            pltpu.VMEM((N_GROUPS_B, SUPER, HEAD_DIM), F32),
            pltpu.VMEM((N_GROUPS_B, SUPER, LANES), F32),
        ],
        compiler_params=_cparams(("arbitrary", "arbitrary")),
        name="attn_prompt",
    )(q, q, q, kv, kv, kv, kv, bias_rows)


N_BUF = min(WINDOW_MAX, PAST_LEN)
ROW_MAJOR = 16
ROW_MID = 4


def _decode_bias_tables(rel_bias):
    n = ATT_N
    j = n - jnp.arange(n)
    cached, new = [], []
    for g, (_, dil) in enumerate(DILATION_GROUPS):
        tab = rel_bias[:, g * N_HEADS_B:(g + 1) * N_HEADS_B]
        cached.append(tab[_rel_bucket(j * dil)])
        new.append(tab[_rel_bucket(jnp.zeros((), jnp.int32))])
    cached = jnp.broadcast_to(jnp.stack(cached)[..., None] * LOG2E, (N_GROUPS_B, n, N_HEADS_B, LANES)).astype(F32)
    new = jnp.broadcast_to(jnp.stack(new)[..., None] * LOG2E, (N_GROUPS_B, N_HEADS_B, LANES)).astype(F32)
    return cached, new


def _attn_decode_body(q_ref, kvn_ref, k0_ref, v0_ref, k1_ref, v1_ref, k2_ref, v2_ref, bc_ref, bn_ref, o_ref):
    n = ATT_N
    scale = HEAD_DIM ** -0.5 * LOG2E
    rows = lambda x: x.reshape((-1, N_HEADS_B, HEAD_DIM))

    def windows(r0, r1, r2, b):
        w0 = r0[b]
        w1 = r1[b]
        return (rows(w0),
                jnp.concatenate([rows(w1), rows(w0[:, :, 0])], axis=0),
                jnp.concatenate([r2[b], w1[:, 0], w0[:, 0, 0]], axis=0))

    for b in range(SEQS_PER_STEP):
        kn, vn = kvn_ref[b, 0], kvn_ref[b, 1]
        ks = windows(k0_ref, k1_ref, k2_ref, b)
        vs = windows(v0_ref, v1_ref, v2_ref, b)
        outs, lses = [], []
        for g in range(N_GROUPS_B):
            q = q_ref[b, g] * scale
            k, v = ks[g], vs[g]
            s = jnp.sum(k * q[None], axis=-1, keepdims=True) + bc_ref[g]
            s0 = jnp.sum(kn * q, axis=-1, keepdims=True) + bn_ref[g]
            m = jnp.maximum(jnp.max(s, axis=0), s0)
            p = jnp.exp2(s - m[None])
            p0 = jnp.exp2(s0 - m)
            den = jnp.sum(p, axis=0) + p0
            outs.append((jnp.sum(p * v, axis=0) + p0 * vn) * (1.0 / den))
            lses.append(m + jnp.log2(den))
        mx = functools.reduce(jnp.maximum, lses)
        es = [jnp.exp2(l - mx) for l in lses]
        tot = functools.reduce(lambda x, y: x + y, es)
        acc = functools.reduce(lambda x, y: x + y, [e * o for e, o in zip(es, outs)])
        o_ref[b] = (acc * (1.0 / tot)).astype(o_ref.dtype)


def attn_decode(q, kv_new, cache_k, cache_v, bias_cached, bias_new):
    b = q.shape[0]
    view = (b, N_BUF // ROW_MAJOR, ROW_MAJOR // ROW_MID, ROW_MID, N_HEADS_B, HEAD_DIM)
    ck, cv = cache_k.reshape(view), cache_v.reshape(view)
    n = ATT_N
    nq = SEQS_PER_STEP
    hd = (N_HEADS_B, HEAD_DIM)
    m0 = n // ROW_MAJOR
    m1 = (n * ROW_MID - n) // ROW_MAJOR
    m2 = N_BUF // ROW_MAJOR - m0 - m1
    assert m2 % m1 == 0 and (m1 + m2) % m0 == 0
    g0 = pl.BlockSpec((nq, m0, ROW_MAJOR // ROW_MID, ROW_MID) + hd, lambda i: (i, (m1 + m2) // m0, 0, 0, 0, 0))
    g1 = pl.BlockSpec((nq, m1, ROW_MAJOR // ROW_MID, None) + hd, lambda i: (i, m2 // m1, 0, 0, 0, 0))
    g2 = pl.BlockSpec((nq, m2, None, None) + hd, lambda i: (i, 0, 0, 0, 0, 0))
    return pl.pallas_call(
        _attn_decode_body,
        grid=(b // nq,),
        in_specs=[
            pl.BlockSpec((nq, N_GROUPS_B) + hd, lambda i: (i, 0, 0, 0)),
            pl.BlockSpec((nq, 2) + hd, lambda i: (i, 0, 0, 0)),
            g0, g0, g1, g1, g2, g2,
            pl.BlockSpec((N_GROUPS_B, n) + (N_HEADS_B, LANES), lambda i: (0, 0, 0, 0)),
            pl.BlockSpec((N_GROUPS_B, N_HEADS_B, LANES), lambda i: (0, 0, 0)),
        ],
        out_specs=pl.BlockSpec((nq,) + hd, lambda i: (i, 0, 0)),
        out_shape=jax.ShapeDtypeStruct((b,) + hd, BF16),
        compiler_params=_cparams(("arbitrary",)),
        name="attn_decode",
    )(q, kv_new, ck, cv, ck, cv, ck, cv, bias_cached, bias_new)


Z_TILE = 1024


def _pad_lanes(v):
    return jnp.pad(v.astype(F32), (0, LANES - v.shape[0]))[None]


TM = 1024
TM_LONG_K = 512
TM_GLU = 2048
TM_EW = 256


def _trunk(xs, adas, mixers_a, mixers_b, p):
    tms = (TM_EW, xs[1].shape[0])
    vec = lambda a, l, s: a[l, s][None]
    sh1, sc1, g1, sh2, sc2, g2 = zip(*(a[0] for a in adas))
    sh1b, sc1b, g1b, sh2b, sc2b, g2b = zip(*(a[1] for a in adas))
    shkv, sckv = zip(*(a[2] for a in adas))
    paths = range(2)

    def ln(x, y, gate, l, s, mods):
        outs = [post_ln(x[i], y[i], gate[i], vec(p["ln_g"], l, s), vec(p["ln_b"], l, s),
                        [(sc[i], sh[i]) for sc, sh in mods], tm=tms[i]) for i in paths]
        return zip(*outs)

    def mm(a, w, **kw):
        tm = kw.pop("tm", TM)
        return matmul(a[0], w, rider=a[1], tm=tm, **kw)

    u = [modulate(xs[i], sc1[i], sh1[i], tm=tms[i]) for i in paths]
    w_in_t = jnp.swapaxes(p["ssm_w_in"], 1, 2).reshape(-1, D_MODEL)
    sz = mm(u, w_in_t, w_transposed=True, out_act="silu", tn=Z_TILE, n_out=D_INNER, name="in_z")
    xbc = mm(u, w_in_t, w_transposed=True, tn=Z_TILE, n_out=CONV_DIM, col_off=D_INNER // Z_TILE, name="in_xbc")
    dt = mm(u, w_in_t, w_transposed=True, tn=LANES, n_out=LANES, col_off=(D_INNER + CONV_DIM) // LANES, name="in_dt")
    gated, extras = zip(*(mixers_a[i](sz[i], xbc[i], dt[i]) for i in paths))
    y = mm(gated, p["ssm_w_out"].reshape(D_INNER, D_MODEL), tn=512, out_dtype=BF16, name="ssm_out")
    x, u = ln(xs, y, g1, 0, 0, [(sc2, sh2)])
    h = glu_matmul(u[0], u[1], p["ffn_w_gate"], p["ffn_w_up"], layer=0, tm=TM_GLU, tn=512, name="ffn0_glu")
    y = mm(h, p["ffn_w_down"], layer=0, tm=TM_LONG_K, tn=512, out_dtype=BF16, name="ffn0_down")
    x, u, ukv = ln(x, y, g2, 0, 1, [(sc1b, sh1b), (sckv, shkv)])

    kv = mm(ukv, p["w_kv"], tn=1024, name="kv_proj")
    q = mm(u, p["attn_w_q"], layer=0, tn=1024, name="q_proj")
    o = [mixers_b[i](q[i], kv[i]) for i in paths]
    y = mm(o, p["attn_w_o"], layer=0, tn=1024, out_dtype=BF16, name="attn_out")
    x, u = ln(x, y, g1b, 1, 0, [(sc2b, sh2b)])
    h = glu_matmul(u[0], u[1], p["ffn_w_gate"], p["ffn_w_up"], layer=1, tm=TM_GLU, tn=512, name="ffn1_glu")
    y = mm(h, p["ffn_w_down"], layer=1, tm=TM_LONG_K, tn=512, out_dtype=BF16, name="ffn1_down")
    (x,) = ln(x, y, g2b, 1, 1, [])
    return [(x[i], extras[i], kv[i]) for i in paths]


def kernel(x_prompt, x_sample, state_conv, state_ssm, cache_k, cache_v, c_prompt, c_sample, w_ada, b_ada, ln_g, ln_b, ffn_w_gate, ffn_w_up, ffn_w_down, ssm_w_in, ssm_conv_w, ssm_conv_b, ssm_dt_bias, ssm_a_log, ssm_d, ssm_norm_w, ssm_w_out, kv_w_ada, kv_b_ada, w_kv, attn_w_q, attn_w_o, rel_bias):
    p = dict(ln_g=ln_g, ln_b=ln_b, ffn_w_gate=ffn_w_gate, ffn_w_up=ffn_w_up, ffn_w_down=ffn_w_down,
             ssm_w_in=ssm_w_in, ssm_w_out=ssm_w_out, w_kv=w_kv, attn_w_q=attn_w_q, attn_w_o=attn_w_o)
    s, b, d = SEQ, DEC_BATCH, D_MODEL

    c_all = jnp.concatenate([c_sample, c_prompt, jnp.zeros((ADA_ROWS - b - 1, d), F32)], axis=0)
    ada = [matmul(c_all, w_ada, layer=l, tm=ADA_ROWS, tn=1024, bias=b_ada[l][None], act="silu", name=f"ada{l}")
           for l in range(DEPTH)]
    adakv = matmul(c_all, kv_w_ada, tm=ADA_ROWS, tn=1024, bias=kv_b_ada[None], act="silu", name="ada_kv")
    split = lambda table, n, prompt: [Mod(table, i, prompt) for i in range(n)]

    conv_w, conv_b = ssm_conv_w[0], ssm_conv_b[0][None]
    dt_bias, a_log = _pad_lanes(ssm_dt_bias[0]), _pad_lanes(ssm_a_log[0])
    d_exp = jnp.repeat(ssm_d[0].astype(F32), SSM_HEAD_DIM)[None]
    norm_w = ssm_norm_w[0][None]

    def mixer_a_prompt(sz, xbc, dt):
        gated, hfin = ssd_prompt(sz, xbc, dt, conv_w, conv_b, dt_bias, a_log, d_exp, norm_w)
        return gated, (xbc[s - (CONV_WIDTH - 1):], hfin)

    bias_rows = _prompt_bias_rows(rel_bias)
    mixer_b_prompt = lambda q, kv: attn_prompt(q, kv, bias_rows)

    expand_mat = (lax.broadcasted_iota(jnp.int32, (LANES, D_INNER), 0)
                  == lax.broadcasted_iota(jnp.int32, (LANES, D_INNER), 1) // SSM_HEAD_DIM).astype(BF16)

    def mixer_a_decode(sz, xbc, dt):
        st = state_conv[0]
        xa = decode_conv(st.reshape(b, (CONV_WIDTH - 1) * CONV_DIM), xbc, conv_w, conv_b)
        dtx, daexp, bcx = decode_prep(xa, dt, dt_bias, a_log, expand_mat)
        hnew, hc = decode_state(state_ssm.reshape(b, STATE_ROWS, D_STATE), dtx[:, None], daexp[:, None], xa[:, None])
        gated = decode_post(hc[:, 0], daexp, bcx, xa, sz, d_exp, norm_w)
        new_conv = jnp.concatenate([st[:, 1:], xbc[:, None]], axis=1)
        return gated, (new_conv, hnew)

    bias_cached, bias_new = _decode_bias_tables(rel_bias)

    def mixer_b_decode(q, kv):
        o = attn_decode(q.reshape(b, N_GROUPS_B, N_HEADS_B, HEAD_DIM), kv.reshape(b, 2, N_HEADS_B, HEAD_DIM),
                        cache_k, cache_v, bias_cached, bias_new)
        return o.reshape(b, KV_DIM)

    adas = [(split(ada[0], 6, prompt), split(ada[1], 6, prompt), split(adakv, 2, prompt)) for prompt in (True, False)]
    (y_p, (conv_p, ssm_p), kv_p), (y_s, (conv_s, ssm_s), kv_s) = _trunk(
        (x_prompt.reshape(s, d), x_sample.reshape(b, d)), adas,
        (mixer_a_prompt, mixer_a_decode), (mixer_b_prompt, mixer_b_decode), p)

    n_keep = min(WINDOW_MAX, s)
    heads = (N_HEADS_B, HEAD_DIM)
    return (
        y_p.reshape(1, s, d),
        y_s.reshape(b, 1, d),
        conv_p.reshape(1, 1, CONV_WIDTH - 1, CONV_DIM),
        ssm_p.reshape(1, 1, SSM_HEADS, SSM_HEAD_DIM, D_STATE),
        kv_p[s - n_keep:, :KV_DIM].reshape((1, n_keep) + heads),
        kv_p[s - n_keep:, KV_DIM:].reshape((1, n_keep) + heads),
        conv_s.reshape(1, b, CONV_WIDTH - 1, CONV_DIM),
        ssm_s.reshape(1, b, SSM_HEADS, SSM_HEAD_DIM, D_STATE),
        kv_s[:, :KV_DIM].reshape((b, 1) + heads),
        kv_s[:, KV_DIM:].reshape((b, 1) + heads),
    )
```

```python
import functools
import math

import jax
import jax.numpy as jnp
from jax import lax
from jax.experimental import pallas as pl
from jax.experimental.pallas import tpu as pltpu

F32 = jnp.float32
BF16 = jnp.bfloat16

D_MODEL = 2048
SEQ = 8192
DEPTH = 2
DEC_BATCH = 128
PAST_LEN = 2048
D_INNER = 2 * D_MODEL
SSM_HEAD_DIM = 64
SSM_HEADS = D_INNER // SSM_HEAD_DIM
SSM_GROUPS = 8
HEADS_PER_GROUP = SSM_HEADS // SSM_GROUPS
D_STATE = 128
CONV_WIDTH = 4
CONV_DIM = D_INNER + 2 * SSM_GROUPS * D_STATE
SSD_CHUNK = 128
RMS_EPS = 1e-5
HEAD_DIM = 128
N_HEADS_B = D_MODEL // HEAD_DIM
DILATION_GROUPS = ((128, 1), (512, 4), (2048, 16))
N_GROUPS_B = len(DILATION_GROUPS)
WINDOW_MAX = max(w for w, _ in DILATION_GROUPS)
Q_DIM = N_GROUPS_B * N_HEADS_B * HEAD_DIM
KV_DIM = N_HEADS_B * HEAD_DIM
NUM_BUCKETS = 32
MAX_DISTANCE = WINDOW_MAX
D_FF = -(-8 * D_MODEL // (3 * 256)) * 256
ALPHA = (2 * DEPTH) ** 0.25
LN_EPS = 1e-5

LANES = 128
SUBLANES = 8
VMEM_LIMIT_BYTES = 56 * 1024 * 1024
NEG_BIG = -1e30
LOG2E = math.log2(math.e)


def _cparams(semantics):
    return pltpu.CompilerParams(dimension_semantics=semantics, vmem_limit_bytes=VMEM_LIMIT_BYTES)


def _silu(x):
    return x * (1.0 / (1.0 + jnp.exp2(x * -LOG2E)))


def _split3(x):
    hi = x.astype(BF16)
    r1 = x - hi.astype(F32)
    mid = r1.astype(BF16)
    lo = (r1 - mid.astype(F32)).astype(BF16)
    return hi, mid, lo


CAST_ROWS = 256


def _cast_tile(w_ref, wbf_ref, transposed=False):
    k, tn = wbf_ref.shape
    if transposed:
        step = min(CAST_ROWS, tn)
        for c in range(tn // step):
            wbf_ref[:, c * step:(c + 1) * step] = w_ref[c * step:(c + 1) * step, :].T.astype(BF16)
        return
    rows = CAST_ROWS if k % CAST_ROWS == 0 else k
    def body(r, c):
        off = pl.multiple_of(r * rows, rows)
        wbf_ref[pl.ds(off, rows), :] = w_ref[pl.ds(off, rows), :].astype(BF16)
        return c
    lax.fori_loop(0, k // rows, body, 0)


def _mm_body(*refs, act, out_act, has_bias, has_rider, w_transposed):
    refs = list(refs)
    a_ref, w_ref = refs.pop(0), refs.pop(0)
    b_ref = refs.pop(0) if has_bias else None
    a2_ref = refs.pop(0) if has_rider else None
    o_ref = refs.pop(0)
    o2_ref = refs.pop(0) if has_rider else None
    wbf_ref = refs.pop(0)

    def product(lhs_ref):
        a = lhs_ref[...]
        if act == "silu":
            a = _silu(a.astype(F32))
        acc = jnp.dot(a.astype(BF16), wbf_ref[...], preferred_element_type=F32)
        if has_bias:
            acc = acc + b_ref[...]
        return _silu(acc) if out_act == "silu" else acc

    @pl.when(pl.program_id(1) == 0)
    def _():
        _cast_tile(w_ref, wbf_ref, w_transposed)
        if has_rider:
            o2_ref[...] = product(a2_ref).astype(o2_ref.dtype)

    o_ref[...] = product(a_ref).astype(o_ref.dtype)


def _weight_spec(w, layer, tn, col_off=0, transposed=False):
    if transposed:
        return pl.BlockSpec((tn, w.shape[1]), lambda j, i: (j + col_off, 0))
    k = w.shape[-2]
    if w.ndim == 2:
        return pl.BlockSpec((k, tn), lambda j, i: (0, j + col_off))
    return pl.BlockSpec((None, k, tn), lambda j, i: (layer, 0, j + col_off))


def matmul(a, w, *, tm, tn, layer=0, n_out=None, col_off=0, bias=None, act=None, out_act=None, rider=None,
           w_transposed=False, out_dtype=F32, name="mm"):
    m, k = a.shape
    n_out = (w.shape[0] if w_transposed else w.shape[-1]) if n_out is None else n_out
    grid = (pl.cdiv(n_out, tn), m // tm)
    in_specs = [
        pl.BlockSpec((tm, k), lambda j, i: (i, 0)),
        _weight_spec(w, layer, tn, col_off, w_transposed),
    ]
    args = [a, w]
    out_specs = [pl.BlockSpec((tm, tn), lambda j, i: (i, j))]
    out_shape = [jax.ShapeDtypeStruct((m, n_out), out_dtype)]
    if bias is not None:
        in_specs.append(pl.BlockSpec((1, tn), lambda j, i: (0, j + col_off)))
        args.append(bias)
    if rider is not None:
        m2 = rider.shape[0]
        in_specs.append(pl.BlockSpec((m2, k), lambda j, i: (0, 0)))
        args.append(rider)
        out_specs.append(pl.BlockSpec((m2, tn), lambda j, i: (0, j)))
        out_shape.append(jax.ShapeDtypeStruct((m2, n_out), out_dtype))
    outs = pl.pallas_call(
        functools.partial(_mm_body, act=act, out_act=out_act, has_bias=bias is not None,
                          has_rider=rider is not None, w_transposed=w_transposed),
        grid=grid,
        in_specs=in_specs,
        out_specs=out_specs,
        out_shape=out_shape,
        scratch_shapes=[pltpu.VMEM((k, tn), BF16)],
        compiler_params=_cparams(("arbitrary", "arbitrary")),
        name=name,
    )(*args)
    return outs if rider is not None else outs[0]


def _glu_body(a_ref, wg_ref, wu_ref, a2_ref, o_ref, o2_ref, wgbf_ref, wubf_ref):
    def product(lhs_ref):
        a = lhs_ref[...]
        g = jnp.dot(a, wgbf_ref[...], preferred_element_type=F32)
        u = jnp.dot(a, wubf_ref[...], preferred_element_type=F32)
        return (_silu(g) * u).astype(BF16)

    @pl.when(pl.program_id(1) == 0)
    def _():
        _cast_tile(wg_ref, wgbf_ref)
        _cast_tile(wu_ref, wubf_ref)
        o2_ref[...] = product(a2_ref)

    o_ref[...] = product(a_ref)


def glu_matmul(a, rider, w_gate, w_up, *, tm, tn, layer=0, name="glu"):
    m, k = a.shape
    m2 = rider.shape[0]
    n = w_gate.shape[-1]
    return pl.pallas_call(
        _glu_body,
        grid=(n // tn, m // tm),
        in_specs=[
            pl.BlockSpec((tm, k), lambda j, i: (i, 0)),
            _weight_spec(w_gate, layer, tn),
            _weight_spec(w_up, layer, tn),
            pl.BlockSpec((m2, k), lambda j, i: (0, 0)),
        ],
        out_specs=[pl.BlockSpec((tm, tn), lambda j, i: (i, j)), pl.BlockSpec((m2, tn), lambda j, i: (0, j))],
        out_shape=[jax.ShapeDtypeStruct((m, n), BF16), jax.ShapeDtypeStruct((m2, n), BF16)],
        scratch_shapes=[pltpu.VMEM((k, tn), BF16), pltpu.VMEM((k, tn), BF16)],
        compiler_params=_cparams(("arbitrary", "arbitrary")),
        name=name,
    )(a, w_gate, w_up, rider)


ADA_ROWS = 136


class Mod:
    def __init__(self, table, col, prompt):
        self.table, self.col, self.prompt = table, col, prompt

    def spec(self, tm):
        col = self.col
        if self.prompt:
            return pl.BlockSpec((SUBLANES, D_MODEL), lambda i: (DEC_BATCH // SUBLANES, col))
        return pl.BlockSpec((tm, D_MODEL), lambda i: (i, col))


def _mod_rows(ref):
    return ref[0:1, :] if ref.shape[0] == SUBLANES else ref[...]


def _modulate_body(x_ref, sc_ref, sh_ref, o_ref):
    o_ref[...] = (x_ref[...] * (1.0 + _mod_rows(sc_ref)) + _mod_rows(sh_ref)).astype(o_ref.dtype)


def modulate(x, scale, shift, *, tm):
    m, d = x.shape
    return pl.pallas_call(
        _modulate_body,
        grid=(m // tm,),
        in_specs=[pl.BlockSpec((tm, d), lambda i: (i, 0)), scale.spec(tm), shift.spec(tm)],
        out_specs=pl.BlockSpec((tm, d), lambda i: (i, 0)),
        out_shape=jax.ShapeDtypeStruct((m, d), BF16),
        compiler_params=_cparams(("arbitrary",)),
        name="modulate",
    )(x, scale.table, shift.table)


def _ln_body(*refs, n_mod):
    x_ref, y_ref, gate_ref, g_ref, b_ref = refs[:5]
    mod_refs = refs[5:5 + 2 * n_mod]
    xo_ref = refs[5 + 2 * n_mod]
    u_refs = refs[6 + 2 * n_mod:]
    t = ALPHA * x_ref[...] + (1.0 + _mod_rows(gate_ref)) * y_ref[...].astype(F32)
    mu = jnp.mean(t, axis=-1, keepdims=True)
    tc = t - mu
    var = jnp.mean(tc * tc, axis=-1, keepdims=True)
    xn = tc * lax.rsqrt(var + LN_EPS) * g_ref[...] + b_ref[...]
    xo_ref[...] = xn
    for q in range(n_mod):
        u_refs[q][...] = (xn * (1.0 + _mod_rows(mod_refs[2 * q])) + _mod_rows(mod_refs[2 * q + 1])).astype(BF16)


def post_ln(x, y, gate, ln_g, ln_b, mods, *, tm):
    m, d = x.shape
    row = pl.BlockSpec((tm, d), lambda i: (i, 0))
    vec = pl.BlockSpec((1, d), lambda i: (0, 0))
    in_specs = [row, row, gate.spec(tm), vec, vec]
    args = [x, y, gate.table, ln_g, ln_b]
    for sc, sh in mods:
        in_specs += [sc.spec(tm), sh.spec(tm)]
        args += [sc.table, sh.table]
    outs = pl.pallas_call(
        functools.partial(_ln_body, n_mod=len(mods)),
        grid=(m // tm,),
        in_specs=in_specs,
        out_specs=[row] * (1 + len(mods)),
        out_shape=[jax.ShapeDtypeStruct((m, d), F32)] + [jax.ShapeDtypeStruct((m, d), BF16)] * len(mods),
        compiler_params=_cparams(("arbitrary",)),
        name="post_ln",
    )(*args)
    return outs


CONV_SLAB = 512
GROUP_X = D_INNER // SSM_GROUPS
PAIR = 2 * SSM_HEAD_DIM


def _softplus(x):
    return jnp.maximum(x, 0.0) + jnp.log1p(jnp.exp(-jnp.abs(x)))


def _dot_nt(a, b):
    return lax.dot_general(a, b, (((1,), (1,)), ((), ())), preferred_element_type=F32)


def _dot_tn(a, b):
    return lax.dot_general(a, b, (((0,), (0,)), ((), ())), preferred_element_type=F32)


def _head_spread_matrix():
    r = lax.broadcasted_iota(jnp.int32, (SSM_GROUPS, 2 * LANES, HEADS_PER_GROUP * LANES), 1)
    col = lax.broadcasted_iota(jnp.int32, (SSM_GROUPS, 2 * LANES, HEADS_PER_GROUP * LANES), 2)
    g = lax.broadcasted_iota(jnp.int32, (SSM_GROUPS, 2 * LANES, HEADS_PER_GROUP * LANES), 0)
    hit = jnp.logical_and(r % SSM_HEADS == g * HEADS_PER_GROUP + col // LANES, r < 3 * SSM_HEADS)
    return hit.astype(BF16)


def _ssd_prompt_body(sz_ref, xbc_ref, dt_ref, cw_ref, cb_ref, dtb_ref, alog_ref, dexp_ref, nw_ref, e_ref,
                     g_ref, hfin_ref,
                     ext_ref, act_ref, statet_ref, p_ref, wt_ref, rowp_ref):
    c = pl.program_id(0)
    L = SSD_CHUNK

    @pl.when(c == 0)
    def _():
        ext_ref[0:SUBLANES, :] = jnp.zeros((SUBLANES, CONV_DIM), F32)
        statet_ref[...] = jnp.zeros(statet_ref.shape, F32)

    ext_ref[SUBLANES:SUBLANES + L, :] = xbc_ref[...]
    for s in range(CONV_DIM // LANES):
        cols = slice(s * LANES, (s + 1) * LANES)
        acc = cb_ref[:, cols]
        for i in range(CONV_WIDTH):
            lo = SUBLANES - (CONV_WIDTH - 1) + i
            acc = acc + cw_ref[i:i + 1, cols] * ext_ref[lo:lo + L, cols]
        act_ref[:, cols] = _silu(acc)
    ext_ref[0:SUBLANES, :] = ext_ref[L:L + SUBLANES, :]

    lane = lax.broadcasted_iota(jnp.int32, (L, LANES), 1)
    row = lax.broadcasted_iota(jnp.int32, (L, LANES), 0)
    head_ok = lane < SSM_HEADS
    dt = jnp.where(head_ok, _softplus(dt_ref[...] + dtb_ref[...]), 0.0)
    da = dt * (-jnp.exp(alog_ref[...]))
    causal = row >= lane
    tril = jnp.where(causal, 1.0, 0.0).astype(BF16)
    cs = sum(jnp.dot(tril, p, preferred_element_type=F32) for p in _split3(da)) * LOG2E
    hi, mid, lo = _split3(cs)
    p_ref[:, 0:LANES] = jnp.where(head_ok, hi.astype(F32), pltpu.roll(mid.astype(F32), SSM_HEADS, 1)).astype(BF16)
    p_ref[:, LANES:2 * LANES] = lo
    cst = cs.T
    dtt = dt.T
    rowp_ref[...] = cst - jnp.log2(dtt)
    wt_ref[...] = dtt * jnp.exp2(jnp.broadcast_to(cst[:, L - 1:L], (LANES, L)) - cst)
    lane_lo = lane < SSM_HEAD_DIM

    b_off, c_off = D_INNER, D_INNER + SSM_GROUPS * D_STATE
    for g in range(SSM_GROUPS):
        bg = act_ref[:, b_off + g * D_STATE:b_off + (g + 1) * D_STATE]
        bg_bf = bg.astype(BF16)
        bgt = bg.T
        cg = act_ref[:, c_off + g * D_STATE:c_off + (g + 1) * D_STATE].astype(BF16)
        cb = _dot_nt(cg, bg_bf)
        yoff = jnp.dot(cg, statet_ref[g].astype(BF16), preferred_element_type=F32)
        colb = jnp.dot(p_ref[...], e_ref[g], preferred_element_type=F32)
        heads = slice(g * HEADS_PER_GROUP, (g + 1) * HEADS_PER_GROUP)
        rowpg, wtg = rowp_ref[heads, :], wt_ref[heads, :]
        gated = []
        for pr in range(HEADS_PER_GROUP // 2):
            lanes = slice(pr * PAIR, (pr + 1) * PAIR)
            cols = slice(g * GROUP_X + pr * PAIR, g * GROUP_X + (pr + 1) * PAIR)
            x2 = act_ref[:, cols]
            ms, ecols, ss = [], [], []
            for k in (2 * pr, 2 * pr + 1):
                col = colb[:, k * LANES:(k + 1) * LANES]
                rowv = jnp.broadcast_to(rowpg[k:k + 1, :], (L, LANES))
                lmat = jnp.exp2(jnp.where(causal, col - rowv, NEG_BIG))
                ms.append((cb * lmat).astype(BF16))
                ecols.append(jnp.exp2(col))
                ss.append((bgt * jnp.broadcast_to(wtg[k:k + 1, :], (LANES, L))).astype(BF16))
            rhs = jnp.concatenate([jnp.where(lane_lo, x2, 0.0), jnp.where(lane_lo, 0.0, x2)], axis=0).astype(BF16)
            ydiag = jnp.dot(jnp.concatenate(ms, axis=1), rhs, preferred_element_type=F32)
            escale = jnp.where(lane_lo, ecols[0], ecols[1])
            y = ydiag + yoff[:, lanes] * escale + dexp_ref[:, cols] * x2
            gated.append(y * sz_ref[:, cols])
            snew = jnp.dot(jnp.concatenate(ss, axis=1), rhs, preferred_element_type=F32)
            decay = jnp.broadcast_to(escale[L - 1:L, :], (LANES, LANES))
            statet_ref[g, :, lanes] = statet_ref[g, :, lanes] * decay + snew
        sq = functools.reduce(lambda a, b: a + b, [t * t for t in gated])
        inv = lax.rsqrt(jnp.sum(sq, axis=-1, keepdims=True) * (1.0 / GROUP_X) + RMS_EPS)
        for pr, t in enumerate(gated):
            cols = slice(g * GROUP_X + pr * PAIR, g * GROUP_X + (pr + 1) * PAIR)
            g_ref[:, cols] = (t * inv * nw_ref[:, cols]).astype(g_ref.dtype)

    @pl.when(c == pl.num_programs(0) - 1)
    def _():
        for g in range(SSM_GROUPS):
            for q in range(GROUP_X // LANES):
                blk = slice(q * LANES, (q + 1) * LANES)
                hfin_ref[g, blk, :] = statet_ref[g, :, blk].T


def ssd_prompt(sz, xbc, dt_raw, conv_w, conv_b, dt_bias, a_log, d_exp, norm_w):
    s = sz.shape[0]
    L = SSD_CHUNK
    vec = lambda n: pl.BlockSpec((1, n), lambda c: (0, 0))
    return pl.pallas_call(
        _ssd_prompt_body,
        grid=(s // L,),
        in_specs=[
            pl.BlockSpec((L, D_INNER), lambda c: (c, 0)),
            pl.BlockSpec((L, CONV_DIM), lambda c: (c, 0)),
            pl.BlockSpec((L, LANES), lambda c: (c, 0)),
            pl.BlockSpec((CONV_WIDTH, CONV_DIM), lambda c: (0, 0)),
            vec(CONV_DIM), vec(LANES), vec(LANES), vec(D_INNER), vec(D_INNER),
            pl.BlockSpec((SSM_GROUPS, 2 * LANES, HEADS_PER_GROUP * LANES), lambda c: (0, 0, 0)),
        ],
        out_specs=[
            pl.BlockSpec((L, D_INNER), lambda c: (c, 0)),
            pl.BlockSpec((SSM_GROUPS, GROUP_X, D_STATE), lambda c: (0, 0, 0)),
        ],
        out_shape=[
            jax.ShapeDtypeStruct((s, D_INNER), BF16),
            jax.ShapeDtypeStruct((SSM_GROUPS, GROUP_X, D_STATE), F32),
        ],
        scratch_shapes=[
            pltpu.VMEM((L + 2 * SUBLANES, CONV_DIM), F32),
            pltpu.VMEM((L, CONV_DIM), F32),
            pltpu.VMEM((SSM_GROUPS, D_STATE, GROUP_X), F32),
            pltpu.VMEM((L, 2 * LANES), BF16),
            pltpu.VMEM((LANES, L), F32),
            pltpu.VMEM((LANES, L), F32),
        ],
        compiler_params=_cparams(("arbitrary",)),
        name="ssd_prompt",
    )(sz, xbc, dt_raw, conv_w, conv_b, dt_bias, a_log, d_exp, norm_w, _head_spread_matrix())


def _decode_conv_body(s0_ref, s1_ref, s2_ref, xn_ref, cw_ref, cb_ref, o_ref):
    acc = cb_ref[...] + cw_ref[0:1, :] * s0_ref[...] + cw_ref[1:2, :] * s1_ref[...]
    acc = acc + cw_ref[2:3, :] * s2_ref[...] + cw_ref[3:4, :] * xn_ref[...]
    o_ref[...] = _silu(acc)


def decode_conv(s0, s1, s2, xnew, conv_w, conv_b):
    b, n = xnew.shape
    blk = pl.BlockSpec((b, CONV_SLAB), lambda j: (0, j))
    return pl.pallas_call(
        _decode_conv_body,
        grid=(n // CONV_SLAB,),
        in_specs=[blk, blk, blk, blk, pl.BlockSpec((CONV_WIDTH, CONV_SLAB), lambda j: (0, j)),
                  pl.BlockSpec((1, CONV_SLAB), lambda j: (0, j))],
        out_specs=blk,
        out_shape=jax.ShapeDtypeStruct((b, n), F32),
        compiler_params=_cparams(("arbitrary",)),
        name="decode_conv",
    )(s0, s1, s2, xnew, conv_w, conv_b)


def _decode_prep_body(xs_ref, b_ref, c_ref, dt_ref, dtb_ref, alog_ref, e_ref, dtx_ref, daexp_ref, bcx_ref):
    lane = lax.broadcasted_iota(jnp.int32, dt_ref.shape, 1)
    dt = jnp.where(lane < SSM_HEADS, _softplus(dt_ref[...] + dtb_ref[...]), 0.0)
    da = jnp.where(lane < SSM_HEADS, jnp.exp(dt * (-jnp.exp(alog_ref[...]))), 0.0)
    e = e_ref[...]
    expand = lambda v: sum(jnp.dot(p, e, preferred_element_type=F32) for p in _split3(v))
    dtx = expand(dt) * xs_ref[...]
    dtx_ref[...] = dtx
    daexp_ref[...] = expand(da)
    bcx_ref[...] = dtx * jnp.sum(b_ref[...] * c_ref[...], axis=-1, keepdims=True)


def decode_prep(xbc_act, dt_raw, dt_bias, a_log, expand_mat):
    b = xbc_act.shape[0]
    xblk = pl.BlockSpec((b, GROUP_X), lambda g: (0, g))
    vec = pl.BlockSpec((1, LANES), lambda g: (0, 0))
    nb = D_INNER // D_STATE
    out = jax.ShapeDtypeStruct((b, D_INNER), F32)
    return pl.pallas_call(
        _decode_prep_body,
        grid=(SSM_GROUPS,),
        in_specs=[
            xblk,
            pl.BlockSpec((b, D_STATE), lambda g: (0, nb + g)),
            pl.BlockSpec((b, D_STATE), lambda g: (0, nb + SSM_GROUPS + g)),
            pl.BlockSpec((b, LANES), lambda g: (0, 0)),
            vec, vec,
            pl.BlockSpec((LANES, GROUP_X), lambda g: (0, g)),
        ],
        out_specs=[xblk, xblk, xblk],
        out_shape=[out, out, out],
        compiler_params=_cparams(("arbitrary",)),
        name="decode_prep",
    )(xbc_act, xbc_act, xbc_act, dt_raw, dt_bias, a_log, expand_mat)


TILE_ROWS = 128
SPLIT_ROWS = 16


SEQS_PER_STEP = 2


STATE_ROWS = SSM_HEADS * SSM_HEAD_DIM
STATE_SEQS_PER_STEP = 4


def _decode_state_body(h_ref, dtx_ref, da_ref, b_ref, c_ref, ho_ref, hc_ref):
    rid = lambda w: lax.broadcasted_iota(jnp.int32, (SPLIT_ROWS, w), 0)
    pick = lambda r, pieces, w: functools.reduce(
        lambda acc, kv: jnp.where(r == kv[0], jnp.broadcast_to(kv[1].astype(F32), (SPLIT_ROWS, w)), acc),
        pieces, jnp.zeros((SPLIT_ROWS, w), F32))
    r = rid(D_INNER)
    rb = rid(SSM_GROUPS * D_STATE)
    r1 = rid(D_STATE)
    rd = jnp.where(jnp.logical_and(r1 >= 6, r1 <= 8), 1.0, 0.0).astype(BF16)
    for q in range(STATE_SEQS_PER_STEP):
        xh, xm, xl = _split3(dtx_ref[q])
        ah, am, al = _split3(da_ref[q])
        bh, bm, bl = _split3(b_ref[q])
        lmat = pick(r, [(0, xh), (1, xh), (2, xh), (3, xm), (4, xm), (5, xl), (6, ah), (7, am), (8, al)], D_INNER)
        ru = pick(rb, [(0, bh), (1, bm), (2, bl), (3, bh), (4, bm), (5, bh)], SSM_GROUPS * D_STATE).astype(BF16)
        crow = c_ref[q]
        for i in range(STATE_ROWS // TILE_ROWS):
            g = (i * TILE_ROWS) // GROUP_X
            lanes = slice(i * TILE_ROWS, (i + 1) * TILE_ROWS)
            lt = lmat[:, lanes].T.astype(BF16)
            upd = jnp.dot(lt, ru[:, g * D_STATE:(g + 1) * D_STATE], preferred_element_type=F32)
            dec = jnp.dot(lt, rd, preferred_element_type=F32)
            h = h_ref[q, lanes, :]
            ho_ref[q, lanes, :] = h * dec + upd
            t = (h * crow[:, g * D_STATE:(g + 1) * D_STATE]).T
            hc_ref[q, :, lanes] = jnp.sum(t, axis=0, keepdims=True)


def decode_state(h, dtx, daexp, xbc_act3):
    b = h.shape[0]
    nq = STATE_SEQS_PER_STEP
    hblk = pl.BlockSpec((nq, STATE_ROWS, D_STATE), lambda i: (i, 0, 0))
    rblk = pl.BlockSpec((nq, 1, D_INNER), lambda i: (i, 0, 0))
    bc = SSM_GROUPS * D_STATE
    return pl.pallas_call(
        _decode_state_body,
        grid=(b // nq,),
        in_specs=[
            hblk, rblk, rblk,
            pl.BlockSpec((nq, 1, bc), lambda i: (i, 0, D_INNER // bc)),
            pl.BlockSpec((nq, 1, bc), lambda i: (i, 0, D_INNER // bc + 1)),
        ],
        out_specs=[hblk, rblk],
        out_shape=[jax.ShapeDtypeStruct((b, STATE_ROWS, D_STATE), F32), jax.ShapeDtypeStruct((b, 1, D_INNER), F32)],
        compiler_params=_cparams(("arbitrary",)),
        name="decode_state",
    )(h, dtx, daexp, xbc_act3, xbc_act3)


def _decode_post_body(hc_ref, da_ref, bcx_ref, xs_ref, sz_ref, dexp_ref, nw_ref, o_ref):
    y = da_ref[...] * hc_ref[...] + bcx_ref[...] + dexp_ref[...] * xs_ref[...]
    gt = y * sz_ref[...]
    ms = jnp.mean(gt * gt, axis=-1, keepdims=True)
    o_ref[...] = (gt * lax.rsqrt(ms + RMS_EPS) * nw_ref[...]).astype(o_ref.dtype)


def decode_post(hc, daexp, bcx, xbc_act, sz, d_exp, norm_w):
    b = hc.shape[0]
    xblk = pl.BlockSpec((b, GROUP_X), lambda g: (0, g))
    vblk = pl.BlockSpec((1, GROUP_X), lambda g: (0, g))
    return pl.pallas_call(
        _decode_post_body,
        grid=(SSM_GROUPS,),
        in_specs=[xblk, xblk, xblk, xblk, xblk, vblk, vblk],
        out_specs=xblk,
        out_shape=jax.ShapeDtypeStruct((b, D_INNER), BF16),
        compiler_params=_cparams(("arbitrary",)),
        name="decode_post",
    )(hc, daexp, bcx, xbc_act, sz, d_exp, norm_w)


SUPER = WINDOW_MAX
ATT_N = 128
MERGE_ROWS = 256


def _rel_bucket(dist):
    max_exact = NUM_BUCKETS // 2
    df = jnp.maximum(dist, max_exact).astype(F32)
    large = max_exact + (jnp.log(df / max_exact) / math.log(MAX_DISTANCE / max_exact)
                         * (NUM_BUCKETS - max_exact)).astype(jnp.int32)
    return jnp.where(dist < max_exact, dist, jnp.minimum(large, NUM_BUCKETS - 1))


def _prompt_bias_rows(rel_bias):
    n = ATT_N
    back = n - jnp.arange(2 * n)
    rows = []
    for g, (_, dil) in enumerate(DILATION_GROUPS):
        tab = rel_bias[:, g * N_HEADS_B:(g + 1) * N_HEADS_B]
        bias = tab[_rel_bucket(jnp.clip(back, 0, n) * dil)].astype(F32) * LOG2E
        rows.append(jnp.where((back >= 0)[:, None], bias, NEG_BIG).T)
    return jnp.stack(rows, axis=0)[:, :, None, :]


def _rows(start, size, stride):
    return pl.ds(start, size) if stride == 1 else pl.ds(start, size, stride=stride)


def _attn_prompt_body(q0_ref, q1_ref, q2_ref, kc_ref, kp_ref, vc_ref, vp_ref, t_ref, o_ref, og_ref, lse_ref):
    first = pl.program_id(0) == 0
    n = ATT_N
    scale = HEAD_DIM ** -0.5 * LOG2E
    key_id = lax.broadcasted_iota(jnp.int32, (n, 2 * n), 1)
    q_refs = (q0_ref, q1_ref, q2_ref)
    for g, (win, dil) in enumerate(DILATION_GROUPS):
        nblk = SUPER // (n * dil)
        bias = pltpu.roll(jnp.broadcast_to(t_ref[g, 0], (n, 2 * n)), 0, 1, stride=1, stride_axis=0)
        for r in range(dil):
            prev = _rows((nblk - 1) * n * dil + r, n, dil)
            kprev, vprev = kp_ref[prev, :].astype(BF16), vp_ref[prev, :].astype(BF16)
            for c in range(nblk):
                start = c * n * dil + r
                cur = _rows(start, n, dil)
                qb = q_refs[g][cur, :].astype(BF16)
                kcur, vcur = kc_ref[cur, :].astype(BF16), vc_ref[cur, :].astype(BF16)
                keys = jnp.concatenate([kprev, kcur], axis=0)
                vals = jnp.concatenate([vprev, vcur], axis=0)
                kprev, vprev = kcur, vcur
                s = _dot_nt(qb, keys) * scale + bias
                if c == 0:
                    s = jnp.where(jnp.logical_and(first, key_id < n), NEG_BIG, s)
                m = jnp.max(s, axis=-1, keepdims=True)
                p = jnp.exp2(s - m)
                den = jnp.sum(p, axis=-1, keepdims=True)
                o = jnp.dot(p.astype(BF16), vals, preferred_element_type=F32) * (1.0 / den)
                og_ref[g, cur, :] = o
                lse_ref[g, cur, :] = jnp.broadcast_to(m + jnp.log2(den), (n, LANES))
    for i in range(SUPER // MERGE_ROWS):
        rows = slice(i * MERGE_ROWS, (i + 1) * MERGE_ROWS)
        ls = [lse_ref[g, rows, :] for g in range(N_GROUPS_B)]
        mx = functools.reduce(jnp.maximum, ls)
        es = [jnp.exp2(l - mx) for l in ls]
        tot = functools.reduce(lambda a, b: a + b, es)
        acc = functools.reduce(lambda a, b: a + b, [es[g] * og_ref[g, rows, :] for g in range(N_GROUPS_B)])
        o_ref[rows, :] = (acc * (1.0 / tot)).astype(o_ref.dtype)


def attn_prompt(q, kv, bias_rows):
    s = q.shape[0]
    blk = lambda f: pl.BlockSpec((SUPER, HEAD_DIM), f)
    return pl.pallas_call(
        _attn_prompt_body,
        grid=(s // SUPER, N_HEADS_B),
        in_specs=[
            blk(lambda c, h: (c, h)),
            blk(lambda c, h: (c, N_HEADS_B + h)),
            blk(lambda c, h: (c, 2 * N_HEADS_B + h)),
            blk(lambda c, h: (c, h)),
            blk(lambda c, h: (jnp.maximum(c - 1, 0), h)),
            blk(lambda c, h: (c, N_HEADS_B + h)),
            blk(lambda c, h: (jnp.maximum(c - 1, 0), N_HEADS_B + h)),
            pl.BlockSpec((N_GROUPS_B, 1, 1, 2 * ATT_N), lambda c, h: (0, h, 0, 0)),
        ],
        out_specs=blk(lambda c, h: (c, h)),
        out_shape=jax.ShapeDtypeStruct((s, KV_DIM), BF16),
        scratch_shapes=[
            pltpu.VMEM((N_GROUPS_B, SUPER, HEAD_DIM), F32),
            pltpu.VMEM((N_GROUPS_B, SUPER, LANES), F32),
        ],
        compiler_params=_cparams(("arbitrary", "arbitrary")),
        name="attn_prompt",
    )(q, q, q, kv, kv, kv, kv, bias_rows)


N_BUF = min(WINDOW_MAX, PAST_LEN)
ROW_MAJOR = 16
ROW_MID = 4


def _decode_bias_tables(rel_bias):
    n = ATT_N
    j = n - jnp.arange(n)
    cached, new = [], []
    for g, (_, dil) in enumerate(DILATION_GROUPS):
        tab = rel_bias[:, g * N_HEADS_B:(g + 1) * N_HEADS_B]
        cached.append(tab[_rel_bucket(j * dil)])
        new.append(tab[_rel_bucket(jnp.zeros((), jnp.int32))])
    cached = jnp.broadcast_to(jnp.stack(cached)[..., None] * LOG2E, (N_GROUPS_B, n, N_HEADS_B, LANES)).astype(F32)
    new = jnp.broadcast_to(jnp.stack(new)[..., None] * LOG2E, (N_GROUPS_B, N_HEADS_B, LANES)).astype(F32)
    return cached, new


def _attn_decode_body(q_ref, kvn_ref, k0_ref, v0_ref, k1_ref, v1_ref, k2_ref, v2_ref, bc_ref, bn_ref, o_ref):
    n = ATT_N
    scale = HEAD_DIM ** -0.5 * LOG2E
    rows = lambda x: x.reshape((-1, N_HEADS_B, HEAD_DIM))

    def windows(r0, r1, r2, b):
        w0 = r0[b]
        w1 = r1[b]
        return (rows(w0),
                jnp.concatenate([rows(w1), rows(w0[:, :, 0])], axis=0),
                jnp.concatenate([r2[b], w1[:, 0], w0[:, 0, 0]], axis=0))

    for b in range(SEQS_PER_STEP):
        kn, vn = kvn_ref[b, 0], kvn_ref[b, 1]
        ks = windows(k0_ref, k1_ref, k2_ref, b)
        vs = windows(v0_ref, v1_ref, v2_ref, b)
        outs, lses = [], []
        for g in range(N_GROUPS_B):
            q = q_ref[b, g] * scale
            k, v = ks[g], vs[g]
            s = jnp.sum(k * q[None], axis=-1, keepdims=True) + bc_ref[g]
            s0 = jnp.sum(kn * q, axis=-1, keepdims=True) + bn_ref[g]
            m = jnp.maximum(jnp.max(s, axis=0), s0)
            p = jnp.exp2(s - m[None])
            p0 = jnp.exp2(s0 - m)
            den = jnp.sum(p, axis=0) + p0
            outs.append((jnp.sum(p * v, axis=0) + p0 * vn) * (1.0 / den))
            lses.append(m + jnp.log2(den))
        mx = functools.reduce(jnp.maximum, lses)
        es = [jnp.exp2(l - mx) for l in lses]
        tot = functools.reduce(lambda x, y: x + y, es)
        acc = functools.reduce(lambda x, y: x + y, [e * o for e, o in zip(es, outs)])
        o_ref[b] = (acc * (1.0 / tot)).astype(o_ref.dtype)


def attn_decode(q, kv_new, cache_k, cache_v, bias_cached, bias_new):
    b = q.shape[0]
    view = (b, N_BUF // ROW_MAJOR, ROW_MAJOR // ROW_MID, ROW_MID, N_HEADS_B, HEAD_DIM)
    ck, cv = cache_k.reshape(view), cache_v.reshape(view)
    n = ATT_N
    nq = SEQS_PER_STEP
    hd = (N_HEADS_B, HEAD_DIM)
    m0 = n // ROW_MAJOR
    m1 = (n * ROW_MID - n) // ROW_MAJOR
    m2 = N_BUF // ROW_MAJOR - m0 - m1
    assert m2 % m1 == 0 and (m1 + m2) % m0 == 0
    g0 = pl.BlockSpec((nq, m0, ROW_MAJOR // ROW_MID, ROW_MID) + hd, lambda i: (i, (m1 + m2) // m0, 0, 0, 0, 0))
    g1 = pl.BlockSpec((nq, m1, ROW_MAJOR // ROW_MID, None) + hd, lambda i: (i, m2 // m1, 0, 0, 0, 0))
    g2 = pl.BlockSpec((nq, m2, None, None) + hd, lambda i: (i, 0, 0, 0, 0, 0))
    return pl.pallas_call(
        _attn_decode_body,
        grid=(b // nq,),
        in_specs=[
            pl.BlockSpec((nq, N_GROUPS_B) + hd, lambda i: (i, 0, 0, 0)),
            pl.BlockSpec((nq, 2) + hd, lambda i: (i, 0, 0, 0)),
            g0, g0, g1, g1, g2, g2,
            pl.BlockSpec((N_GROUPS_B, n) + (N_HEADS_B, LANES), lambda i: (0, 0, 0, 0)),
            pl.BlockSpec((N_GROUPS_B, N_HEADS_B, LANES), lambda i: (0, 0, 0)),
        ],
        out_specs=pl.BlockSpec((nq,) + hd, lambda i: (i, 0, 0)),
        out_shape=jax.ShapeDtypeStruct((b,) + hd, BF16),
        compiler_params=_cparams(("arbitrary",)),
        name="attn_decode",
    )(q, kv_new, ck, cv, ck, cv, ck, cv, bias_cached, bias_new)


Z_TILE = 1024


def _pad_lanes(v):
    return jnp.pad(v.astype(F32), (0, LANES - v.shape[0]))[None]


TM = 1024
TM_LONG_K = 512
TM_GLU = 2048
TM_EW = 512


def _trunk(xs, adas, mixers_a, mixers_b, p):
    tms = (TM_EW, xs[1].shape[0])
    vec = lambda a, l, s: a[l, s][None]
    sh1, sc1, g1, sh2, sc2, g2 = zip(*(a[0] for a in adas))
    sh1b, sc1b, g1b, sh2b, sc2b, g2b = zip(*(a[1] for a in adas))
    shkv, sckv = zip(*(a[2] for a in adas))
    paths = range(2)

    def ln(x, y, gate, l, s, mods):
        outs = [post_ln(x[i], y[i], gate[i], vec(p["ln_g"], l, s), vec(p["ln_b"], l, s),
                        [(sc[i], sh[i]) for sc, sh in mods], tm=tms[i]) for i in paths]
        return zip(*outs)

    def mm(a, w, **kw):
        tm = kw.pop("tm", TM)
        return matmul(a[0], w, rider=a[1], tm=tm, **kw)

    u = [modulate(xs[i], sc1[i], sh1[i], tm=tms[i]) for i in paths]
    w_in_t = jnp.swapaxes(p["ssm_w_in"], 1, 2).reshape(-1, D_MODEL)
    sz = mm(u, w_in_t, w_transposed=True, out_act="silu", tn=Z_TILE, n_out=D_INNER, name="in_z")
    xbc = mm(u, w_in_t, w_transposed=True, tn=Z_TILE, n_out=CONV_DIM, col_off=D_INNER // Z_TILE, name="in_xbc")
    dt = mm(u, w_in_t, w_transposed=True, tn=LANES, n_out=LANES, col_off=(D_INNER + CONV_DIM) // LANES, name="in_dt")
    gated, extras = zip(*(mixers_a[i](sz[i], xbc[i], dt[i]) for i in paths))
    y = mm(gated, p["ssm_w_out"].reshape(D_INNER, D_MODEL), tn=512, out_dtype=BF16, name="ssm_out")
    x, u = ln(xs, y, g1, 0, 0, [(sc2, sh2)])
    h = glu_matmul(u[0], u[1], p["ffn_w_gate"], p["ffn_w_up"], layer=0, tm=TM_GLU, tn=512, name="ffn0_glu")
    y = mm(h, p["ffn_w_down"], layer=0, tm=TM_LONG_K, tn=512, out_dtype=BF16, name="ffn0_down")
    x, u, ukv = ln(x, y, g2, 0, 1, [(sc1b, sh1b), (sckv, shkv)])

    kv = mm(ukv, p["w_kv"], tn=1024, name="kv_proj")
    q = mm(u, p["attn_w_q"], layer=0, tn=1024, name="q_proj")
    o = [mixers_b[i](q[i], kv[i]) for i in paths]
    y = mm(o, p["attn_w_o"], layer=0, tn=1024, out_dtype=BF16, name="attn_out")
    x, u = ln(x, y, g1b, 1, 0, [(sc2b, sh2b)])
    h = glu_matmul(u[0], u[1], p["ffn_w_gate"], p["ffn_w_up"], layer=1, tm=TM_GLU, tn=512, name="ffn1_glu")
    y = mm(h, p["ffn_w_down"], layer=1, tm=TM_LONG_K, tn=512, out_dtype=BF16, name="ffn1_down")
    (x,) = ln(x, y, g2b, 1, 1, [])
    return [(x[i], extras[i], kv[i]) for i in paths]


def kernel(x_prompt, x_sample, state_conv, state_ssm, cache_k, cache_v, c_prompt, c_sample, w_ada, b_ada, ln_g, ln_b, ffn_w_gate, ffn_w_up, ffn_w_down, ssm_w_in, ssm_conv_w, ssm_conv_b, ssm_dt_bias, ssm_a_log, ssm_d, ssm_norm_w, ssm_w_out, kv_w_ada, kv_b_ada, w_kv, attn_w_q, attn_w_o, rel_bias):
    p = dict(ln_g=ln_g, ln_b=ln_b, ffn_w_gate=ffn_w_gate, ffn_w_up=ffn_w_up, ffn_w_down=ffn_w_down,
             ssm_w_in=ssm_w_in, ssm_w_out=ssm_w_out, w_kv=w_kv, attn_w_q=attn_w_q, attn_w_o=attn_w_o)
    s, b, d = SEQ, DEC_BATCH, D_MODEL

    c_all = jnp.concatenate([c_sample, c_prompt, jnp.zeros((ADA_ROWS - b - 1, d), F32)], axis=0)
    ada = [matmul(c_all, w_ada, layer=l, tm=ADA_ROWS, tn=1024, bias=b_ada[l][None], act="silu", name=f"ada{l}")
           for l in range(DEPTH)]
    adakv = matmul(c_all, kv_w_ada, tm=ADA_ROWS, tn=1024, bias=kv_b_ada[None], act="silu", name="ada_kv")
    split = lambda table, n, prompt: [Mod(table, i, prompt) for i in range(n)]

    conv_w, conv_b = ssm_conv_w[0], ssm_conv_b[0][None]
    dt_bias, a_log = _pad_lanes(ssm_dt_bias[0]), _pad_lanes(ssm_a_log[0])
    d_exp = jnp.repeat(ssm_d[0].astype(F32), SSM_HEAD_DIM)[None]
    norm_w = ssm_norm_w[0][None]

    def mixer_a_prompt(sz, xbc, dt):
        gated, hfin = ssd_prompt(sz, xbc, dt, conv_w, conv_b, dt_bias, a_log, d_exp, norm_w)
        return gated, (xbc[s - (CONV_WIDTH - 1):], hfin)

    bias_rows = _prompt_bias_rows(rel_bias)
    mixer_b_prompt = lambda q, kv: attn_prompt(q, kv, bias_rows)

    expand_mat = (lax.broadcasted_iota(jnp.int32, (LANES, D_INNER), 0)
                  == lax.broadcasted_iota(jnp.int32, (LANES, D_INNER), 1) // SSM_HEAD_DIM).astype(BF16)

    def mixer_a_decode(sz, xbc, dt):
        st = state_conv[0]
        xa = decode_conv(st[:, 0], st[:, 1], st[:, 2], xbc, conv_w, conv_b)
        dtx, daexp, bcx = decode_prep(xa, dt, dt_bias, a_log, expand_mat)
        hnew, hc = decode_state(state_ssm.reshape(b, STATE_ROWS, D_STATE), dtx[:, None], daexp[:, None], xa[:, None])
        gated = decode_post(hc[:, 0], daexp, bcx, xa, sz, d_exp, norm_w)
        new_conv = jnp.concatenate([st[:, 1:], xbc[:, None]], axis=1)
        return gated, (new_conv, hnew)

    bias_cached, bias_new = _decode_bias_tables(rel_bias)

    def mixer_b_decode(q, kv):
        o = attn_decode(q.reshape(b, N_GROUPS_B, N_HEADS_B, HEAD_DIM), kv.reshape(b, 2, N_HEADS_B, HEAD_DIM),
                        cache_k, cache_v, bias_cached, bias_new)
        return o.reshape(b, KV_DIM)

    adas = [(split(ada[0], 6, prompt), split(ada[1], 6, prompt), split(adakv, 2, prompt)) for prompt in (True, False)]
    (y_p, (conv_p, ssm_p), kv_p), (y_s, (conv_s, ssm_s), kv_s) = _trunk(
        (x_prompt.reshape(s, d), x_sample.reshape(b, d)), adas,
        (mixer_a_prompt, mixer_a_decode), (mixer_b_prompt, mixer_b_decode), p)

    n_keep = min(WINDOW_MAX, s)
    heads = (N_HEADS_B, HEAD_DIM)
    return (
        y_p.reshape(1, s, d),
        y_s.reshape(b, 1, d),
        conv_p.reshape(1, 1, CONV_WIDTH - 1, CONV_DIM),
        ssm_p.reshape(1, 1, SSM_HEADS, SSM_HEAD_DIM, D_STATE),
        kv_p[s - n_keep:, :KV_DIM].reshape((1, n_keep) + heads),
        kv_p[s - n_keep:, KV_DIM:].reshape((1, n_keep) + heads),
        conv_s.reshape(1, b, CONV_WIDTH - 1, CONV_DIM),
        ssm_s.reshape(1, b, SSM_HEADS, SSM_HEAD_DIM, D_STATE),
        kv_s[:, :KV_DIM].reshape((b, 1) + heads),
        kv_s[:, KV_DIM:].reshape((b, 1) + heads),
    )
```

```python
import functools
import math

import jax
import jax.numpy as jnp
from jax import lax
from jax.experimental import pallas as pl
from jax.experimental.pallas import tpu as pltpu

F32 = jnp.float32
BF16 = jnp.bfloat16

D_MODEL = 2048
SEQ = 8192
DEPTH = 2
DEC_BATCH = 128
PAST_LEN = 2048
D_INNER = 2 * D_MODEL
SSM_HEAD_DIM = 64
SSM_HEADS = D_INNER // SSM_HEAD_DIM
SSM_GROUPS = 8
HEADS_PER_GROUP = SSM_HEADS // SSM_GROUPS
D_STATE = 128
CONV_WIDTH = 4
CONV_DIM = D_INNER + 2 * SSM_GROUPS * D_STATE
SSD_CHUNK = 128
RMS_EPS = 1e-5
HEAD_DIM = 128
N_HEADS_B = D_MODEL // HEAD_DIM
DILATION_GROUPS = ((128, 1), (512, 4), (2048, 16))
N_GROUPS_B = len(DILATION_GROUPS)
WINDOW_MAX = max(w for w, _ in DILATION_GROUPS)
Q_DIM = N_GROUPS_B * N_HEADS_B * HEAD_DIM
KV_DIM = N_HEADS_B * HEAD_DIM
NUM_BUCKETS = 32
MAX_DISTANCE = WINDOW_MAX
D_FF = -(-8 * D_MODEL // (3 * 256)) * 256
ALPHA = (2 * DEPTH) ** 0.25
LN_EPS = 1e-5

LANES = 128
SUBLANES = 8
VMEM_LIMIT_BYTES = 56 * 1024 * 1024
NEG_BIG = -1e30
LOG2E = math.log2(math.e)


def _cparams(semantics):
    return pltpu.CompilerParams(dimension_semantics=semantics, vmem_limit_bytes=VMEM_LIMIT_BYTES)


def _silu(x):
    return x * (1.0 / (1.0 + jnp.exp2(x * -LOG2E)))


def _split3(x):
    hi = x.astype(BF16)
    r1 = x - hi.astype(F32)
    mid = r1.astype(BF16)
    lo = (r1 - mid.astype(F32)).astype(BF16)
    return hi, mid, lo


CAST_ROWS = 256


def _cast_tile(w_ref, wbf_ref, transposed=False):
    k, tn = wbf_ref.shape
    if transposed:
        step = min(CAST_ROWS, tn)
        for c in range(tn // step):
            wbf_ref[:, c * step:(c + 1) * step] = w_ref[c * step:(c + 1) * step, :].T.astype(BF16)
        return
    rows = CAST_ROWS if k % CAST_ROWS == 0 else k
    def body(r, c):
        off = pl.multiple_of(r * rows, rows)
        wbf_ref[pl.ds(off, rows), :] = w_ref[pl.ds(off, rows), :].astype(BF16)
        return c
    lax.fori_loop(0, k // rows, body, 0)


def _mm_body(*refs, act, out_act, has_bias, has_rider, w_transposed):
    refs = list(refs)
    a_ref, w_ref = refs.pop(0), refs.pop(0)
    b_ref = refs.pop(0) if has_bias else None
    a2_ref = refs.pop(0) if has_rider else None
    o_ref = refs.pop(0)
    o2_ref = refs.pop(0) if has_rider else None
    wbf_ref = refs.pop(0)

    def product(lhs_ref):
        a = lhs_ref[...]
        if act == "silu":
            a = _silu(a.astype(F32))
        acc = jnp.dot(a.astype(BF16), wbf_ref[...], preferred_element_type=F32)
        if has_bias:
            acc = acc + b_ref[...]
        return _silu(acc) if out_act == "silu" else acc

    @pl.when(pl.program_id(1) == 0)
    def _():
        _cast_tile(w_ref, wbf_ref, w_transposed)
        if has_rider:
            o2_ref[...] = product(a2_ref).astype(o2_ref.dtype)

    o_ref[...] = product(a_ref).astype(o_ref.dtype)


def _weight_spec(w, layer, tn, col_off=0, transposed=False):
    if transposed:
        return pl.BlockSpec((tn, w.shape[1]), lambda j, i: (j + col_off, 0))
    k = w.shape[-2]
    if w.ndim == 2:
        return pl.BlockSpec((k, tn), lambda j, i: (0, j + col_off))
    return pl.BlockSpec((None, k, tn), lambda j, i: (layer, 0, j + col_off))


def matmul(a, w, *, tm, tn, layer=0, n_out=None, col_off=0, bias=None, act=None, out_act=None, rider=None,
           w_transposed=False, out_dtype=F32, name="mm"):
    m, k = a.shape
    n_out = (w.shape[0] if w_transposed else w.shape[-1]) if n_out is None else n_out
    grid = (pl.cdiv(n_out, tn), m // tm)
    in_specs = [
        pl.BlockSpec((tm, k), lambda j, i: (i, 0)),
        _weight_spec(w, layer, tn, col_off, w_transposed),
    ]
    args = [a, w]
    out_specs = [pl.BlockSpec((tm, tn), lambda j, i: (i, j))]
    out_shape = [jax.ShapeDtypeStruct((m, n_out), out_dtype)]
    if bias is not None:
        in_specs.append(pl.BlockSpec((1, tn), lambda j, i: (0, j + col_off)))
        args.append(bias)
    if rider is not None:
        m2 = rider.shape[0]
        in_specs.append(pl.BlockSpec((m2, k), lambda j, i: (0, 0)))
        args.append(rider)
        out_specs.append(pl.BlockSpec((m2, tn), lambda j, i: (0, j)))
        out_shape.append(jax.ShapeDtypeStruct((m2, n_out), out_dtype))
    outs = pl.pallas_call(
        functools.partial(_mm_body, act=act, out_act=out_act, has_bias=bias is not None,
                          has_rider=rider is not None, w_transposed=w_transposed),
        grid=grid,
        in_specs=in_specs,
        out_specs=out_specs,
        out_shape=out_shape,
        scratch_shapes=[pltpu.VMEM((k, tn), BF16)],
        compiler_params=_cparams(("arbitrary", "arbitrary")),
        name=name,
    )(*args)
    return outs if rider is not None else outs[0]


def _glu_body(a_ref, wg_ref, wu_ref, a2_ref, o_ref, o2_ref, wgbf_ref, wubf_ref):
    def product(lhs_ref):
        a = lhs_ref[...]
        g = jnp.dot(a, wgbf_ref[...], preferred_element_type=F32)
        u = jnp.dot(a, wubf_ref[...], preferred_element_type=F32)
        return (_silu(g) * u).astype(BF16)

    @pl.when(pl.program_id(1) == 0)
    def _():
        _cast_tile(wg_ref, wgbf_ref)
        _cast_tile(wu_ref, wubf_ref)
        o2_ref[...] = product(a2_ref)

    o_ref[...] = product(a_ref)


def glu_matmul(a, rider, w_gate, w_up, *, tm, tn, layer=0, name="glu"):
    m, k = a.shape
    m2 = rider.shape[0]
    n = w_gate.shape[-1]
    return pl.pallas_call(
        _glu_body,
        grid=(n // tn, m // tm),
        in_specs=[
            pl.BlockSpec((tm, k), lambda j, i: (i, 0)),
            _weight_spec(w_gate, layer, tn),
            _weight_spec(w_up, layer, tn),
            pl.BlockSpec((m2, k), lambda j, i: (0, 0)),
        ],
        out_specs=[pl.BlockSpec((tm, tn), lambda j, i: (i, j)), pl.BlockSpec((m2, tn), lambda j, i: (0, j))],
        out_shape=[jax.ShapeDtypeStruct((m, n), BF16), jax.ShapeDtypeStruct((m2, n), BF16)],
        scratch_shapes=[pltpu.VMEM((k, tn), BF16), pltpu.VMEM((k, tn), BF16)],
        compiler_params=_cparams(("arbitrary", "arbitrary")),
        name=name,
    )(a, w_gate, w_up, rider)


ADA_ROWS = 136


class Mod:
    def __init__(self, table, col, prompt):
        self.table, self.col, self.prompt = table, col, prompt

    def spec(self, tm):
        col = self.col
        if self.prompt:
            return pl.BlockSpec((SUBLANES, D_MODEL), lambda i: (DEC_BATCH // SUBLANES, col))
        return pl.BlockSpec((tm, D_MODEL), lambda i: (i, col))


def _mod_rows(ref):
    return ref[0:1, :] if ref.shape[0] == SUBLANES else ref[...]


def _modulate_body(x_ref, sc_ref, sh_ref, o_ref):
    o_ref[...] = (x_ref[...] * (1.0 + _mod_rows(sc_ref)) + _mod_rows(sh_ref)).astype(o_ref.dtype)


def modulate(x, scale, shift, *, tm):
    m, d = x.shape
    return pl.pallas_call(
        _modulate_body,
        grid=(m // tm,),
        in_specs=[pl.BlockSpec((tm, d), lambda i: (i, 0)), scale.spec(tm), shift.spec(tm)],
        out_specs=pl.BlockSpec((tm, d), lambda i: (i, 0)),
        out_shape=jax.ShapeDtypeStruct((m, d), BF16),
        compiler_params=_cparams(("arbitrary",)),
        name="modulate",
    )(x, scale.table, shift.table)


def _ln_body(*refs, n_mod):
    x_ref, y_ref, gate_ref, g_ref, b_ref = refs[:5]
    mod_refs = refs[5:5 + 2 * n_mod]
    xo_ref = refs[5 + 2 * n_mod]
    u_refs = refs[6 + 2 * n_mod:]
    t = ALPHA * x_ref[...] + (1.0 + _mod_rows(gate_ref)) * y_ref[...].astype(F32)
    mu = jnp.mean(t, axis=-1, keepdims=True)
    tc = t - mu
    var = jnp.mean(tc * tc, axis=-1, keepdims=True)
    xn = tc * lax.rsqrt(var + LN_EPS) * g_ref[...] + b_ref[...]
    xo_ref[...] = xn
    for q in range(n_mod):
        u_refs[q][...] = (xn * (1.0 + _mod_rows(mod_refs[2 * q])) + _mod_rows(mod_refs[2 * q + 1])).astype(BF16)


def post_ln(x, y, gate, ln_g, ln_b, mods, *, tm):
    m, d = x.shape
    row = pl.BlockSpec((tm, d), lambda i: (i, 0))
    vec = pl.BlockSpec((1, d), lambda i: (0, 0))
    in_specs = [row, row, gate.spec(tm), vec, vec]
    args = [x, y, gate.table, ln_g, ln_b]
    for sc, sh in mods:
        in_specs += [sc.spec(tm), sh.spec(tm)]
        args += [sc.table, sh.table]
    outs = pl.pallas_call(
        functools.partial(_ln_body, n_mod=len(mods)),
        grid=(m // tm,),
        in_specs=in_specs,
        out_specs=[row] * (1 + len(mods)),
        out_shape=[jax.ShapeDtypeStruct((m, d), F32)] + [jax.ShapeDtypeStruct((m, d), BF16)] * len(mods),
        compiler_params=_cparams(("arbitrary",)),
        name="post_ln",
    )(*args)
    return outs


CONV_SLAB = 512
GROUP_X = D_INNER // SSM_GROUPS
PAIR = 2 * SSM_HEAD_DIM


def _softplus(x):
    return jnp.maximum(x, 0.0) + jnp.log1p(jnp.exp(-jnp.abs(x)))


def _dot_nt(a, b):
    return lax.dot_general(a, b, (((1,), (1,)), ((), ())), preferred_element_type=F32)


def _dot_tn(a, b):
    return lax.dot_general(a, b, (((0,), (0,)), ((), ())), preferred_element_type=F32)


def _head_spread_matrix():
    r = lax.broadcasted_iota(jnp.int32, (SSM_GROUPS, 2 * LANES, HEADS_PER_GROUP * LANES), 1)
    col = lax.broadcasted_iota(jnp.int32, (SSM_GROUPS, 2 * LANES, HEADS_PER_GROUP * LANES), 2)
    g = lax.broadcasted_iota(jnp.int32, (SSM_GROUPS, 2 * LANES, HEADS_PER_GROUP * LANES), 0)
    hit = jnp.logical_and(r % SSM_HEADS == g * HEADS_PER_GROUP + col // LANES, r < 3 * SSM_HEADS)
    return hit.astype(BF16)


def _ssd_prompt_body(sz_ref, xbc_ref, dt_ref, cw_ref, cb_ref, dtb_ref, alog_ref, dexp_ref, nw_ref, e_ref,
                     g_ref, hfin_ref,
                     ext_ref, act_ref, statet_ref, p_ref, wt_ref, rowp_ref):
    c = pl.program_id(0)
    L = SSD_CHUNK

    @pl.when(c == 0)
    def _():
        ext_ref[0:SUBLANES, :] = jnp.zeros((SUBLANES, CONV_DIM), F32)
        statet_ref[...] = jnp.zeros(statet_ref.shape, F32)

    ext_ref[SUBLANES:SUBLANES + L, :] = xbc_ref[...]
    for s in range(CONV_DIM // LANES):
        cols = slice(s * LANES, (s + 1) * LANES)
        acc = cb_ref[:, cols]
        for i in range(CONV_WIDTH):
            lo = SUBLANES - (CONV_WIDTH - 1) + i
            acc = acc + cw_ref[i:i + 1, cols] * ext_ref[lo:lo + L, cols]
        act_ref[:, cols] = _silu(acc)
    ext_ref[0:SUBLANES, :] = ext_ref[L:L + SUBLANES, :]

    lane = lax.broadcasted_iota(jnp.int32, (L, LANES), 1)
    row = lax.broadcasted_iota(jnp.int32, (L, LANES), 0)
    head_ok = lane < SSM_HEADS
    dt = jnp.where(head_ok, _softplus(dt_ref[...] + dtb_ref[...]), 0.0)
    da = dt * (-jnp.exp(alog_ref[...]))
    causal = row >= lane
    tril = jnp.where(causal, 1.0, 0.0).astype(BF16)
    cs = sum(jnp.dot(tril, p, preferred_element_type=F32) for p in _split3(da)) * LOG2E
    hi, mid, lo = _split3(cs)
    p_ref[:, 0:LANES] = jnp.where(head_ok, hi.astype(F32), pltpu.roll(mid.astype(F32), SSM_HEADS, 1)).astype(BF16)
    p_ref[:, LANES:2 * LANES] = lo
    cst = cs.T
    dtt = dt.T
    rowp_ref[...] = cst - jnp.log2(dtt)
    wt_ref[...] = dtt * jnp.exp2(jnp.broadcast_to(cst[:, L - 1:L], (LANES, L)) - cst)
    lane_lo = lane < SSM_HEAD_DIM

    b_off, c_off = D_INNER, D_INNER + SSM_GROUPS * D_STATE
    for g in range(SSM_GROUPS):
        bg = act_ref[:, b_off + g * D_STATE:b_off + (g + 1) * D_STATE]
        bg_bf = bg.astype(BF16)
        bgt = bg.T
        cg = act_ref[:, c_off + g * D_STATE:c_off + (g + 1) * D_STATE].astype(BF16)
        cb = _dot_nt(cg, bg_bf)
        yoff = jnp.dot(cg, statet_ref[g].astype(BF16), preferred_element_type=F32)
        colb = jnp.dot(p_ref[...], e_ref[g], preferred_element_type=F32)
        heads = slice(g * HEADS_PER_GROUP, (g + 1) * HEADS_PER_GROUP)
        rowpg, wtg = rowp_ref[heads, :], wt_ref[heads, :]
        gated = []
        for pr in range(HEADS_PER_GROUP // 2):
            lanes = slice(pr * PAIR, (pr + 1) * PAIR)
            cols = slice(g * GROUP_X + pr * PAIR, g * GROUP_X + (pr + 1) * PAIR)
            x2 = act_ref[:, cols]
            ms, ecols, ss = [], [], []
            for k in (2 * pr, 2 * pr + 1):
                col = colb[:, k * LANES:(k + 1) * LANES]
                rowv = jnp.broadcast_to(rowpg[k:k + 1, :], (L, LANES))
                lmat = jnp.exp2(jnp.where(causal, col - rowv, NEG_BIG))
                ms.append((cb * lmat).astype(BF16))
                ecols.append(jnp.exp2(col))
                ss.append((bgt * jnp.broadcast_to(wtg[k:k + 1, :], (LANES, L))).astype(BF16))
            rhs = jnp.concatenate([jnp.where(lane_lo, x2, 0.0), jnp.where(lane_lo, 0.0, x2)], axis=0).astype(BF16)
            ydiag = jnp.dot(jnp.concatenate(ms, axis=1), rhs, preferred_element_type=F32)
            escale = jnp.where(lane_lo, ecols[0], ecols[1])
            y = ydiag + yoff[:, lanes] * escale + dexp_ref[:, cols] * x2
            gated.append(y * sz_ref[:, cols])
            snew = jnp.dot(jnp.concatenate(ss, axis=1), rhs, preferred_element_type=F32)
            decay = jnp.broadcast_to(escale[L - 1:L, :], (LANES, LANES))
            statet_ref[g, :, lanes] = statet_ref[g, :, lanes] * decay + snew
        sq = functools.reduce(lambda a, b: a + b, [t * t for t in gated])
        inv = lax.rsqrt(jnp.sum(sq, axis=-1, keepdims=True) * (1.0 / GROUP_X) + RMS_EPS)
        for pr, t in enumerate(gated):
            cols = slice(g * GROUP_X + pr * PAIR, g * GROUP_X + (pr + 1) * PAIR)
            g_ref[:, cols] = (t * inv * nw_ref[:, cols]).astype(g_ref.dtype)

    @pl.when(c == pl.num_programs(0) - 1)
    def _():
        for g in range(SSM_GROUPS):
            for q in range(GROUP_X // LANES):
                blk = slice(q * LANES, (q + 1) * LANES)
                hfin_ref[g, blk, :] = statet_ref[g, :, blk].T


def ssd_prompt(sz, xbc, dt_raw, conv_w, conv_b, dt_bias, a_log, d_exp, norm_w):
    s = sz.shape[0]
    L = SSD_CHUNK
    vec = lambda n: pl.BlockSpec((1, n), lambda c: (0, 0))
    return pl.pallas_call(
        _ssd_prompt_body,
        grid=(s // L,),
        in_specs=[
            pl.BlockSpec((L, D_INNER), lambda c: (c, 0)),
            pl.BlockSpec((L, CONV_DIM), lambda c: (c, 0)),
            pl.BlockSpec((L, LANES), lambda c: (c, 0)),
            pl.BlockSpec((CONV_WIDTH, CONV_DIM), lambda c: (0, 0)),
            vec(CONV_DIM), vec(LANES), vec(LANES), vec(D_INNER), vec(D_INNER),
            pl.BlockSpec((SSM_GROUPS, 2 * LANES, HEADS_PER_GROUP * LANES), lambda c: (0, 0, 0)),
        ],
        out_specs=[
            pl.BlockSpec((L, D_INNER), lambda c: (c, 0)),
            pl.BlockSpec((SSM_GROUPS, GROUP_X, D_STATE), lambda c: (0, 0, 0)),
        ],
        out_shape=[
            jax.ShapeDtypeStruct((s, D_INNER), BF16),
            jax.ShapeDtypeStruct((SSM_GROUPS, GROUP_X, D_STATE), F32),
        ],
        scratch_shapes=[
            pltpu.VMEM((L + 2 * SUBLANES, CONV_DIM), F32),
            pltpu.VMEM((L, CONV_DIM), F32),
            pltpu.VMEM((SSM_GROUPS, D_STATE, GROUP_X), F32),
            pltpu.VMEM((L, 2 * LANES), BF16),
            pltpu.VMEM((LANES, L), F32),
            pltpu.VMEM((LANES, L), F32),
        ],
        compiler_params=_cparams(("arbitrary",)),
        name="ssd_prompt",
    )(sz, xbc, dt_raw, conv_w, conv_b, dt_bias, a_log, d_exp, norm_w, _head_spread_matrix())


def _decode_conv_body(s0_ref, s1_ref, s2_ref, xn_ref, cw_ref, cb_ref, o_ref):
    acc = cb_ref[...] + cw_ref[0:1, :] * s0_ref[...] + cw_ref[1:2, :] * s1_ref[...]
    acc = acc + cw_ref[2:3, :] * s2_ref[...] + cw_ref[3:4, :] * xn_ref[...]
    o_ref[...] = _silu(acc)


def decode_conv(s0, s1, s2, xnew, conv_w, conv_b):
    b, n = xnew.shape
    blk = pl.BlockSpec((b, CONV_SLAB), lambda j: (0, j))
    return pl.pallas_call(
        _decode_conv_body,
        grid=(n // CONV_SLAB,),
        in_specs=[blk, blk, blk, blk, pl.BlockSpec((CONV_WIDTH, CONV_SLAB), lambda j: (0, j)),
                  pl.BlockSpec((1, CONV_SLAB), lambda j: (0, j))],
        out_specs=blk,
        out_shape=jax.ShapeDtypeStruct((b, n), F32),
        compiler_params=_cparams(("arbitrary",)),
        name="decode_conv",
    )(s0, s1, s2, xnew, conv_w, conv_b)


def _decode_prep_body(xs_ref, b_ref, c_ref, dt_ref, dtb_ref, alog_ref, e_ref, dtx_ref, daexp_ref, bcx_ref):
    lane = lax.broadcasted_iota(jnp.int32, dt_ref.shape, 1)
    dt = jnp.where(lane < SSM_HEADS, _softplus(dt_ref[...] + dtb_ref[...]), 0.0)
    da = jnp.where(lane < SSM_HEADS, jnp.exp(dt * (-jnp.exp(alog_ref[...]))), 0.0)
    e = e_ref[...]
    expand = lambda v: sum(jnp.dot(p, e, preferred_element_type=F32) for p in _split3(v))
    dtx = expand(dt) * xs_ref[...]
    dtx_ref[...] = dtx
    daexp_ref[...] = expand(da)
    bcx_ref[...] = dtx * jnp.sum(b_ref[...] * c_ref[...], axis=-1, keepdims=True)


def decode_prep(xbc_act, dt_raw, dt_bias, a_log, expand_mat):
    b = xbc_act.shape[0]
    xblk = pl.BlockSpec((b, GROUP_X), lambda g: (0, g))
    vec = pl.BlockSpec((1, LANES), lambda g: (0, 0))
    nb = D_INNER // D_STATE
    out = jax.ShapeDtypeStruct((b, D_INNER), F32)
    return pl.pallas_call(
        _decode_prep_body,
        grid=(SSM_GROUPS,),
        in_specs=[
            xblk,
            pl.BlockSpec((b, D_STATE), lambda g: (0, nb + g)),
            pl.BlockSpec((b, D_STATE), lambda g: (0, nb + SSM_GROUPS + g)),
            pl.BlockSpec((b, LANES), lambda g: (0, 0)),
            vec, vec,
            pl.BlockSpec((LANES, GROUP_X), lambda g: (0, g)),
        ],
        out_specs=[xblk, xblk, xblk],
        out_shape=[out, out, out],
        compiler_params=_cparams(("arbitrary",)),
        name="decode_prep",
    )(xbc_act, xbc_act, xbc_act, dt_raw, dt_bias, a_log, expand_mat)


TILE_ROWS = 128
SPLIT_ROWS = 16


SEQS_PER_STEP = 4


STATE_ROWS = SSM_HEADS * SSM_HEAD_DIM
STATE_SEQS_PER_STEP = 4


def _decode_state_body(h_ref, dtx_ref, da_ref, b_ref, c_ref, ho_ref, hc_ref):
    rid = lambda w: lax.broadcasted_iota(jnp.int32, (SPLIT_ROWS, w), 0)
    pick = lambda r, pieces, w: functools.reduce(
        lambda acc, kv: jnp.where(r == kv[0], jnp.broadcast_to(kv[1].astype(F32), (SPLIT_ROWS, w)), acc),
        pieces, jnp.zeros((SPLIT_ROWS, w), F32))
    r = rid(D_INNER)
    rb = rid(SSM_GROUPS * D_STATE)
    r1 = rid(D_STATE)
    rd = jnp.where(jnp.logical_and(r1 >= 6, r1 <= 8), 1.0, 0.0).astype(BF16)
    for q in range(STATE_SEQS_PER_STEP):
        xh, xm, xl = _split3(dtx_ref[q])
        ah, am, al = _split3(da_ref[q])
        bh, bm, bl = _split3(b_ref[q])
        lmat = pick(r, [(0, xh), (1, xh), (2, xh), (3, xm), (4, xm), (5, xl), (6, ah), (7, am), (8, al)], D_INNER)
        ru = pick(rb, [(0, bh), (1, bm), (2, bl), (3, bh), (4, bm), (5, bh)], SSM_GROUPS * D_STATE).astype(BF16)
        crow = c_ref[q]
        for i in range(STATE_ROWS // TILE_ROWS):
            g = (i * TILE_ROWS) // GROUP_X
            lanes = slice(i * TILE_ROWS, (i + 1) * TILE_ROWS)
            lt = lmat[:, lanes].T.astype(BF16)
            upd = jnp.dot(lt, ru[:, g * D_STATE:(g + 1) * D_STATE], preferred_element_type=F32)
            dec = jnp.dot(lt, rd, preferred_element_type=F32)
            h = h_ref[q, lanes, :]
            ho_ref[q, lanes, :] = h * dec + upd
            t = (h * crow[:, g * D_STATE:(g + 1) * D_STATE]).T
            hc_ref[q, :, lanes] = jnp.sum(t, axis=0, keepdims=True)


def decode_state(h, dtx, daexp, xbc_act3):
    b = h.shape[0]
    nq = STATE_SEQS_PER_STEP
    hblk = pl.BlockSpec((nq, STATE_ROWS, D_STATE), lambda i: (i, 0, 0))
    rblk = pl.BlockSpec((nq, 1, D_INNER), lambda i: (i, 0, 0))
    bc = SSM_GROUPS * D_STATE
    return pl.pallas_call(
        _decode_state_body,
        grid=(b // nq,),
        in_specs=[
            hblk, rblk, rblk,
            pl.BlockSpec((nq, 1, bc), lambda i: (i, 0, D_INNER // bc)),
            pl.BlockSpec((nq, 1, bc), lambda i: (i, 0, D_INNER // bc + 1)),
        ],
        out_specs=[hblk, rblk],
        out_shape=[jax.ShapeDtypeStruct((b, STATE_ROWS, D_STATE), F32), jax.ShapeDtypeStruct((b, 1, D_INNER), F32)],
        compiler_params=_cparams(("arbitrary",)),
        name="decode_state",
    )(h, dtx, daexp, xbc_act3, xbc_act3)


def _decode_post_body(hc_ref, da_ref, bcx_ref, xs_ref, sz_ref, dexp_ref, nw_ref, o_ref):
    y = da_ref[...] * hc_ref[...] + bcx_ref[...] + dexp_ref[...] * xs_ref[...]
    gt = y * sz_ref[...]
    ms = jnp.mean(gt * gt, axis=-1, keepdims=True)
    o_ref[...] = (gt * lax.rsqrt(ms + RMS_EPS) * nw_ref[...]).astype(o_ref.dtype)


def decode_post(hc, daexp, bcx, xbc_act, sz, d_exp, norm_w):
    b = hc.shape[0]
    xblk = pl.BlockSpec((b, GROUP_X), lambda g: (0, g))
    vblk = pl.BlockSpec((1, GROUP_X), lambda g: (0, g))
    return pl.pallas_call(
        _decode_post_body,
        grid=(SSM_GROUPS,),
        in_specs=[xblk, xblk, xblk, xblk, xblk, vblk, vblk],
        out_specs=xblk,
        out_shape=jax.ShapeDtypeStruct((b, D_INNER), BF16),
        compiler_params=_cparams(("arbitrary",)),
        name="decode_post",
    )(hc, daexp, bcx, xbc_act, sz, d_exp, norm_w)


SUPER = WINDOW_MAX
ATT_N = 128
MERGE_ROWS = 256


def _rel_bucket(dist):
    max_exact = NUM_BUCKETS // 2
    df = jnp.maximum(dist, max_exact).astype(F32)
    large = max_exact + (jnp.log(df / max_exact) / math.log(MAX_DISTANCE / max_exact)
                         * (NUM_BUCKETS - max_exact)).astype(jnp.int32)
    return jnp.where(dist < max_exact, dist, jnp.minimum(large, NUM_BUCKETS - 1))


def _prompt_bias_rows(rel_bias):
    n = ATT_N
    back = n - jnp.arange(2 * n)
    rows = []
    for g, (_, dil) in enumerate(DILATION_GROUPS):
        tab = rel_bias[:, g * N_HEADS_B:(g + 1) * N_HEADS_B]
        bias = tab[_rel_bucket(jnp.clip(back, 0, n) * dil)].astype(F32) * LOG2E
        rows.append(jnp.where((back >= 0)[:, None], bias, NEG_BIG).T)
    return jnp.stack(rows, axis=0)[:, :, None, :]


def _rows(start, size, stride):
    return pl.ds(start, size) if stride == 1 else pl.ds(start, size, stride=stride)


def _attn_prompt_body(q0_ref, q1_ref, q2_ref, kc_ref, kp_ref, vc_ref, vp_ref, t_ref, o_ref, og_ref, lse_ref):
    first = pl.program_id(0) == 0
    n = ATT_N
    scale = HEAD_DIM ** -0.5 * LOG2E
    key_id = lax.broadcasted_iota(jnp.int32, (n, 2 * n), 1)
    q_refs = (q0_ref, q1_ref, q2_ref)
    for g, (win, dil) in enumerate(DILATION_GROUPS):
        nblk = SUPER // (n * dil)
        bias = pltpu.roll(jnp.broadcast_to(t_ref[g, 0], (n, 2 * n)), 0, 1, stride=1, stride_axis=0)
        for r in range(dil):
            prev = _rows((nblk - 1) * n * dil + r, n, dil)
            kprev, vprev = kp_ref[prev, :].astype(BF16), vp_ref[prev, :].astype(BF16)
            for c in range(nblk):
                start = c * n * dil + r
                cur = _rows(start, n, dil)
                qb = q_refs[g][cur, :].astype(BF16)
                kcur, vcur = kc_ref[cur, :].astype(BF16), vc_ref[cur, :].astype(BF16)
                keys = jnp.concatenate([kprev, kcur], axis=0)
                vals = jnp.concatenate([vprev, vcur], axis=0)
                kprev, vprev = kcur, vcur
                s = _dot_nt(qb, keys) * scale + bias
                if c == 0:
                    s = jnp.where(jnp.logical_and(first, key_id < n), NEG_BIG, s)
                m = jnp.max(s, axis=-1, keepdims=True)
                p = jnp.exp2(s - m)
                den = jnp.sum(p, axis=-1, keepdims=True)
                o = jnp.dot(p.astype(BF16), vals, preferred_element_type=F32) * (1.0 / den)
                og_ref[g, cur, :] = o
                lse_ref[g, cur, :] = jnp.broadcast_to(m + jnp.log2(den), (n, LANES))
    for i in range(SUPER // MERGE_ROWS):
        rows = slice(i * MERGE_ROWS, (i + 1) * MERGE_ROWS)
        ls = [lse_ref[g, rows, :] for g in range(N_GROUPS_B)]
        mx = functools.reduce(jnp.maximum, ls)
        es = [jnp.exp2(l - mx) for l in ls]
        tot = functools.reduce(lambda a, b: a + b, es)
        acc = functools.reduce(lambda a, b: a + b, [es[g] * og_ref[g, rows, :] for g in range(N_GROUPS_B)])
        o_ref[rows, :] = (acc * (1.0 / tot)).astype(o_ref.dtype)


def attn_prompt(q, kv, bias_rows):
    s = q.shape[0]
    blk = lambda f: pl.BlockSpec((SUPER, HEAD_DIM), f)
    return pl.pallas_call(
        _attn_prompt_body,
        grid=(s // SUPER, N_HEADS_B),
        in_specs=[
            blk(lambda c, h: (c, h)),
            blk(lambda c, h: (c, N_HEADS_B + h)),
            blk(lambda c, h: (c, 2 * N_HEADS_B + h)),
            blk(lambda c, h: (c, h)),
            blk(lambda c, h: (jnp.maximum(c - 1, 0), h)),
            blk(lambda c, h: (c, N_HEADS_B + h)),
            blk(lambda c, h: (jnp.maximum(c - 1, 0), N_HEADS_B + h)),
            pl.BlockSpec((N_GROUPS_B, 1, 1, 2 * ATT_N), lambda c, h: (0, h, 0, 0)),
        ],
        out_specs=blk(lambda c, h: (c, h)),
        out_shape=jax.ShapeDtypeStruct((s, KV_DIM), BF16),
        scratch_shapes=[
            pltpu.VMEM((N_GROUPS_B, SUPER, HEAD_DIM), F32),
            pltpu.VMEM((N_GROUPS_B, SUPER, LANES), F32),
        ],
        compiler_params=_cparams(("arbitrary", "arbitrary")),
        name="attn_prompt",
    )(q, q, q, kv, kv, kv, kv, bias_rows)


N_BUF = min(WINDOW_MAX, PAST_LEN)
ROW_MAJOR = 16
ROW_MID = 4


def _decode_bias_tables(rel_bias):
    n = ATT_N
    j = n - jnp.arange(n)
    cached, new = [], []
    for g, (_, dil) in enumerate(DILATION_GROUPS):
        tab = rel_bias[:, g * N_HEADS_B:(g + 1) * N_HEADS_B]
        cached.append(tab[_rel_bucket(j * dil)])
        new.append(tab[_rel_bucket(jnp.zeros((), jnp.int32))])
    cached = jnp.broadcast_to(jnp.stack(cached)[..., None] * LOG2E, (N_GROUPS_B, n, N_HEADS_B, LANES)).astype(F32)
    new = jnp.broadcast_to(jnp.stack(new)[..., None] * LOG2E, (N_GROUPS_B, N_HEADS_B, LANES)).astype(F32)
    return cached, new


def _attn_decode_body(q_ref, kvn_ref, k0_ref, v0_ref, k1_ref, v1_ref, k2_ref, v2_ref, bc_ref, bn_ref, o_ref):
    n = ATT_N
    scale = HEAD_DIM ** -0.5 * LOG2E
    rows = lambda x: x.reshape((-1, N_HEADS_B, HEAD_DIM))

    def windows(r0, r1, r2, b):
        w0 = r0[b]
        w1 = r1[b]
        return (rows(w0),
                jnp.concatenate([rows(w1), rows(w0[:, :, 0])], axis=0),
                jnp.concatenate([r2[b], w1[:, 0], w0[:, 0, 0]], axis=0))

    for b in range(SEQS_PER_STEP):
        kn, vn = kvn_ref[b, 0], kvn_ref[b, 1]
        ks = windows(k0_ref, k1_ref, k2_ref, b)
        vs = windows(v0_ref, v1_ref, v2_ref, b)
        outs, lses = [], []
        for g in range(N_GROUPS_B):
            q = q_ref[b, g] * scale
            k, v = ks[g], vs[g]
            s = jnp.sum(k * q[None], axis=-1, keepdims=True) + bc_ref[g]
            s0 = jnp.sum(kn * q, axis=-1, keepdims=True) + bn_ref[g]
            m = jnp.maximum(jnp.max(s, axis=0), s0)
            p = jnp.exp2(s - m[None])
            p0 = jnp.exp2(s0 - m)
            den = jnp.sum(p, axis=0) + p0
            outs.append((jnp.sum(p * v, axis=0) + p0 * vn) * (1.0 / den))
            lses.append(m + jnp.log2(den))
        mx = functools.reduce(jnp.maximum, lses)
        es = [jnp.exp2(l - mx) for l in lses]
        tot = functools.reduce(lambda x, y: x + y, es)
        acc = functools.reduce(lambda x, y: x + y, [e * o for e, o in zip(es, outs)])
        o_ref[b] = (acc * (1.0 / tot)).astype(o_ref.dtype)


def attn_decode(q, kv_new, cache_k, cache_v, bias_cached, bias_new):
    b = q.shape[0]
    view = (b, N_BUF // ROW_MAJOR, ROW_MAJOR // ROW_MID, ROW_MID, N_HEADS_B, HEAD_DIM)
    ck, cv = cache_k.reshape(view), cache_v.reshape(view)
    n = ATT_N
    nq = SEQS_PER_STEP
    hd = (N_HEADS_B, HEAD_DIM)
    m0 = n // ROW_MAJOR
    m1 = (n * ROW_MID - n) // ROW_MAJOR
    m2 = N_BUF // ROW_MAJOR - m0 - m1
    assert m2 % m1 == 0 and (m1 + m2) % m0 == 0
    g0 = pl.BlockSpec((nq, m0, ROW_MAJOR // ROW_MID, ROW_MID) + hd, lambda i: (i, (m1 + m2) // m0, 0, 0, 0, 0))
    g1 = pl.BlockSpec((nq, m1, ROW_MAJOR // ROW_MID, None) + hd, lambda i: (i, m2 // m1, 0, 0, 0, 0))
    g2 = pl.BlockSpec((nq, m2, None, None) + hd, lambda i: (i, 0, 0, 0, 0, 0))
    return pl.pallas_call(
        _attn_decode_body,
        grid=(b // nq,),
        in_specs=[
            pl.BlockSpec((nq, N_GROUPS_B) + hd, lambda i: (i, 0, 0, 0)),
            pl.BlockSpec((nq, 2) + hd, lambda i: (i, 0, 0, 0)),
            g0, g0, g1, g1, g2, g2,
            pl.BlockSpec((N_GROUPS_B, n) + (N_HEADS_B, LANES), lambda i: (0, 0, 0, 0)),
            pl.BlockSpec((N_GROUPS_B, N_HEADS_B, LANES), lambda i: (0, 0, 0)),
        ],
        out_specs=pl.BlockSpec((nq,) + hd, lambda i: (i, 0, 0)),
        out_shape=jax.ShapeDtypeStruct((b,) + hd, BF16),
        compiler_params=_cparams(("arbitrary",)),
        name="attn_decode",
    )(q, kv_new, ck, cv, ck, cv, ck, cv, bias_cached, bias_new)


Z_TILE = 1024


def _pad_lanes(v):
    return jnp.pad(v.astype(F32), (0, LANES - v.shape[0]))[None]


TM = 1024
TM_LONG_K = 512
TM_GLU = 2048
TM_EW = 512
TM_EW_WIDE = 1024


def _trunk(xs, adas, mixers_a, mixers_b, p):
    tms = (TM_EW, xs[1].shape[0])
    vec = lambda a, l, s: a[l, s][None]
    sh1, sc1, g1, sh2, sc2, g2 = zip(*(a[0] for a in adas))
    sh1b, sc1b, g1b, sh2b, sc2b, g2b = zip(*(a[1] for a in adas))
    shkv, sckv = zip(*(a[2] for a in adas))
    paths = range(2)

    def ln(x, y, gate, l, s, mods):
        tile = (TM_EW_WIDE if not mods else TM_EW, tms[1])
        outs = [post_ln(x[i], y[i], gate[i], vec(p["ln_g"], l, s), vec(p["ln_b"], l, s),
                        [(sc[i], sh[i]) for sc, sh in mods], tm=tile[i]) for i in paths]
        return zip(*outs)

    def mm(a, w, **kw):
        tm = kw.pop("tm", TM)
        return matmul(a[0], w, rider=a[1], tm=tm, **kw)

    u = [modulate(xs[i], sc1[i], sh1[i], tm=(TM_EW_WIDE, tms[1])[i]) for i in paths]
    w_in_t = jnp.swapaxes(p["ssm_w_in"], 1, 2).reshape(-1, D_MODEL)
    sz = mm(u, w_in_t, w_transposed=True, out_act="silu", tn=Z_TILE, n_out=D_INNER, name="in_z")
    xbc = mm(u, w_in_t, w_transposed=True, tn=Z_TILE, n_out=CONV_DIM, col_off=D_INNER // Z_TILE, name="in_xbc")
    dt = mm(u, w_in_t, w_transposed=True, tn=LANES, n_out=LANES, col_off=(D_INNER + CONV_DIM) // LANES, name="in_dt")
    gated, extras = zip(*(mixers_a[i](sz[i], xbc[i], dt[i]) for i in paths))
    y = mm(gated, p["ssm_w_out"].reshape(D_INNER, D_MODEL), tn=512, out_dtype=BF16, name="ssm_out")
    x, u = ln(xs, y, g1, 0, 0, [(sc2, sh2)])
    h = glu_matmul(u[0], u[1], p["ffn_w_gate"], p["ffn_w_up"], layer=0, tm=TM_GLU, tn=512, name="ffn0_glu")
    y = mm(h, p["ffn_w_down"], layer=0, tm=TM_LONG_K, tn=512, out_dtype=BF16, name="ffn0_down")
    x, u, ukv = ln(x, y, g2, 0, 1, [(sc1b, sh1b), (sckv, shkv)])

    kv = mm(ukv, p["w_kv"], tn=1024, name="kv_proj")
    q = mm(u, p["attn_w_q"], layer=0, tn=1024, name="q_proj")
    o = [mixers_b[i](q[i], kv[i]) for i in paths]
    y = mm(o, p["attn_w_o"], layer=0, tn=1024, out_dtype=BF16, name="attn_out")
    x, u = ln(x, y, g1b, 1, 0, [(sc2b, sh2b)])
    h = glu_matmul(u[0], u[1], p["ffn_w_gate"], p["ffn_w_up"], layer=1, tm=TM_GLU, tn=512, name="ffn1_glu")
    y = mm(h, p["ffn_w_down"], layer=1, tm=TM_LONG_K, tn=512, out_dtype=BF16, name="ffn1_down")
    (x,) = ln(x, y, g2b, 1, 1, [])
    return [(x[i], extras[i], kv[i]) for i in paths]


def kernel(x_prompt, x_sample, state_conv, state_ssm, cache_k, cache_v, c_prompt, c_sample, w_ada, b_ada, ln_g, ln_b, ffn_w_gate, ffn_w_up, ffn_w_down, ssm_w_in, ssm_conv_w, ssm_conv_b, ssm_dt_bias, ssm_a_log, ssm_d, ssm_norm_w, ssm_w_out, kv_w_ada, kv_b_ada, w_kv, attn_w_q, attn_w_o, rel_bias):
    p = dict(ln_g=ln_g, ln_b=ln_b, ffn_w_gate=ffn_w_gate, ffn_w_up=ffn_w_up, ffn_w_down=ffn_w_down,
             ssm_w_in=ssm_w_in, ssm_w_out=ssm_w_out, w_kv=w_kv, attn_w_q=attn_w_q, attn_w_o=attn_w_o)
    s, b, d = SEQ, DEC_BATCH, D_MODEL

    c_all = jnp.concatenate([c_sample, c_prompt, jnp.zeros((ADA_ROWS - b - 1, d), F32)], axis=0)
    ada = [matmul(c_all, w_ada, layer=l, tm=ADA_ROWS, tn=1024, bias=b_ada[l][None], act="silu", name=f"ada{l}")
           for l in range(DEPTH)]
    adakv = matmul(c_all, kv_w_ada, tm=ADA_ROWS, tn=1024, bias=kv_b_ada[None], act="silu", name="ada_kv")
    split = lambda table, n, prompt: [Mod(table, i, prompt) for i in range(n)]

    conv_w, conv_b = ssm_conv_w[0], ssm_conv_b[0][None]
    dt_bias, a_log = _pad_lanes(ssm_dt_bias[0]), _pad_lanes(ssm_a_log[0])
    d_exp = jnp.repeat(ssm_d[0].astype(F32), SSM_HEAD_DIM)[None]
    norm_w = ssm_norm_w[0][None]

    def mixer_a_prompt(sz, xbc, dt):
        gated, hfin = ssd_prompt(sz, xbc, dt, conv_w, conv_b, dt_bias, a_log, d_exp, norm_w)
        return gated, (xbc[s - (CONV_WIDTH - 1):], hfin)

    bias_rows = _prompt_bias_rows(rel_bias)
    mixer_b_prompt = lambda q, kv: attn_prompt(q, kv, bias_rows)

    expand_mat = (lax.broadcasted_iota(jnp.int32, (LANES, D_INNER), 0)
                  == lax.broadcasted_iota(jnp.int32, (LANES, D_INNER), 1) // SSM_HEAD_DIM).astype(BF16)

    def mixer_a_decode(sz, xbc, dt):
        st = state_conv[0]
        xa = decode_conv(st[:, 0], st[:, 1], st[:, 2], xbc, conv_w, conv_b)
        dtx, daexp, bcx = decode_prep(xa, dt, dt_bias, a_log, expand_mat)
        hnew, hc = decode_state(state_ssm.reshape(b, STATE_ROWS, D_STATE), dtx[:, None], daexp[:, None], xa[:, None])
        gated = decode_post(hc[:, 0], daexp, bcx, xa, sz, d_exp, norm_w)
        new_conv = jnp.concatenate([st[:, 1:], xbc[:, None]], axis=1)
        return gated, (new_conv, hnew)

    bias_cached, bias_new = _decode_bias_tables(rel_bias)

    def mixer_b_decode(q, kv):
        o = attn_decode(q.reshape(b, N_GROUPS_B, N_HEADS_B, HEAD_DIM), kv.reshape(b, 2, N_HEADS_B, HEAD_DIM),
                        cache_k, cache_v, bias_cached, bias_new)
        return o.reshape(b, KV_DIM)

    adas = [(split(ada[0], 6, prompt), split(ada[1], 6, prompt), split(adakv, 2, prompt)) for prompt in (True, False)]
    (y_p, (conv_p, ssm_p), kv_p), (y_s, (conv_s, ssm_s), kv_s) = _trunk(
        (x_prompt.reshape(s, d), x_sample.reshape(b, d)), adas,
        (mixer_a_prompt, mixer_a_decode), (mixer_b_prompt, mixer_b_decode), p)

    n_keep = min(WINDOW_MAX, s)
    heads = (N_HEADS_B, HEAD_DIM)
    return (
        y_p.reshape(1, s, d),
        y_s.reshape(b, 1, d),
        conv_p.reshape(1, 1, CONV_WIDTH - 1, CONV_DIM),
        ssm_p.reshape(1, 1, SSM_HEADS, SSM_HEAD_DIM, D_STATE),
        kv_p[s - n_keep:, :KV_DIM].reshape((1, n_keep) + heads),
        kv_p[s - n_keep:, KV_DIM:].reshape((1, n_keep) + heads),
        conv_s.reshape(1, b, CONV_WIDTH - 1, CONV_DIM),
        ssm_s.reshape(1, b, SSM_HEADS, SSM_HEAD_DIM, D_STATE),
        kv_s[:, :KV_DIM].reshape((b, 1) + heads),
        kv_s[:, KV_DIM:].reshape((b, 1) + heads),
    )
```
